```python
import math
import jax, jax.numpy as jnp
from jax import lax
import numpy as np

D_MODEL = 1024
BATCH = 2
SEQ = 8192
DEPTH = 2

HEAD_DIM = 64
N_RET = 6
N_NSA = 6
N_KV = 2
HPG = N_NSA // N_KV
N_GM = 4
GM_DIM = 64
RET_W = N_RET * HEAD_DIM
NSA_W = N_NSA * HEAD_DIM
GM_W = N_GM * GM_DIM
MIX_W = RET_W + NSA_W + GM_W
RET_CHUNK = 128
CMP_LEN = 32
CMP_STRIDE = 16
SEL_BLOCK = 64
N_SEL = 16
WINDOW = 512
Q_BLOCK = 128
GM_CHUNK = 128
N_BUCKETS = 32
MAX_DISTANCE = 128
ROPE_THETA = 10000.0
D_FF = -(-8 * D_MODEL // (3 * 256)) * 256
SPLIT_SIZES = (RET_W, RET_W, RET_W, RET_W, NSA_W, 3 * 2 * N_KV * HEAD_DIM, 3 * N_NSA, 2 * GM_W)
IN_W = sum(SPLIT_SIZES)
EPS = 1e-6
BIG = 1e9
NEG = -1e30

kernel_name = "hybrid_retention_nsa_gmlp_block"


def rms_norm(x, gain=None):
    xf = x.astype(jnp.float32)
    y = xf * lax.rsqrt(jnp.mean(xf * xf, axis=-1, keepdims=True) + EPS)
    if gain is not None:
        y = y * gain.astype(jnp.float32)
    return y.astype(x.dtype)


def rope(x, pos):
    half = x.shape[-1] // 2
    inv = ROPE_THETA ** (-jnp.arange(half, dtype=jnp.float32) / half)
    ang = pos.astype(jnp.float32)[:, None] * inv[None, :]
    cos = jnp.cos(ang)[None, :, None, :]
    sin = jnp.sin(ang)[None, :, None, :]
    x1 = x[..., :half].astype(jnp.float32)
    x2 = x[..., half:].astype(jnp.float32)
    return jnp.concatenate([x1 * cos - x2 * sin, x2 * cos + x1 * sin], axis=-1).astype(x.dtype)


def t5_bucket(dist):
    n = jnp.maximum(dist, 0)
    max_exact = N_BUCKETS // 2
    nf = jnp.maximum(n, 1).astype(jnp.float32)
    large = max_exact + (jnp.log(nf / max_exact) / math.log(MAX_DISTANCE / max_exact)
                         * (N_BUCKETS - max_exact)).astype(jnp.int32)
    large = jnp.minimum(large, N_BUCKETS - 1)
    return jnp.where(n < max_exact, n, large)


def masked_softmax(logits, mask):
    s = jnp.where(mask, logits.astype(jnp.float32), NEG)
    return jax.nn.softmax(s, axis=-1) * mask


def retention(q, k, v, g):
    B, T, H, d = q.shape
    C = RET_CHUNK
    nc = T // C
    f32 = jnp.float32
    to_chunks = lambda a: a.astype(f32).reshape(B, nc, C, H, d).transpose(0, 3, 1, 2, 4)
    qc, kc, vc = to_chunks(q), to_chunks(k * d ** -0.5), to_chunks(v)
    log_gamma = jnp.log(1.0 - 2.0 ** (-5.0 - jnp.arange(H, dtype=f32)))
    idx = jnp.arange(C, dtype=f32)
    diff = idx[:, None] - idx[None, :]
    decay = jnp.where(diff >= 0, jnp.exp(jnp.maximum(diff, 0.0)[None] * log_gamma[:, None, None]), 0.0)
    scores = jnp.einsum('bhcnd,bhcmd->bhcnm', qc, kc) * decay[:, None]
    inner = jnp.einsum('bhcnm,bhcme->bhcne', scores, vc)
    zeta = jnp.exp((C - 1 - idx)[None, :] * log_gamma[:, None])
    xi = jnp.exp((idx + 1)[None, :] * log_gamma[:, None])
    chunk_decay = jnp.exp(C * log_gamma)

    def step(R, kv):
        kt, vt = kv
        R_new = chunk_decay[None, :, None, None] * R + jnp.einsum('bhmd,bhme->bhde', kt * zeta[None, :, :, None], vt)
        return R_new, R

    R0 = jnp.zeros((B, H, d, d), f32)
    _, R_prev = lax.scan(step, R0, (kc.transpose(2, 0, 1, 3, 4), vc.transpose(2, 0, 1, 3, 4)))
    cross = jnp.einsum('bhcnd,cbhde->bhcne', qc, R_prev) * xi[None, :, None, :, None]
    o = (inner + cross).transpose(0, 2, 3, 1, 4).reshape(B, T, H, d)
    o = rms_norm(o) * jax.nn.silu(g.astype(f32))
    return o.reshape(B, T, H * d).astype(q.dtype)


def nsa(q, kv, gates, q_gain, k_gain, cmp_pe, cmp_w1, cmp_w2, rel_bias):
    B, T, _, d = q.shape
    out_dtype = q.dtype
    q = rms_norm(q, q_gain) * d ** -0.5
    q = q.reshape(B, T, N_KV, HPG, d).transpose(0, 2, 3, 1, 4)
    kv = kv.transpose(2, 3, 0, 4, 1, 5)

    n_cmp = (T - CMP_LEN) // CMP_STRIDE + 1
    blk_idx = np.arange(n_cmp)[:, None] * CMP_STRIDE + np.arange(CMP_LEN)[None, :]

    def compress(a, j):
        blocks = a[:, :, blk_idx] + cmp_pe[j]
        flat = blocks.reshape(B, N_KV, n_cmp, CMP_LEN * d)
        return jax.nn.gelu(flat @ cmp_w1[j]) @ cmp_w2[j]

    k_cmp = rms_norm(compress(kv[0, 0], 0), k_gain[0])
    v_cmp = compress(kv[0, 1], 1)
    cmp_end = jnp.asarray(blk_idx[:, -1], jnp.int32)

    n_slc = T // SEL_BLOCK
    n_sel = min(N_SEL, n_slc)
    k_sel = rms_norm(kv[1, 0], k_gain[1]).reshape(B, N_KV, n_slc, SEL_BLOCK, d)
    v_sel = kv[1, 1].reshape(B, N_KV, n_slc, SEL_BLOCK, d)
    cs = np.arange(n_cmp)[:, None] * CMP_STRIDE
    ss = np.arange(n_slc)[None, :] * SEL_BLOCK
    overlap = np.clip(np.minimum(cs + CMP_LEN, ss + SEL_BLOCK) - np.maximum(cs, ss), 0, None) // CMP_STRIDE
    overlap = jnp.asarray(overlap, jnp.float32)

    pad = ((0, 0), (0, 0), (WINDOW, 0), (0, 0))
    k_win = jnp.pad(rms_norm(kv[2, 0], k_gain[2]), pad)
    v_win = jnp.pad(kv[2, 1], pad)

    gates_t = jax.nn.sigmoid(gates.astype(jnp.float32)).reshape(B, T, N_KV, HPG, 3).transpose(0, 2, 3, 1, 4)
    bias_tab = rel_bias.reshape(N_BUCKETS, N_KV, HPG)
    bi = jnp.arange(B)[:, None, None, None]
    gi = jnp.arange(N_KV)[None, :, None, None]
    K_SEL = n_sel * SEL_BLOCK

    def block(i):
        q0 = i * Q_BLOCK
        qb = lax.dynamic_slice_in_dim(q, q0, Q_BLOCK, axis=3)
        t = q0 + jnp.arange(Q_BLOCK, dtype=jnp.int32)

        dist_c = t[:, None] - cmp_end[None, :]
        bias_c = bias_tab[t5_bucket(dist_c)].transpose(2, 3, 0, 1)
        s_c = jnp.einsum('bghqd,bgnd->bghqn', qb, k_cmp) + bias_c
        p_c = masked_softmax(s_c, dist_c >= 0)
        o_c = jnp.einsum('bghqn,bgnd->bghqd', p_c, v_cmp)

        imp = jnp.einsum('bghqn,nj->bgqj', p_c, overlap)
        cur = (t // SEL_BLOCK)[:, None]
        j = jnp.arange(n_slc)[None, :]
        imp = jnp.where((j == 0) | (j == cur) | (j == cur - 1), BIG, imp)
        imp = jnp.where(j > cur, -BIG, imp)
        _, sel = lax.top_k(imp, n_sel)
        ks = k_sel[bi, gi, sel].reshape(B, N_KV, Q_BLOCK, K_SEL, d)
        vs = v_sel[bi, gi, sel].reshape(B, N_KV, Q_BLOCK, K_SEL, d)
        pos = sel[..., None] * SEL_BLOCK + jnp.arange(SEL_BLOCK, dtype=jnp.int32)
        dist_s = (t[None, None, :, None, None] - pos).reshape(B, N_KV, Q_BLOCK, K_SEL)
        bias_s = bias_tab[t5_bucket(dist_s), gi].transpose(0, 1, 4, 2, 3)
        s_s = jnp.einsum('bghqd,bgqkd->bghqk', qb, ks) + bias_s
        p_s = masked_softmax(s_s, (dist_s >= 0)[:, :, None])
        o_s = jnp.einsum('bghqk,bgqkd->bghqd', p_s, vs)

        kw = lax.dynamic_slice_in_dim(k_win, q0, WINDOW + Q_BLOCK, axis=2)
        vw = lax.dynamic_slice_in_dim(v_win, q0, WINDOW + Q_BLOCK, axis=2)
        s_pos = q0 - WINDOW + jnp.arange(WINDOW + Q_BLOCK, dtype=jnp.int32)
        dist_w = t[:, None] - s_pos[None, :]
        mask_w = (dist_w >= 0) & (dist_w < WINDOW) & (s_pos[None, :] >= 0)
        bias_w = bias_tab[t5_bucket(dist_w)].transpose(2, 3, 0, 1)
        s_w = jnp.einsum('bghqd,bgkd->bghqk', qb, kw) + bias_w
        p_w = masked_softmax(s_w, mask_w)
        o_w = jnp.einsum('bghqk,bgkd->bghqd', p_w, vw)

        g = lax.dynamic_slice_in_dim(gates_t, q0, Q_BLOCK, axis=3)
        return g[..., 0:1] * o_c + g[..., 1:2] * o_s + g[..., 2:3] * o_w

    out = lax.map(block, jnp.arange(T // Q_BLOCK, dtype=jnp.int32))
    return out.transpose(1, 0, 4, 2, 3, 5).reshape(B, T, N_NSA * d).astype(out_dtype)


def spatial_gating(z, ws, b):
    B, T, _ = z.shape
    nch = T // GM_CHUNK
    z = jax.nn.gelu(z)
    u, v = jnp.split(z, 2, axis=-1)
    v = rms_norm(v.reshape(B, nch, GM_CHUNK, N_GM, GM_DIM))
    causal = jnp.tril(jnp.ones((GM_CHUNK, GM_CHUNK), ws.dtype))
    sv = jnp.einsum('gts,bcsgd->bctgd', ws * causal, v) + b.T[None, None, :, :, None]
    return (u.reshape(B, nch, GM_CHUNK, N_GM, GM_DIM) * sv).reshape(B, T, GM_W)


def setup_inputs(seed: int = 0) -> dict:
    key = jax.random.key(seed)
    ks = jax.random.split(key, 16)
    f32 = jnp.float32
    nrm = lambda k, shape, scale: jax.random.normal(k, shape, f32) * scale
    return {
        "x": nrm(ks[0], (BATCH, SEQ, D_MODEL), 1.0),
        "attn_norm": 1.0 + nrm(ks[1], (DEPTH, D_MODEL), 0.02),
        "w_in": nrm(ks[2], (DEPTH, D_MODEL, IN_W), D_MODEL ** -0.5),
        "w_out": nrm(ks[3], (DEPTH, MIX_W, D_MODEL), MIX_W ** -0.5),
        "nsa_q_gain": 1.0 + nrm(ks[4], (DEPTH, HEAD_DIM), 0.02),
        "nsa_k_gain": 1.0 + nrm(ks[5], (DEPTH, 3, HEAD_DIM), 0.02),
        "cmp_pe": nrm(ks[6], (DEPTH, 2, CMP_LEN, HEAD_DIM), 0.1),
        "cmp_w1": nrm(ks[7], (DEPTH, 2, CMP_LEN * HEAD_DIM, HEAD_DIM), (CMP_LEN * HEAD_DIM) ** -0.5),
        "cmp_w2": nrm(ks[8], (DEPTH, 2, HEAD_DIM, HEAD_DIM), HEAD_DIM ** -0.5),
        "gm_ws": nrm(ks[9], (DEPTH, N_GM, GM_CHUNK, GM_CHUNK), GM_CHUNK ** -0.5),
        "gm_b": 1.0 + nrm(ks[10], (DEPTH, N_GM, GM_CHUNK), 0.02),
        "ffn_norm": 1.0 + nrm(ks[11], (DEPTH, D_MODEL), 0.02),
        "w_gate_up": nrm(ks[12], (DEPTH, D_MODEL, 2 * D_FF), D_MODEL ** -0.5),
        "w_down": nrm(ks[13], (DEPTH, D_FF, D_MODEL), D_FF ** -0.5),
        "rel_bias": nrm(ks[14], (N_BUCKETS, N_NSA), 0.5),
    }


def reference(x, attn_norm, w_in, w_out, nsa_q_gain, nsa_k_gain, cmp_pe, cmp_w1, cmp_w2,
              gm_ws, gm_b, ffn_norm, w_gate_up, w_down, rel_bias):
    B, T, _ = x.shape
    pos = jnp.arange(T, dtype=jnp.int32)
    split_points = [int(s) for s in np.cumsum(SPLIT_SIZES)[:-1]]
    for l in range(DEPTH):
        h = rms_norm(x, attn_norm[l])
        proj = h @ w_in[l]
        r_q, r_k, r_v, r_g, n_q, n_kv, n_g, gm_z = jnp.split(proj, split_points, axis=-1)
        hs = (B, T, N_RET, HEAD_DIM)
        ret_o = retention(rope(r_q.reshape(hs), pos), rope(r_k.reshape(hs), pos),
                          r_v.reshape(hs), r_g.reshape(hs))
        nsa_o = nsa(n_q.reshape(B, T, N_NSA, HEAD_DIM),
                    n_kv.reshape(B, T, 3, 2, N_KV, HEAD_DIM),
                    n_g.reshape(B, T, N_NSA, 3),
                    nsa_q_gain[l], nsa_k_gain[l], cmp_pe[l], cmp_w1[l], cmp_w2[l], rel_bias)
        gm_o = spatial_gating(gm_z, gm_ws[l], gm_b[l])
        mix = jnp.concatenate([ret_o, nsa_o.astype(ret_o.dtype), gm_o.astype(ret_o.dtype)], axis=-1)
        x = x + (mix @ w_out[l]).astype(x.dtype)
        h = rms_norm(x, ffn_norm[l])
        gate, up = jnp.split(h @ w_gate_up[l], 2, axis=-1)
        x = x + ((jax.nn.silu(gate) * up) @ w_down[l]).astype(x.dtype)
    return x
```

```python
import functools
import math

import numpy as np
import jax
import jax.numpy as jnp
from jax import lax
from jax.experimental import pallas as pl
from jax.experimental.pallas import tpu as pltpu

F32 = jnp.float32
BF16 = jnp.bfloat16

D_MODEL = 1024
HEAD_DIM = 64
N_RET = 6
N_NSA = 6
N_KV = 2
N_GM = 4
GM_DIM = 64
RET_W = N_RET * HEAD_DIM
NSA_W = N_NSA * HEAD_DIM
GM_W = N_GM * GM_DIM
RET_CHUNK = 128
CMP_LEN = 32
CMP_STRIDE = 16
SEL_BLOCK = 64
N_SEL = 16
WINDOW = 512
GM_CHUNK = 128
N_BUCKETS = 32
MAX_DISTANCE = 128
ROPE_THETA = 10000.0
D_FF = 2816
EPS = 1e-6
BIG = 1e9
NEG = -1e30

LANES = 128
PROJ_W = 3328
COL_RQ, COL_RK, COL_RV, COL_RG = 0, 3, 6, 9
COL_NQ = 12
COL_CMPK, COL_CMPV, COL_SELK, COL_SELV, COL_WINK, COL_WINV = 15, 16, 17, 18, 19, 20
COL_GATE = 21
COL_GMU, COL_GMV = 22, 24
QB = 256
TK = 256
VMEM_LIMIT = 56 * 1024 * 1024


def _bucket_thresholds():
    n = np.arange(0, 4 * MAX_DISTANCE)
    max_exact = N_BUCKETS // 2
    nf = np.maximum(n, 1).astype(np.float64)
    large = max_exact + np.floor(np.log(nf / max_exact) / math.log(MAX_DISTANCE / max_exact)
                                 * (N_BUCKETS - max_exact)).astype(np.int64)
    bucket = np.where(n < max_exact, n, np.minimum(large, N_BUCKETS - 1))
    return [int(np.min(n[bucket >= b])) for b in range(N_BUCKETS)]


BUCKET_THR = _bucket_thresholds()


def _cparams(sem):
    return pltpu.CompilerParams(dimension_semantics=sem, vmem_limit_bytes=VMEM_LIMIT)


def _head_ones():
    r = lax.broadcasted_iota(jnp.int32, (LANES, LANES), 0) // HEAD_DIM
    c = lax.broadcasted_iota(jnp.int32, (LANES, LANES), 1) // HEAD_DIM
    return jnp.where(r == c, 1.0, 0.0).astype(BF16)


def _head_sum(x, ones_bd):
    hi = x.astype(BF16)
    lo = (x - hi.astype(F32)).astype(BF16)
    return (jnp.dot(hi, ones_bd, preferred_element_type=F32)
            + jnp.dot(lo, ones_bd, preferred_element_type=F32))


def _head_rmsnorm(x, ones_bd):
    return x * lax.rsqrt(_head_sum(x * x, ones_bd) * (1.0 / HEAD_DIM) + EPS)


def _dot_nt(a, b):
    return lax.dot_general(a, b, (((1,), (1,)), ((), ())), preferred_element_type=F32)


def _dot_tn(a, b):
    return lax.dot_general(a, b, (((0,), (0,)), ((), ())), preferred_element_type=F32)


def _stack_heads(q, low):
    zero = jnp.zeros_like(q)
    return jnp.concatenate([jnp.where(low, q, zero), jnp.where(low, zero, q)], axis=0)


def _inproj_kernel(x_ref, g_ref, w_ref, cos_ref, sin_ref, hg_ref, o_ref, *, tm):
    x = x_ref[...]
    ms = jnp.mean(x * x, axis=-1, keepdims=True)
    h = (x * lax.rsqrt(ms + EPS) * g_ref[...]).astype(BF16)
    lane = lax.broadcasted_iota(jnp.int32, (tm, LANES), 1)
    first_half = (lane % HEAD_DIM) < (HEAD_DIM // 2)
    ones_bd = _head_ones()
    cos = cos_ref[...]
    sin = sin_ref[...]
    norm_gain = {COL_NQ: 0, COL_NQ + 1: 0, COL_NQ + 2: 0, COL_SELK: 1, COL_WINK: 2}
    for s in range(0, PROJ_W // LANES, 2):
        acc = jnp.dot(h, w_ref[:, s * LANES:(s + 2) * LANES], preferred_element_type=F32)
        for sub in range(2):
            j = s + sub
            a = acc[:, sub * LANES:(sub + 1) * LANES]
            if j < COL_RV:
                swapped = jnp.where(first_half, pltpu.roll(a, LANES - HEAD_DIM // 2, axis=1),
                                    pltpu.roll(a, HEAD_DIM // 2, axis=1))
                a = a * cos + swapped * sin
                if j >= COL_RK:
                    a = a * (HEAD_DIM ** -0.5)
            elif j in norm_gain:
                a = _head_rmsnorm(a, ones_bd) * hg_ref[norm_gain[j]:norm_gain[j] + 1, :]
            o_ref[:, j * LANES:(j + 1) * LANES] = a.astype(BF16)


def _inproj(x2d, gain, w1, cos_tab, sin_tab, head_gains, seq):
    m = x2d.shape[0]
    tm = 512
    nt = seq // tm
    return pl.pallas_call(
        functools.partial(_inproj_kernel, tm=tm),
        grid=(m // tm,),
        in_specs=[
            pl.BlockSpec((tm, D_MODEL), lambda i: (i, 0)),
            pl.BlockSpec((1, D_MODEL), lambda i: (0, 0)),
            pl.BlockSpec((D_MODEL, PROJ_W), lambda i: (0, 0)),
            pl.BlockSpec((tm, LANES), lambda i: (i % nt, 0)),
            pl.BlockSpec((tm, LANES), lambda i: (i % nt, 0)),
            pl.BlockSpec((8, LANES), lambda i: (0, 0)),
        ],
        out_specs=pl.BlockSpec((tm, PROJ_W), lambda i: (i, 0)),
        out_shape=jax.ShapeDtypeStruct((m, PROJ_W), BF16),
        compiler_params=_cparams(("arbitrary",)),
        name="inproj",
    )(x2d, gain, w1, cos_tab, sin_tab, head_gains)


def _ret_kernel(q_ref, k_ref, v_ref, g_ref, dec_ref, xi_ref, zeta_ref, cd_ref, o_ref, r_ref):
    @pl.when(pl.program_id(1) == 0)
    def _():
        r_ref[...] = jnp.zeros_like(r_ref)

    lane = lax.broadcasted_iota(jnp.int32, (RET_CHUNK, LANES), 1)
    low = lane < HEAD_DIM
    ones_bd = _head_ones()
    for c in range(RET_W // LANES):
        sl = slice(c * LANES, (c + 1) * LANES)
        q = q_ref[:, sl]
        k = k_ref[:, sl]
        v = v_ref[:, sl]
        g = g_ref[:, sl].astype(F32)
        qs = _stack_heads(q, low)
        scores = _dot_nt(qs, k) * dec_ref[c]
        inner = jnp.dot(scores.astype(BF16), v, preferred_element_type=F32)
        state = r_ref[c]
        cross = jnp.dot(qs, state.astype(BF16), preferred_element_type=F32) * xi_ref[c]
        tot = inner + cross
        o = jnp.where(low, tot[:RET_CHUNK], tot[RET_CHUNK:])
        kz = (k.astype(F32) * zeta_ref[c]).astype(BF16)
        r_ref[c] = cd_ref[c] * state + _dot_tn(kz, v)
        y = _head_rmsnorm(o, ones_bd) * (g * jax.nn.sigmoid(g))
        o_ref[:, sl] = y.astype(BF16)


def _retention(proj3, tabs):
    b, seq, _ = proj3.shape
    dec, xi, zeta, cd = tabs
    ncol = RET_W // LANES
    qkvg = [pl.BlockSpec((None, RET_CHUNK, RET_W), functools.partial(lambda bi, ti, col: (bi, ti, col), col=col))
            for col in range(4)]
    const3 = lambda shape: pl.BlockSpec(shape, lambda bi, ti: (0, 0, 0))
    return pl.pallas_call(
        _ret_kernel,
        grid=(b, seq // RET_CHUNK),
        in_specs=qkvg + [const3((ncol, 2 * RET_CHUNK, LANES)), const3((ncol, 2 * RET_CHUNK, LANES)),
                         const3((ncol, RET_CHUNK, LANES)), const3((ncol, LANES, LANES))],
        out_specs=pl.BlockSpec((None, RET_CHUNK, RET_W), lambda bi, ti: (bi, ti, 0)),
        out_shape=jax.ShapeDtypeStruct((b, seq, RET_W), BF16),
        scratch_shapes=[pltpu.VMEM((ncol, LANES, LANES), F32)],
        compiler_params=_cparams(("arbitrary", "arbitrary")),
        name="retention",
    )(proj3, proj3, proj3, proj3, dec, xi, zeta, cd)


def _compress_kernel(a_ref, wtop_ref, wbot_ref, pe_ref, w2_ref, kg_ref, kc_ref, vc_ref, *, ngrp):
    half = CMP_LEN // 2
    ones_bd = _head_ones()
    for kv in range(2):
        top = jnp.zeros((ngrp, LANES), F32)
        bot = jnp.zeros((ngrp, LANES), F32)
        for l in range(half):
            x = a_ref[:, l * 2 * LANES + kv * LANES:l * 2 * LANES + (kv + 1) * LANES].astype(F32)
            top = top + jnp.dot((x + pe_ref[kv, l]).astype(BF16), wtop_ref[kv, l], preferred_element_type=F32)
            bot = bot + jnp.dot((x + pe_ref[kv, half + l]).astype(BF16), wbot_ref[kv, l],
                                preferred_element_type=F32)
        hid = jax.nn.gelu(top + pltpu.roll(bot, ngrp - 1, axis=0), approximate=True)
        out = jnp.dot(hid.astype(BF16), w2_ref[kv], preferred_element_type=F32)
        if kv == 0:
            kc_ref[...] = (_head_rmsnorm(out, ones_bd) * kg_ref[0:1, :]).astype(BF16)
        else:
            vc_ref[...] = out.astype(BF16)


def _compress(a, wtop, wbot, pe2, w2bd, kgain):
    b, ngrp, width = a.shape
    full = lambda arr: pl.BlockSpec(arr.shape, lambda bi: (0,) * arr.ndim)
    out_spec = pl.BlockSpec((None, ngrp, LANES), lambda bi: (bi, 0, 0))
    return pl.pallas_call(
        functools.partial(_compress_kernel, ngrp=ngrp),
        grid=(b,),
        in_specs=[pl.BlockSpec((None, ngrp, width), lambda bi: (bi, 0, 0)),
                  full(wtop), full(wbot), full(pe2), full(w2bd), full(kgain)],
        out_specs=[out_spec, out_spec],
        out_shape=[jax.ShapeDtypeStruct((b, ngrp, LANES), BF16)] * 2,
        compiler_params=_cparams(("arbitrary",)),
        name="nsa_compress",
    )(a, wtop, wbot, pe2, w2bd, kgain)


def _bias_kernel(rb_ref, d0_ref, d1_ref, tc_ref):
    h = pl.program_id(0)
    far = rb_ref[N_BUCKETS - 1, h]

    def rel(n):
        val = jnp.full(n.shape, rb_ref[0, h] - far, F32)
        for bkt in range(1, N_BUCKETS):
            val = jnp.where(n >= BUCKET_THR[bkt], rb_ref[bkt, h] - far, val)
        return val

    ql = lax.broadcasted_iota(jnp.int32, (QB, TK), 0)
    kl = lax.broadcasted_iota(jnp.int32, (QB, TK), 1)
    d = ql - kl
    d0_ref[...] = jnp.where(d >= 0, rel(d), 0.0)
    d1_ref[...] = rel(d + TK)
    ql2 = lax.broadcasted_iota(jnp.int32, (QB, LANES), 0)
    m = lax.broadcasted_iota(jnp.int32, (QB, LANES), 1)
    dc = ql2 - CMP_STRIDE * (m - 16) - (CMP_LEN - 1)
    tc_ref[...] = jnp.where(dc >= 0, rel(dc), 0.0)


def _bias_tiles(rel_bias):
    return pl.pallas_call(
        _bias_kernel,
        grid=(N_NSA,),
        in_specs=[pl.BlockSpec(memory_space=pltpu.SMEM)],
        out_specs=[pl.BlockSpec((None, QB, TK), lambda h: (h, 0, 0)),
                   pl.BlockSpec((None, QB, TK), lambda h: (h, 0, 0)),
                   pl.BlockSpec((None, QB, LANES), lambda h: (h, 0, 0))],
        out_shape=[jax.ShapeDtypeStruct((N_NSA, QB, TK), F32),
                   jax.ShapeDtypeStruct((N_NSA, QB, TK), F32),
                   jax.ShapeDtypeStruct((N_NSA, QB, LANES), F32)],
        compiler_params=_cparams(("arbitrary",)),
        name="t5_bias_tiles",
    )(rel_bias)


def _nsa_kernel(q_ref, gt_ref, sk_ref, sv_ref, wk_ref, wv_ref, kc_ref, vc_ref, oh_ref, ov_ref,
                d0_ref, d1_ref, tc_ref, e_ref, o_ref, acc_ref, m_ref, *, ncmp):
    i = pl.program_id(1)
    t0 = i * QB
    ncol = NSA_W // LANES
    sq = 2 * QB
    low = lax.broadcasted_iota(jnp.int32, (QB, LANES), 1) < HEAD_DIM
    qs = [_stack_heads(q_ref[:, c * LANES:(c + 1) * LANES], low) for c in range(ncol)]

    def stacked(ref, c):
        return jnp.concatenate([ref[c], ref[c + ncol]], axis=0)

    ql_t = lax.broadcasted_iota(jnp.int32, (sq, TK), 0) % QB
    kl_t = lax.broadcasted_iota(jnp.int32, (sq, TK), 1)
    causal = kl_t <= ql_t
    ones_col = jnp.ones((TK, LANES), BF16)

    n_idx = lax.broadcasted_iota(jnp.int32, (sq, ncmp), 1)
    ql_c = lax.broadcasted_iota(jnp.int32, (sq, ncmp), 0) % QB
    valid_c = (CMP_STRIDE * n_idx + (CMP_LEN - 1)) <= (t0 + ql_c)
    first = (QB // CMP_STRIDE) * i - 16 + LANES
    rot = first % LANES
    tile_lo = first // LANES - 1
    lane_q = lax.broadcasted_iota(jnp.int32, (sq, LANES), 1)
    kc = kc_ref[...]
    vc = vc_ref[...]
    psum = jnp.zeros((sq, ncmp), F32)
    o_cmp = []
    for c in range(ncol):
        rolled = pltpu.roll(stacked(tc_ref, c), rot, axis=1)
        part_a = jnp.where(lane_q >= rot, rolled, 0.0)
        part_b = jnp.where(lane_q < rot, rolled, 0.0)
        bias = jnp.concatenate(
            [jnp.where(tile_lo == kt, part_a, 0.0) + jnp.where(tile_lo + 1 == kt, part_b, 0.0)
             for kt in range(ncmp // LANES)], axis=1)
        s = jnp.where(valid_c, _dot_nt(qs[c], kc) + bias, NEG)
        mx = jnp.max(s, axis=-1, keepdims=True)
        e = jnp.where(valid_c, jnp.exp(s - mx), 0.0)
        den = jnp.sum(e, axis=-1, keepdims=True)
        p = e * jnp.where(den > 0.0, 1.0 / den, 0.0)
        psum = psum + p
        oc = jnp.dot(p.astype(BF16), vc, preferred_element_type=F32)
        o_cmp.append(jnp.where(low, oc[:QB], oc[QB:]))

    imp = jnp.dot(psum.astype(BF16), ov_ref[...], preferred_element_type=F32)
    blk = lax.broadcasted_iota(jnp.int32, (sq, LANES), 1)
    cur = (t0 + lax.broadcasted_iota(jnp.int32, (sq, LANES), 0) % QB) // SEL_BLOCK
    imp = jnp.where((blk == 0) | (blk == cur) | (blk == cur - 1), BIG, imp)
    imp = jnp.where(blk > cur, -BIG, imp)
    blk_f = blk.astype(F32)

    def pick(_, carry):
        val, chosen = carry
        top = jnp.max(val, axis=-1, keepdims=True)
        idx = jnp.min(jnp.where(val == top, blk_f, 1e6), axis=-1, keepdims=True)
        hit = blk_f == idx
        return jnp.where(hit, -jnp.inf, val), jnp.where(hit, 1.0, chosen)

    _, chosen = lax.fori_loop(0, N_SEL, pick, (imp, jnp.zeros((sq, LANES), F32)))
    msel = jnp.where(chosen > 0.0, 0.0, NEG).astype(BF16)

    qaug = [jnp.concatenate([qs[c], msel], axis=1) for c in range(ncol)]

    def sel_scores(c, start):
        kaug = jnp.concatenate([sk_ref[pl.ds(start, TK), :], oh_ref[pl.ds(start, TK), :]], axis=1)
        return _dot_nt(qaug[c], kaug)

    def sel_vaug(start):
        return jnp.concatenate([sv_ref[pl.ds(start, TK), :], ones_col], axis=1)

    def wide(x):
        return jnp.concatenate([x, x], axis=1)

    start_d = pl.multiple_of(t0, TK)
    start_p = pl.multiple_of(jnp.maximum(t0 - TK, 0), TK)
    has_prev = i >= 1
    for c in range(ncol):
        s0 = jnp.where(causal, sel_scores(c, start_d) + stacked(d0_ref, c), NEG)
        s1 = jnp.where(has_prev, sel_scores(c, start_p) + stacked(d1_ref, c), NEG)
        mx = jnp.maximum(jnp.max(s0, axis=-1, keepdims=True), jnp.max(s1, axis=-1, keepdims=True))
        p0 = jnp.exp(s0 - mx).astype(BF16)
        p1 = jnp.exp(s1 - mx).astype(BF16)
        acc_ref[c] = (jnp.dot(p0, sel_vaug(start_d), preferred_element_type=F32)
                      + jnp.dot(p1, sel_vaug(start_p), preferred_element_type=F32))
        m_ref[c] = jnp.broadcast_to(mx, (sq, LANES))

    def far_tile(kt, carry):
        start = pl.multiple_of(kt * TK, TK)
        vaug = sel_vaug(start)
        for c in range(ncol):
            s = sel_scores(c, start)
            m_old = m_ref[c]
            m_new = jnp.maximum(m_old, jnp.max(s, axis=-1, keepdims=True))
            alpha = jnp.exp(m_old - m_new)
            p = jnp.exp(s - wide(m_new)).astype(BF16)
            acc_ref[c] = wide(alpha) * acc_ref[c] + jnp.dot(p, vaug, preferred_element_type=F32)
            m_ref[c] = m_new
        return carry

    lax.fori_loop(0, jnp.maximum(i - 1, 0), far_tile, 0)

    start_e = pl.multiple_of(jnp.maximum(t0 - 2 * TK, 0), TK)
    has_edge = i >= 2
    edge_mask = (kl_t > ql_t) & has_edge
    wv_all = jnp.concatenate([
        jnp.concatenate([wv_ref[pl.ds(st, TK), :], ones_col], axis=1) for st in (start_d, start_p, start_e)], axis=0)
    o_win = []
    for c in range(ncol):
        s0 = jnp.where(causal, _dot_nt(qs[c], wk_ref[pl.ds(start_d, TK), :]) + stacked(d0_ref, c), NEG)
        s1 = jnp.where(has_prev, _dot_nt(qs[c], wk_ref[pl.ds(start_p, TK), :]) + stacked(d1_ref, c), NEG)
        s2 = jnp.where(edge_mask, _dot_nt(qs[c], wk_ref[pl.ds(start_e, TK), :]), NEG)
        s = jnp.concatenate([s0, s1, s2], axis=1)
        mx = jnp.max(s, axis=-1, keepdims=True)
        p = jnp.exp(s - mx).astype(BF16)
        ow = jnp.dot(p, wv_all, preferred_element_type=F32)
        o_win.append(jnp.where(low, ow[:QB, :LANES] / ow[:QB, LANES:], ow[QB:, :LANES] / ow[QB:, LANES:]))

    gate = jax.nn.sigmoid(gt_ref[...].astype(F32))
    g_hi = gate.astype(BF16)
    g_lo = (gate - g_hi.astype(F32)).astype(BF16)
    gexp = (jnp.dot(g_hi, e_ref[...], preferred_element_type=F32)
            + jnp.dot(g_lo, e_ref[...], preferred_element_type=F32))
    for c in range(ncol):
        acc = acc_ref[c]
        o_sel = jnp.where(low, acc[:QB, :LANES] / acc[:QB, LANES:], acc[QB:, :LANES] / acc[QB:, LANES:])
        gc = lambda br: gexp[:, (br * ncol + c) * LANES:(br * ncol + c + 1) * LANES]
        o_ref[:, c * LANES:(c + 1) * LANES] = (gc(0) * o_cmp[c] + gc(1) * o_sel + gc(2) * o_win[c]).astype(BF16)


def _nsa(proj3, kcmp, vcmp, onehot, overlap, d0, d1, tc, expand):
    b, seq, _ = proj3.shape
    ncmp = kcmp.shape[1]
    col = lambda width, cidx: pl.BlockSpec((None, QB, width), lambda bi, qi: (bi, qi, cidx))
    res = lambda cidx: pl.BlockSpec((None, seq, LANES), lambda bi, qi: (bi, 0, cidx))
    full = lambda arr: pl.BlockSpec(arr.shape, lambda bi, qi: (0,) * arr.ndim)
    cmp_spec = pl.BlockSpec((None, ncmp, LANES), lambda bi, qi: (bi, 0, 0))
    return pl.pallas_call(
        functools.partial(_nsa_kernel, ncmp=ncmp),
        grid=(b, seq // QB),
        in_specs=[col(NSA_W, COL_NQ * LANES // NSA_W), col(LANES, COL_GATE),
                  res(COL_SELK), res(COL_SELV), res(COL_WINK), res(COL_WINV),
                  cmp_spec, cmp_spec, full(onehot), full(overlap), full(d0), full(d1), full(tc), full(expand)],
        out_specs=pl.BlockSpec((None, QB, NSA_W), lambda bi, qi: (bi, qi, 0)),
        out_shape=jax.ShapeDtypeStruct((b, seq, NSA_W), BF16),
        scratch_shapes=[pltpu.VMEM((NSA_W // LANES, 2 * QB, 2 * LANES), F32),
                        pltpu.VMEM((NSA_W // LANES, 2 * QB, LANES), F32)],
        compiler_params=_cparams(("arbitrary", "arbitrary")),
        name="nsa_attention",
    )(proj3, proj3, proj3, proj3, proj3, proj3, kcmp, vcmp, onehot, overlap, d0, d1, tc, expand)


def _gmlp_kernel(u_ref, v_ref, ws_ref, b_ref, o_ref):
    lane = lax.broadcasted_iota(jnp.int32, (GM_CHUNK, LANES), 1)
    low = lane < GM_DIM
    row = lax.broadcasted_iota(jnp.int32, (GM_CHUNK, GM_CHUNK), 0)
    colm = lax.broadcasted_iota(jnp.int32, (GM_CHUNK, GM_CHUNK), 1)
    tril = colm <= row
    ones_bd = _head_ones()
    for cc in range(GM_W // LANES):
        sl = slice(cc * LANES, (cc + 1) * LANES)
        u = jax.nn.gelu(u_ref[:, sl].astype(F32), approximate=True)
        v = jax.nn.gelu(v_ref[:, sl].astype(F32), approximate=True)
        vn = _head_rmsnorm(v, ones_bd).astype(BF16)
        w = jnp.concatenate([jnp.where(tril, ws_ref[2 * cc], 0.0), jnp.where(tril, ws_ref[2 * cc + 1], 0.0)],
                            axis=0).astype(BF16)
        r = jnp.dot(w, vn, preferred_element_type=F32)
        sv = jnp.where(low, r[:GM_CHUNK], r[GM_CHUNK:]) + b_ref[:, sl]
        o_ref[:, sl] = (u * sv).astype(BF16)


def _gmlp(proj3, ws, bexp):
    b, seq, _ = proj3.shape
    return pl.pallas_call(
        _gmlp_kernel,
        grid=(b, seq // GM_CHUNK),
        in_specs=[pl.BlockSpec((None, GM_CHUNK, GM_W), lambda bi, ti: (bi, ti, COL_GMU * LANES // GM_W)),
                  pl.BlockSpec((None, GM_CHUNK, GM_W), lambda bi, ti: (bi, ti, COL_GMV * LANES // GM_W)),
                  pl.BlockSpec((N_GM, GM_CHUNK, GM_CHUNK), lambda bi, ti: (0, 0, 0)),
                  pl.BlockSpec((GM_CHUNK, GM_W), lambda bi, ti: (0, 0))],
        out_specs=pl.BlockSpec((None, GM_CHUNK, GM_W), lambda bi, ti: (bi, ti, 0)),
        out_shape=jax.ShapeDtypeStruct((b, seq, GM_W), BF16),
        compiler_params=_cparams(("arbitrary", "arbitrary")),
        name="gmlp",
    )(proj3, proj3, ws, bexp)


def _outproj_kernel(x_ref, r_ref, n_ref, g_ref, w_ref, o_ref):
    acc = jnp.dot(r_ref[...], w_ref[0:RET_W, :], preferred_element_type=F32)
    acc = acc + jnp.dot(n_ref[...], w_ref[RET_W:RET_W + NSA_W, :], preferred_element_type=F32)
    acc = acc + jnp.dot(g_ref[...], w_ref[RET_W + NSA_W:, :], preferred_element_type=F32)
    o_ref[...] = x_ref[...] + acc


def _outproj(x2d, ret_o, nsa_o, gm_o, w_out):
    m = x2d.shape[0]
    tm = 512
    row = lambda width: pl.BlockSpec((tm, width), lambda i: (i, 0))
    return pl.pallas_call(
        _outproj_kernel,
        grid=(m // tm,),
        in_specs=[row(D_MODEL), row(RET_W), row(NSA_W), row(GM_W),
                  pl.BlockSpec((D_MODEL, D_MODEL), lambda i: (0, 0))],
        out_specs=row(D_MODEL),
        out_shape=jax.ShapeDtypeStruct((m, D_MODEL), F32),
        compiler_params=_cparams(("arbitrary",)),
        name="outproj",
    )(x2d, ret_o, nsa_o, gm_o, w_out)


FFN_CHUNK = 256


def _ffn_kernel(x_ref, g_ref, wg_ref, wu_ref, wd_ref, o_ref):
    x = x_ref[...]
    ms = jnp.mean(x * x, axis=-1, keepdims=True)
    h = (x * lax.rsqrt(ms + EPS) * g_ref[...]).astype(BF16)
    acc = x
    for f in range(0, D_FF, FFN_CHUNK):
        gate = jnp.dot(h, wg_ref[:, f:f + FFN_CHUNK], preferred_element_type=F32)
        up = jnp.dot(h, wu_ref[:, f:f + FFN_CHUNK], preferred_element_type=F32)
        act = (gate * jax.nn.sigmoid(gate) * up).astype(BF16)
        acc = acc + jnp.dot(act, wd_ref[f:f + FFN_CHUNK, :], preferred_element_type=F32)
    o_ref[...] = acc


def _ffn(x2d, gain, wg, wu, wd):
    m = x2d.shape[0]
    tm = 512
    const = lambda arr: pl.BlockSpec(arr.shape, lambda i: (0, 0), pipeline_mode=pl.Buffered(1))
    return pl.pallas_call(
        _ffn_kernel,
        grid=(m // tm,),
        in_specs=[pl.BlockSpec((tm, D_MODEL), lambda i: (i, 0)),
                  pl.BlockSpec((1, D_MODEL), lambda i: (0, 0)),
                  const(wg), const(wu), const(wd)],
        out_specs=pl.BlockSpec((tm, D_MODEL), lambda i: (i, 0)),
        out_shape=jax.ShapeDtypeStruct((m, D_MODEL), F32),
        compiler_params=_cparams(("arbitrary",)),
        name="ffn",
    )(x2d, gain, wg, wu, wd)


def _nq_perm():
    idx = np.zeros(NSA_W, np.int64)
    for c in range(NSA_W // LANES):
        for half in range(2):
            head = c + (N_NSA // N_KV) * half
            idx[c * LANES + half * HEAD_DIM:c * LANES + (half + 1) * HEAD_DIM] = head * HEAD_DIM + np.arange(HEAD_DIM)
    return idx


def _w_in_layout(w):
    off_nq = 4 * RET_W
    off_kv = off_nq + NSA_W
    off_gate = off_kv + 3 * 2 * N_KV * HEAD_DIM
    off_gm = off_gate + 3 * N_NSA
    pad = jnp.zeros((D_MODEL, LANES - 3 * N_NSA), w.dtype)
    return jnp.concatenate([w[:, :off_nq], w[:, off_nq + _nq_perm()], w[:, off_kv:off_gate],
                            w[:, off_gate:off_gm], pad, w[:, off_gm:]], axis=1).astype(BF16)


def _w_out_layout(w):
    return jnp.concatenate([w[:RET_W], w[RET_W + _nq_perm()], w[RET_W + NSA_W:]], axis=0).astype(BF16)


def _gate_expand():
    e = np.zeros((LANES, 3 * NSA_W), np.float32)
    ncol = NSA_W // LANES
    for br in range(3):
        for c in range(ncol):
            for half in range(2):
                head = c + (N_NSA // N_KV) * half
                dst = (br * ncol + c) * LANES + half * HEAD_DIM
                e[head * 3 + br, dst:dst + HEAD_DIM] = 1.0
    return jnp.asarray(e, BF16)


def _rope_tables(seq):
    half = HEAD_DIM // 2
    inv = ROPE_THETA ** (-jnp.arange(half, dtype=F32) / half)
    ang = jnp.arange(seq, dtype=jnp.int32).astype(F32)[:, None] * inv[None, :]
    cos = jnp.tile(jnp.cos(ang), (1, LANES // half))
    sin = jnp.tile(jnp.concatenate([-jnp.sin(ang), jnp.sin(ang)], axis=1), (1, LANES // HEAD_DIM))
    return cos, sin


def _retention_tables():
    c = RET_CHUNK
    log_gamma = jnp.log(1.0 - 2.0 ** (-5.0 - jnp.arange(N_RET, dtype=F32)))
    idx = jnp.arange(c, dtype=F32)
    diff = idx[:, None] - idx[None, :]
    decay = jnp.where(diff >= 0, jnp.exp(jnp.maximum(diff, 0.0)[None] * log_gamma[:, None, None]), 0.0)
    zeta = jnp.exp((c - 1 - idx)[None, :] * log_gamma[:, None])
    xi = jnp.exp((idx + 1)[None, :] * log_gamma[:, None])
    chunk_decay = jnp.exp(c * log_gamma)
    ncol = RET_W // LANES
    dec = decay.reshape(ncol, 2 * c, c)
    xi_st = jnp.broadcast_to(xi.reshape(ncol, 2 * c, 1), (ncol, 2 * c, LANES))
    zeta_l = jnp.repeat(zeta.reshape(ncol, 2, c).transpose(0, 2, 1), HEAD_DIM, axis=2)
    cd = jnp.broadcast_to(jnp.repeat(chunk_decay.reshape(ncol, 2), HEAD_DIM, axis=1)[:, :, None],
                          (ncol, LANES, LANES))
    return dec, xi_st, zeta_l, cd


def _overlap_table(seq):
    n_cmp = (seq - CMP_LEN) // CMP_STRIDE + 1
    n_slc = seq // SEL_BLOCK
    cs = np.arange(n_cmp)[:, None] * CMP_STRIDE
    ss = np.arange(n_slc)[None, :] * SEL_BLOCK
    ov = np.clip(np.minimum(cs + CMP_LEN, ss + SEL_BLOCK) - np.maximum(cs, ss), 0, None) // CMP_STRIDE
    full = np.zeros((seq // CMP_STRIDE, LANES), np.float32)
    full[:n_cmp, :n_slc] = ov
    return jnp.asarray(full, BF16)


def _block_onehot(seq):
    oh = (np.arange(seq)[:, None] // SEL_BLOCK) == np.arange(LANES)[None, :]
    return jnp.asarray(oh.astype(np.float32), BF16)


def _compress_weights(cmp_pe, cmp_w1, cmp_w2):
    eye2 = jnp.eye(N_KV, dtype=F32)
    w1 = cmp_w1.reshape(2, CMP_LEN, HEAD_DIM, HEAD_DIM)
    w1bd = jnp.einsum('gh,kldf->klgdhf', eye2, w1).reshape(2, CMP_LEN, LANES, LANES).astype(BF16)
    w2bd = jnp.einsum('gh,kdf->kgdhf', eye2, cmp_w2).reshape(2, LANES, LANES).astype(BF16)
    pe2 = jnp.tile(cmp_pe, (1, 1, N_KV))[:, :, None, :]
    return w1bd[:, :CMP_LEN // 2], w1bd[:, CMP_LEN // 2:], pe2, w2bd


def kernel(x, attn_norm, w_in, w_out, nsa_q_gain, nsa_k_gain, cmp_pe, cmp_w1, cmp_w2, gm_ws, gm_b, ffn_norm,
           w_gate_up, w_down, rel_bias):
    b, seq, _ = x.shape
    assert seq % (2 * TK) == 0 and N_SEL <= seq // SEL_BLOCK <= LANES, "selection blocks must fit one lane row"
    depth = w_in.shape[0]
    cos_tab, sin_tab = _rope_tables(seq)
    ret_tabs = _retention_tables()
    overlap = _overlap_table(seq)
    onehot = _block_onehot(seq)
    expand = _gate_expand()
    d0, d1, tc = _bias_tiles(rel_bias)
    ngrp = seq // CMP_STRIDE
    x2d = x.reshape(b * seq, D_MODEL)
    for l in range(depth):
        zero_row = jnp.zeros((5, LANES), F32)
        head_gains = jnp.concatenate([jnp.tile(nsa_q_gain[l] * HEAD_DIM ** -0.5, 2)[None],
                                      jnp.tile(nsa_k_gain[l, 1], 2)[None],
                                      jnp.tile(nsa_k_gain[l, 2], 2)[None], zero_row], axis=0)
        proj = _inproj(x2d, attn_norm[l][None], _w_in_layout(w_in[l]), cos_tab, sin_tab, head_gains, seq)
        proj3 = proj.reshape(b, seq, PROJ_W)
        ret_o = _retention(proj3, ret_tabs)
        cmp_in = proj3[:, :, COL_CMPK * LANES:(COL_CMPV + 1) * LANES].reshape(b, ngrp, CMP_STRIDE * 2 * LANES)
        wtop, wbot, pe2, w2bd = _compress_weights(cmp_pe[l], cmp_w1[l], cmp_w2[l])
        kgain = jnp.broadcast_to(jnp.tile(nsa_k_gain[l, 0], 2)[None], (8, LANES))
        kcmp, vcmp = _compress(cmp_in, wtop, wbot, pe2, w2bd, kgain)
        nsa_o = _nsa(proj3, kcmp, vcmp, onehot, overlap, d0, d1, tc, expand)
        gm_o = _gmlp(proj3, gm_ws[l], jnp.repeat(gm_b[l].T, GM_DIM, axis=1))
        x2d = _outproj(x2d, ret_o.reshape(b * seq, RET_W), nsa_o.reshape(b * seq, NSA_W),
                       gm_o.reshape(b * seq, GM_W), _w_out_layout(w_out[l]))
        x2d = _ffn(x2d, ffn_norm[l][None], w_gate_up[l, :, :D_FF].astype(BF16),
                   w_gate_up[l, :, D_FF:].astype(BF16), w_down[l].astype(BF16))
    return x2d.reshape(b, seq, D_MODEL)
```

```python
import functools
import math

import numpy as np
import jax
import jax.numpy as jnp
from jax import lax
from jax.experimental import pallas as pl
from jax.experimental.pallas import tpu as pltpu

F32 = jnp.float32
BF16 = jnp.bfloat16

D_MODEL = 1024
HEAD_DIM = 64
N_RET = 6
N_NSA = 6
N_KV = 2
N_GM = 4
GM_DIM = 64
RET_W = N_RET * HEAD_DIM
NSA_W = N_NSA * HEAD_DIM
GM_W = N_GM * GM_DIM
RET_CHUNK = 128
CMP_LEN = 32
CMP_STRIDE = 16
SEL_BLOCK = 64
N_SEL = 16
WINDOW = 512
GM_CHUNK = 128
N_BUCKETS = 32
MAX_DISTANCE = 128
ROPE_THETA = 10000.0
D_FF = 2816
EPS = 1e-6
BIG = 1e9
NEG = -1e30

LANES = 128
PROJ_W = 3328
COL_RQ, COL_RK, COL_RV, COL_RG = 0, 3, 6, 9
COL_NQ = 12
COL_CMPK, COL_CMPV, COL_SELK, COL_SELV, COL_WINK, COL_WINV = 15, 16, 17, 18, 19, 20
COL_GATE = 21
COL_GMU, COL_GMV = 22, 24
QB = 256
TK = 256
FAR_TILES = 4
VMEM_LIMIT = 56 * 1024 * 1024


def _bucket_thresholds():
    n = np.arange(0, 4 * MAX_DISTANCE)
    max_exact = N_BUCKETS // 2
    nf = np.maximum(n, 1).astype(np.float64)
    large = max_exact + np.floor(np.log(nf / max_exact) / math.log(MAX_DISTANCE / max_exact)
                                 * (N_BUCKETS - max_exact)).astype(np.int64)
    bucket = np.where(n < max_exact, n, np.minimum(large, N_BUCKETS - 1))
    return [int(np.min(n[bucket >= b])) for b in range(N_BUCKETS)]


BUCKET_THR = _bucket_thresholds()


def _cparams(sem):
    return pltpu.CompilerParams(dimension_semantics=sem, vmem_limit_bytes=VMEM_LIMIT)


def _head_ones():
    r = lax.broadcasted_iota(jnp.int32, (LANES, LANES), 0) // HEAD_DIM
    c = lax.broadcasted_iota(jnp.int32, (LANES, LANES), 1) // HEAD_DIM
    return jnp.where(r == c, 1.0, 0.0).astype(BF16)


def _head_sum(x, ones_bd):
    hi = x.astype(BF16)
    lo = (x - hi.astype(F32)).astype(BF16)
    return (jnp.dot(hi, ones_bd, preferred_element_type=F32)
            + jnp.dot(lo, ones_bd, preferred_element_type=F32))


def _head_rmsnorm(x, ones_bd):
    return x * lax.rsqrt(_head_sum(x * x, ones_bd) * (1.0 / HEAD_DIM) + EPS)


def _dot_nt(a, b):
    return lax.dot_general(a, b, (((1,), (1,)), ((), ())), preferred_element_type=F32)


def _dot_tn(a, b):
    return lax.dot_general(a, b, (((0,), (0,)), ((), ())), preferred_element_type=F32)


def _stack_heads(q, low):
    zero = jnp.zeros_like(q)
    return jnp.concatenate([jnp.where(low, q, zero), jnp.where(low, zero, q)], axis=0)


def _inproj_kernel(x_ref, g_ref, w_ref, cos_ref, sin_ref, hg_ref, o_ref, *, tm):
    x = x_ref[...]
    ms = jnp.mean(x * x, axis=-1, keepdims=True)
    h = (x * lax.rsqrt(ms + EPS) * g_ref[...]).astype(BF16)
    lane = lax.broadcasted_iota(jnp.int32, (tm, LANES), 1)
    first_half = (lane % HEAD_DIM) < (HEAD_DIM // 2)
    ones_bd = _head_ones()
    cos = cos_ref[...]
    sin = sin_ref[...]
    norm_gain = {COL_NQ: 0, COL_NQ + 1: 0, COL_NQ + 2: 0, COL_SELK: 1, COL_WINK: 2}
    for s in range(0, PROJ_W // LANES, 2):
        acc = jnp.dot(h, w_ref[:, s * LANES:(s + 2) * LANES], preferred_element_type=F32)
        for sub in range(2):
            j = s + sub
            a = acc[:, sub * LANES:(sub + 1) * LANES]
            if j < COL_RV:
                swapped = jnp.where(first_half, pltpu.roll(a, LANES - HEAD_DIM // 2, axis=1),
                                    pltpu.roll(a, HEAD_DIM // 2, axis=1))
                a = a * cos + swapped * sin
                if j >= COL_RK:
                    a = a * (HEAD_DIM ** -0.5)
            elif j in norm_gain:
                a = _head_rmsnorm(a, ones_bd) * hg_ref[norm_gain[j]:norm_gain[j] + 1, :]
            o_ref[:, j * LANES:(j + 1) * LANES] = a.astype(BF16)


def _inproj(x2d, gain, w1, cos_tab, sin_tab, head_gains, seq):
    m = x2d.shape[0]
    tm = 512
    nt = seq // tm
    return pl.pallas_call(
        functools.partial(_inproj_kernel, tm=tm),
        grid=(m // tm,),
        in_specs=[
            pl.BlockSpec((tm, D_MODEL), lambda i: (i, 0)),
            pl.BlockSpec((1, D_MODEL), lambda i: (0, 0)),
            pl.BlockSpec((D_MODEL, PROJ_W), lambda i: (0, 0)),
            pl.BlockSpec((tm, LANES), lambda i: (i % nt, 0)),
            pl.BlockSpec((tm, LANES), lambda i: (i % nt, 0)),
            pl.BlockSpec((8, LANES), lambda i: (0, 0)),
        ],
        out_specs=pl.BlockSpec((tm, PROJ_W), lambda i: (i, 0)),
        out_shape=jax.ShapeDtypeStruct((m, PROJ_W), BF16),
        compiler_params=_cparams(("arbitrary",)),
        name="inproj",
    )(x2d, gain, w1, cos_tab, sin_tab, head_gains)


def _ret_kernel(q_ref, k_ref, v_ref, g_ref, dec_ref, xi_ref, zeta_ref, cd_ref, o_ref, r_ref):
    @pl.when(pl.program_id(1) == 0)
    def _():
        r_ref[...] = jnp.zeros_like(r_ref)

    lane = lax.broadcasted_iota(jnp.int32, (RET_CHUNK, LANES), 1)
    low = lane < HEAD_DIM
    ones_bd = _head_ones()
    for c in range(RET_W // LANES):
        sl = slice(c * LANES, (c + 1) * LANES)
        q = q_ref[:, sl]
        k = k_ref[:, sl]
        v = v_ref[:, sl]
        g = g_ref[:, sl].astype(F32)
        qs = _stack_heads(q, low)
        scores = _dot_nt(qs, k) * dec_ref[c]
        inner = jnp.dot(scores.astype(BF16), v, preferred_element_type=F32)
        state = r_ref[c]
        cross = jnp.dot(qs, state.astype(BF16), preferred_element_type=F32) * xi_ref[c]
        tot = inner + cross
        o = jnp.where(low, tot[:RET_CHUNK], tot[RET_CHUNK:])
        kz = (k.astype(F32) * zeta_ref[c]).astype(BF16)
        r_ref[c] = cd_ref[c] * state + _dot_tn(kz, v)
        y = _head_rmsnorm(o, ones_bd) * (g * jax.nn.sigmoid(g))
        o_ref[:, sl] = y.astype(BF16)


def _retention(proj3, tabs):
    b, seq, _ = proj3.shape
    dec, xi, zeta, cd = tabs
    ncol = RET_W // LANES
    qkvg = [pl.BlockSpec((None, RET_CHUNK, RET_W), functools.partial(lambda bi, ti, col: (bi, ti, col), col=col))
            for col in range(4)]
    const3 = lambda shape: pl.BlockSpec(shape, lambda bi, ti: (0, 0, 0))
    return pl.pallas_call(
        _ret_kernel,
        grid=(b, seq // RET_CHUNK),
        in_specs=qkvg + [const3((ncol, 2 * RET_CHUNK, LANES)), const3((ncol, 2 * RET_CHUNK, LANES)),
                         const3((ncol, RET_CHUNK, LANES)), const3((ncol, LANES, LANES))],
        out_specs=pl.BlockSpec((None, RET_CHUNK, RET_W), lambda bi, ti: (bi, ti, 0)),
        out_shape=jax.ShapeDtypeStruct((b, seq, RET_W), BF16),
        scratch_shapes=[pltpu.VMEM((ncol, LANES, LANES), F32)],
        compiler_params=_cparams(("arbitrary", "arbitrary")),
        name="retention",
    )(proj3, proj3, proj3, proj3, dec, xi, zeta, cd)


def _compress_kernel(a_ref, wtop_ref, wbot_ref, pe_ref, w2_ref, kg_ref, kc_ref, vc_ref, *, ngrp):
    half = CMP_LEN // 2
    ones_bd = _head_ones()
    for kv in range(2):
        top = jnp.zeros((ngrp, LANES), F32)
        bot = jnp.zeros((ngrp, LANES), F32)
        for l in range(half):
            x = a_ref[:, l * 2 * LANES + kv * LANES:l * 2 * LANES + (kv + 1) * LANES].astype(F32)
            top = top + jnp.dot((x + pe_ref[kv, l]).astype(BF16), wtop_ref[kv, l], preferred_element_type=F32)
            bot = bot + jnp.dot((x + pe_ref[kv, half + l]).astype(BF16), wbot_ref[kv, l],
                                preferred_element_type=F32)
        hid = jax.nn.gelu(top + pltpu.roll(bot, ngrp - 1, axis=0), approximate=True)
        out = jnp.dot(hid.astype(BF16), w2_ref[kv], preferred_element_type=F32)
        if kv == 0:
            kc_ref[...] = (_head_rmsnorm(out, ones_bd) * kg_ref[0:1, :]).astype(BF16)
        else:
            vc_ref[...] = out.astype(BF16)


def _compress(a, wtop, wbot, pe2, w2bd, kgain):
    b, ngrp, width = a.shape
    full = lambda arr: pl.BlockSpec(arr.shape, lambda bi: (0,) * arr.ndim)
    out_spec = pl.BlockSpec((None, ngrp, LANES), lambda bi: (bi, 0, 0))
    return pl.pallas_call(
        functools.partial(_compress_kernel, ngrp=ngrp),
        grid=(b,),
        in_specs=[pl.BlockSpec((None, ngrp, width), lambda bi: (bi, 0, 0)),
                  full(wtop), full(wbot), full(pe2), full(w2bd), full(kgain)],
        out_specs=[out_spec, out_spec],
        out_shape=[jax.ShapeDtypeStruct((b, ngrp, LANES), BF16)] * 2,
        compiler_params=_cparams(("arbitrary",)),
        name="nsa_compress",
    )(a, wtop, wbot, pe2, w2bd, kgain)


def _bias_kernel(rb_ref, d0_ref, d1_ref, tc_ref):
    h = pl.program_id(0)
    far = rb_ref[N_BUCKETS - 1, h]

    def rel(n):
        val = jnp.full(n.shape, rb_ref[0, h] - far, F32)
        for bkt in range(1, N_BUCKETS):
            val = jnp.where(n >= BUCKET_THR[bkt], rb_ref[bkt, h] - far, val)
        return val

    ql = lax.broadcasted_iota(jnp.int32, (QB, TK), 0)
    kl = lax.broadcasted_iota(jnp.int32, (QB, TK), 1)
    d = ql - kl
    d0_ref[...] = jnp.where(d >= 0, rel(d), 0.0)
    d1_ref[...] = rel(d + TK)
    ql2 = lax.broadcasted_iota(jnp.int32, (QB, LANES), 0)
    m = lax.broadcasted_iota(jnp.int32, (QB, LANES), 1)
    dc = ql2 - CMP_STRIDE * (m - 16) - (CMP_LEN - 1)
    tc_ref[...] = jnp.where(dc >= 0, rel(dc), 0.0)


def _bias_tiles(rel_bias):
    return pl.pallas_call(
        _bias_kernel,
        grid=(N_NSA,),
        in_specs=[pl.BlockSpec(memory_space=pltpu.SMEM)],
        out_specs=[pl.BlockSpec((None, QB, TK), lambda h: (h, 0, 0)),
                   pl.BlockSpec((None, QB, TK), lambda h: (h, 0, 0)),
                   pl.BlockSpec((None, QB, LANES), lambda h: (h, 0, 0))],
        out_shape=[jax.ShapeDtypeStruct((N_NSA, QB, TK), F32),
                   jax.ShapeDtypeStruct((N_NSA, QB, TK), F32),
                   jax.ShapeDtypeStruct((N_NSA, QB, LANES), F32)],
        compiler_params=_cparams(("arbitrary",)),
        name="t5_bias_tiles",
    )(rel_bias)


def _nsa_kernel(q_ref, gt_ref, sk_ref, sv_ref, wk_ref, wv_ref, kc_ref, vc_ref, oh_ref, ov_ref,
                d0_ref, d1_ref, tc_ref, e_ref, o_ref, acc_ref, m_ref, *, ncmp):
    i = pl.program_id(1)
    t0 = i * QB
    ncol = NSA_W // LANES
    sq = 2 * QB
    low = lax.broadcasted_iota(jnp.int32, (QB, LANES), 1) < HEAD_DIM
    qs = [_stack_heads(q_ref[:, c * LANES:(c + 1) * LANES], low) for c in range(ncol)]

    def stacked(ref, c):
        return jnp.concatenate([ref[c], ref[c + ncol]], axis=0)

    ql_t = lax.broadcasted_iota(jnp.int32, (sq, TK), 0) % QB
    kl_t = lax.broadcasted_iota(jnp.int32, (sq, TK), 1)
    causal = kl_t <= ql_t
    ones_col = jnp.ones((TK, LANES), BF16)

    n_idx = lax.broadcasted_iota(jnp.int32, (sq, ncmp), 1)
    ql_c = lax.broadcasted_iota(jnp.int32, (sq, ncmp), 0) % QB
    valid_c = (CMP_STRIDE * n_idx + (CMP_LEN - 1)) <= (t0 + ql_c)
    first = (QB // CMP_STRIDE) * i - 16 + LANES
    rot = first % LANES
    tile_lo = first // LANES - 1
    lane_q = lax.broadcasted_iota(jnp.int32, (sq, LANES), 1)
    kc = kc_ref[...]
    vc = vc_ref[...]
    psum = jnp.zeros((sq, ncmp), F32)
    o_cmp = []
    for c in range(ncol):
        rolled = pltpu.roll(stacked(tc_ref, c), rot, axis=1)
        part_a = jnp.where(lane_q >= rot, rolled, 0.0)
        part_b = jnp.where(lane_q < rot, rolled, 0.0)
        bias = jnp.concatenate(
            [jnp.where(tile_lo == kt, part_a, 0.0) + jnp.where(tile_lo + 1 == kt, part_b, 0.0)
             for kt in range(ncmp // LANES)], axis=1)
        s = jnp.where(valid_c, _dot_nt(qs[c], kc) + bias, NEG)
        mx = jnp.max(s, axis=-1, keepdims=True)
        e = jnp.where(valid_c, jnp.exp(s - mx), 0.0)
        den = jnp.sum(e, axis=-1, keepdims=True)
        p = e * jnp.where(den > 0.0, 1.0 / den, 0.0)
        psum = psum + p
        oc = jnp.dot(p.astype(BF16), vc, preferred_element_type=F32)
        o_cmp.append(jnp.where(low, oc[:QB], oc[QB:]))

    imp = jnp.dot(psum.astype(BF16), ov_ref[...], preferred_element_type=F32)
    blk = lax.broadcasted_iota(jnp.int32, (sq, LANES), 1)
    cur = (t0 + lax.broadcasted_iota(jnp.int32, (sq, LANES), 0) % QB) // SEL_BLOCK
    forced = (blk == 0) | (blk == cur) | (blk == cur - 1)
    imp = jnp.where(forced, -jnp.inf, jnp.where(blk > cur, -BIG, imp))
    blk_f = blk.astype(F32)

    def pick(_, carry):
        val, chosen = carry
        top = jnp.max(val, axis=-1, keepdims=True)
        idx = jnp.min(jnp.where(val == top, blk_f, 1e6), axis=-1, keepdims=True)
        hit = blk_f == idx
        return jnp.where(hit, -jnp.inf, val), jnp.where(hit, 1.0, chosen)

    _, chosen = lax.fori_loop(0, N_SEL - 3, pick, (imp, jnp.where(forced, 1.0, 0.0)))
    msel = jnp.where(chosen > 0.0, 0.0, NEG).astype(BF16)

    qaug = [jnp.concatenate([qs[c], msel], axis=1) for c in range(ncol)]

    def sel_scores(c, start, width):
        kaug = jnp.concatenate([sk_ref[pl.ds(start, width), :], oh_ref[pl.ds(start, width), :]], axis=1)
        return _dot_nt(qaug[c], kaug)

    def sel_vaug(start, width):
        return jnp.concatenate([sv_ref[pl.ds(start, width), :], jnp.ones((width, LANES), BF16)], axis=1)

    def lanes(x, width):
        return jnp.concatenate([x] * (width // LANES), axis=1)

    start_d = pl.multiple_of(t0, TK)
    start_p = pl.multiple_of(jnp.maximum(t0 - TK, 0), TK)
    has_prev = i >= 1
    for c in range(ncol):
        s0 = jnp.where(causal, sel_scores(c, start_d, TK) + stacked(d0_ref, c), NEG)
        s1 = jnp.where(has_prev, sel_scores(c, start_p, TK) + stacked(d1_ref, c), NEG)
        mx = jnp.maximum(jnp.max(s0, axis=-1, keepdims=True), jnp.max(s1, axis=-1, keepdims=True))
        p0 = jnp.exp(s0 - mx).astype(BF16)
        p1 = jnp.exp(s1 - mx).astype(BF16)
        acc_ref[c] = (jnp.dot(p0, sel_vaug(start_d, TK), preferred_element_type=F32)
                      + jnp.dot(p1, sel_vaug(start_p, TK), preferred_element_type=F32))
        m_ref[c] = jnp.broadcast_to(mx, (sq, LANES))

    def far_tile(start, width):
        vaug = sel_vaug(start, width)
        for c in range(ncol):
            s = sel_scores(c, start, width)
            m_old = m_ref[c]
            m_new = jnp.maximum(m_old, jnp.max(s, axis=-1, keepdims=True))
            alpha = jnp.exp(m_old - m_new)
            p = jnp.exp(s - lanes(m_new, width)).astype(BF16)
            acc_ref[c] = lanes(alpha, 2 * LANES) * acc_ref[c] + jnp.dot(p, vaug, preferred_element_type=F32)
            m_ref[c] = m_new

    n_far = jnp.maximum(i - 1, 0)
    n_wide = n_far // FAR_TILES

    def far_wide(kt, carry):
        far_tile(pl.multiple_of(kt * (FAR_TILES * TK), FAR_TILES * TK), FAR_TILES * TK)
        return carry

    def far_rest(kt, carry):
        far_tile(pl.multiple_of(kt * TK, TK), TK)
        return carry

    lax.fori_loop(0, n_wide, far_wide, 0)
    lax.fori_loop(n_wide * FAR_TILES, n_far, far_rest, 0)

    start_e = pl.multiple_of(jnp.maximum(t0 - 2 * TK, 0), TK)
    has_edge = i >= 2
    edge_mask = (kl_t > ql_t) & has_edge
    wv_all = jnp.concatenate([
        jnp.concatenate([wv_ref[pl.ds(st, TK), :], ones_col], axis=1) for st in (start_d, start_p, start_e)], axis=0)
    o_win = []
    for c in range(ncol):
        s0 = jnp.where(causal, _dot_nt(qs[c], wk_ref[pl.ds(start_d, TK), :]) + stacked(d0_ref, c), NEG)
        s1 = jnp.where(has_prev, _dot_nt(qs[c], wk_ref[pl.ds(start_p, TK), :]) + stacked(d1_ref, c), NEG)
        s2 = jnp.where(edge_mask, _dot_nt(qs[c], wk_ref[pl.ds(start_e, TK), :]), NEG)
        s = jnp.concatenate([s0, s1, s2], axis=1)
        mx = jnp.max(s, axis=-1, keepdims=True)
        p = jnp.exp(s - mx).astype(BF16)
        ow = jnp.dot(p, wv_all, preferred_element_type=F32)
        o_win.append(jnp.where(low, ow[:QB, :LANES] / ow[:QB, LANES:], ow[QB:, :LANES] / ow[QB:, LANES:]))

    gate = jax.nn.sigmoid(gt_ref[...].astype(F32))
    g_hi = gate.astype(BF16)
    g_lo = (gate - g_hi.astype(F32)).astype(BF16)
    gexp = (jnp.dot(g_hi, e_ref[...], preferred_element_type=F32)
            + jnp.dot(g_lo, e_ref[...], preferred_element_type=F32))
    for c in range(ncol):
        acc = acc_ref[c]
        o_sel = jnp.where(low, acc[:QB, :LANES] / acc[:QB, LANES:], acc[QB:, :LANES] / acc[QB:, LANES:])
        gc = lambda br: gexp[:, (br * ncol + c) * LANES:(br * ncol + c + 1) * LANES]
        o_ref[:, c * LANES:(c + 1) * LANES] = (gc(0) * o_cmp[c] + gc(1) * o_sel + gc(2) * o_win[c]).astype(BF16)


def _nsa(proj3, kcmp, vcmp, onehot, overlap, d0, d1, tc, expand):
    b, seq, _ = proj3.shape
    ncmp = kcmp.shape[1]
    col = lambda width, cidx: pl.BlockSpec((None, QB, width), lambda bi, qi: (bi, qi, cidx))
    res = lambda cidx: pl.BlockSpec((None, seq, LANES), lambda bi, qi: (bi, 0, cidx))
    full = lambda arr: pl.BlockSpec(arr.shape, lambda bi, qi: (0,) * arr.ndim)
    cmp_spec = pl.BlockSpec((None, ncmp, LANES), lambda bi, qi: (bi, 0, 0))
    return pl.pallas_call(
        functools.partial(_nsa_kernel, ncmp=ncmp),
        grid=(b, seq // QB),
        in_specs=[col(NSA_W, COL_NQ * LANES // NSA_W), col(LANES, COL_GATE),
                  res(COL_SELK), res(COL_SELV), res(COL_WINK), res(COL_WINV),
                  cmp_spec, cmp_spec, full(onehot), full(overlap), full(d0), full(d1), full(tc), full(expand)],
        out_specs=pl.BlockSpec((None, QB, NSA_W), lambda bi, qi: (bi, qi, 0)),
        out_shape=jax.ShapeDtypeStruct((b, seq, NSA_W), BF16),
        scratch_shapes=[pltpu.VMEM((NSA_W // LANES, 2 * QB, 2 * LANES), F32),
                        pltpu.VMEM((NSA_W // LANES, 2 * QB, LANES), F32)],
        compiler_params=_cparams(("arbitrary", "arbitrary")),
        name="nsa_attention",
    )(proj3, proj3, proj3, proj3, proj3, proj3, kcmp, vcmp, onehot, overlap, d0, d1, tc, expand)


def _gmlp_kernel(u_ref, v_ref, ws_ref, b_ref, o_ref):
    lane = lax.broadcasted_iota(jnp.int32, (GM_CHUNK, LANES), 1)
    low = lane < GM_DIM
    row = lax.broadcasted_iota(jnp.int32, (GM_CHUNK, GM_CHUNK), 0)
    colm = lax.broadcasted_iota(jnp.int32, (GM_CHUNK, GM_CHUNK), 1)
    tril = colm <= row
    ones_bd = _head_ones()
    for cc in range(GM_W // LANES):
        sl = slice(cc * LANES, (cc + 1) * LANES)
        u = jax.nn.gelu(u_ref[:, sl].astype(F32), approximate=True)
        v = jax.nn.gelu(v_ref[:, sl].astype(F32), approximate=True)
        vn = _head_rmsnorm(v, ones_bd).astype(BF16)
        w = jnp.concatenate([jnp.where(tril, ws_ref[2 * cc], 0.0), jnp.where(tril, ws_ref[2 * cc + 1], 0.0)],
                            axis=0).astype(BF16)
        r = jnp.dot(w, vn, preferred_element_type=F32)
        sv = jnp.where(low, r[:GM_CHUNK], r[GM_CHUNK:]) + b_ref[:, sl]
        o_ref[:, sl] = (u * sv).astype(BF16)


def _gmlp(proj3, ws, bexp):
    b, seq, _ = proj3.shape
    return pl.pallas_call(
        _gmlp_kernel,
        grid=(b, seq // GM_CHUNK),
        in_specs=[pl.BlockSpec((None, GM_CHUNK, GM_W), lambda bi, ti: (bi, ti, COL_GMU * LANES // GM_W)),
                  pl.BlockSpec((None, GM_CHUNK, GM_W), lambda bi, ti: (bi, ti, COL_GMV * LANES // GM_W)),
                  pl.BlockSpec((N_GM, GM_CHUNK, GM_CHUNK), lambda bi, ti: (0, 0, 0)),
                  pl.BlockSpec((GM_CHUNK, GM_W), lambda bi, ti: (0, 0))],
        out_specs=pl.BlockSpec((None, GM_CHUNK, GM_W), lambda bi, ti: (bi, ti, 0)),
        out_shape=jax.ShapeDtypeStruct((b, seq, GM_W), BF16),
        compiler_params=_cparams(("arbitrary", "arbitrary")),
        name="gmlp",
    )(proj3, proj3, ws, bexp)


def _outproj_kernel(x_ref, r_ref, n_ref, g_ref, w_ref, o_ref):
    acc = jnp.dot(r_ref[...], w_ref[0:RET_W, :], preferred_element_type=F32)
    acc = acc + jnp.dot(n_ref[...], w_ref[RET_W:RET_W + NSA_W, :], preferred_element_type=F32)
    acc = acc + jnp.dot(g_ref[...], w_ref[RET_W + NSA_W:, :], preferred_element_type=F32)
    o_ref[...] = x_ref[...] + acc


def _outproj(x2d, ret_o, nsa_o, gm_o, w_out):
    m = x2d.shape[0]
    tm = 512
    row = lambda width: pl.BlockSpec((tm, width), lambda i: (i, 0))
    return pl.pallas_call(
        _outproj_kernel,
        grid=(m // tm,),
        in_specs=[row(D_MODEL), row(RET_W), row(NSA_W), row(GM_W),
                  pl.BlockSpec((D_MODEL, D_MODEL), lambda i: (0, 0))],
        out_specs=row(D_MODEL),
        out_shape=jax.ShapeDtypeStruct((m, D_MODEL), F32),
        compiler_params=_cparams(("arbitrary",)),
        name="outproj",
    )(x2d, ret_o, nsa_o, gm_o, w_out)


FFN_CHUNK = 256


def _ffn_kernel(x_ref, g_ref, wg_ref, wu_ref, wd_ref, o_ref):
    x = x_ref[...]
    ms = jnp.mean(x * x, axis=-1, keepdims=True)
    h = (x * lax.rsqrt(ms + EPS) * g_ref[...]).astype(BF16)
    acc = x
    for f in range(0, D_FF, FFN_CHUNK):
        gate = jnp.dot(h, wg_ref[:, f:f + FFN_CHUNK], preferred_element_type=F32)
        up = jnp.dot(h, wu_ref[:, f:f + FFN_CHUNK], preferred_element_type=F32)
        act = (gate * jax.nn.sigmoid(gate) * up).astype(BF16)
        acc = acc + jnp.dot(act, wd_ref[f:f + FFN_CHUNK, :], preferred_element_type=F32)
    o_ref[...] = acc


def _ffn(x2d, gain, wg, wu, wd):
    m = x2d.shape[0]
    tm = 512
    const = lambda arr: pl.BlockSpec(arr.shape, lambda i: (0, 0), pipeline_mode=pl.Buffered(1))
    return pl.pallas_call(
        _ffn_kernel,
        grid=(m // tm,),
        in_specs=[pl.BlockSpec((tm, D_MODEL), lambda i: (i, 0)),
                  pl.BlockSpec((1, D_MODEL), lambda i: (0, 0)),
                  const(wg), const(wu), const(wd)],
        out_specs=pl.BlockSpec((tm, D_MODEL), lambda i: (i, 0)),
        out_shape=jax.ShapeDtypeStruct((m, D_MODEL), F32),
        compiler_params=_cparams(("arbitrary",)),
        name="ffn",
    )(x2d, gain, wg, wu, wd)


def _nq_perm():
    idx = np.zeros(NSA_W, np.int64)
    for c in range(NSA_W // LANES):
        for half in range(2):
            head = c + (N_NSA // N_KV) * half
            idx[c * LANES + half * HEAD_DIM:c * LANES + (half + 1) * HEAD_DIM] = head * HEAD_DIM + np.arange(HEAD_DIM)
    return idx


def _w_in_layout(w):
    off_nq = 4 * RET_W
    off_kv = off_nq + NSA_W
    off_gate = off_kv + 3 * 2 * N_KV * HEAD_DIM
    off_gm = off_gate + 3 * N_NSA
    pad = jnp.zeros((D_MODEL, LANES - 3 * N_NSA), w.dtype)
    return jnp.concatenate([w[:, :off_nq], w[:, off_nq + _nq_perm()], w[:, off_kv:off_gate],
                            w[:, off_gate:off_gm], pad, w[:, off_gm:]], axis=1).astype(BF16)


def _w_out_layout(w):
    return jnp.concatenate([w[:RET_W], w[RET_W + _nq_perm()], w[RET_W + NSA_W:]], axis=0).astype(BF16)


def _gate_expand():
    e = np.zeros((LANES, 3 * NSA_W), np.float32)
    ncol = NSA_W // LANES
    for br in range(3):
        for c in range(ncol):
            for half in range(2):
                head = c + (N_NSA // N_KV) * half
                dst = (br * ncol + c) * LANES + half * HEAD_DIM
                e[head * 3 + br, dst:dst + HEAD_DIM] = 1.0
    return jnp.asarray(e, BF16)


def _rope_tables(seq):
    half = HEAD_DIM // 2
    inv = ROPE_THETA ** (-jnp.arange(half, dtype=F32) / half)
    ang = jnp.arange(seq, dtype=jnp.int32).astype(F32)[:, None] * inv[None, :]
    cos = jnp.tile(jnp.cos(ang), (1, LANES // half))
    sin = jnp.tile(jnp.concatenate([-jnp.sin(ang), jnp.sin(ang)], axis=1), (1, LANES // HEAD_DIM))
    return cos, sin


def _retention_tables():
    c = RET_CHUNK
    log_gamma = jnp.log(1.0 - 2.0 ** (-5.0 - jnp.arange(N_RET, dtype=F32)))
    idx = jnp.arange(c, dtype=F32)
    diff = idx[:, None] - idx[None, :]
    decay = jnp.where(diff >= 0, jnp.exp(jnp.maximum(diff, 0.0)[None] * log_gamma[:, None, None]), 0.0)
    zeta = jnp.exp((c - 1 - idx)[None, :] * log_gamma[:, None])
    xi = jnp.exp((idx + 1)[None, :] * log_gamma[:, None])
    chunk_decay = jnp.exp(c * log_gamma)
    ncol = RET_W // LANES
    dec = decay.reshape(ncol, 2 * c, c)
    xi_st = jnp.broadcast_to(xi.reshape(ncol, 2 * c, 1), (ncol, 2 * c, LANES))
    zeta_l = jnp.repeat(zeta.reshape(ncol, 2, c).transpose(0, 2, 1), HEAD_DIM, axis=2)
    cd = jnp.broadcast_to(jnp.repeat(chunk_decay.reshape(ncol, 2), HEAD_DIM, axis=1)[:, :, None],
                          (ncol, LANES, LANES))
    return dec, xi_st, zeta_l, cd


def _overlap_table(seq):
    n_cmp = (seq - CMP_LEN) // CMP_STRIDE + 1
    n_slc = seq // SEL_BLOCK
    cs = np.arange(n_cmp)[:, None] * CMP_STRIDE
    ss = np.arange(n_slc)[None, :] * SEL_BLOCK
    ov = np.clip(np.minimum(cs + CMP_LEN, ss + SEL_BLOCK) - np.maximum(cs, ss), 0, None) // CMP_STRIDE
    full = np.zeros((seq // CMP_STRIDE, LANES), np.float32)
    full[:n_cmp, :n_slc] = ov
    return jnp.asarray(full, BF16)


def _block_onehot(seq):
    oh = (np.arange(seq)[:, None] // SEL_BLOCK) == np.arange(LANES)[None, :]
    return jnp.asarray(oh.astype(np.float32), BF16)


def _compress_weights(cmp_pe, cmp_w1, cmp_w2):
    eye2 = jnp.eye(N_KV, dtype=F32)
    w1 = cmp_w1.reshape(2, CMP_LEN, HEAD_DIM, HEAD_DIM)
    w1bd = jnp.einsum('gh,kldf->klgdhf', eye2, w1).reshape(2, CMP_LEN, LANES, LANES).astype(BF16)
    w2bd = jnp.einsum('gh,kdf->kgdhf', eye2, cmp_w2).reshape(2, LANES, LANES).astype(BF16)
    pe2 = jnp.tile(cmp_pe, (1, 1, N_KV))[:, :, None, :]
    return w1bd[:, :CMP_LEN // 2], w1bd[:, CMP_LEN // 2:], pe2, w2bd


def kernel(x, attn_norm, w_in, w_out, nsa_q_gain, nsa_k_gain, cmp_pe, cmp_w1, cmp_w2, gm_ws, gm_b, ffn_norm,
           w_gate_up, w_down, rel_bias):
    b, seq, _ = x.shape
    assert seq % (2 * TK) == 0 and N_SEL <= seq // SEL_BLOCK <= LANES, "selection blocks must fit one lane row"
    depth = w_in.shape[0]
    cos_tab, sin_tab = _rope_tables(seq)
    ret_tabs = _retention_tables()
    overlap = _overlap_table(seq)
    onehot = _block_onehot(seq)
    expand = _gate_expand()
    d0, d1, tc = _bias_tiles(rel_bias)
    ngrp = seq // CMP_STRIDE
    x2d = x.reshape(b * seq, D_MODEL)
    for l in range(depth):
        zero_row = jnp.zeros((5, LANES), F32)
        head_gains = jnp.concatenate([jnp.tile(nsa_q_gain[l] * HEAD_DIM ** -0.5, 2)[None],
                                      jnp.tile(nsa_k_gain[l, 1], 2)[None],
                                      jnp.tile(nsa_k_gain[l, 2], 2)[None], zero_row], axis=0)
        proj = _inproj(x2d, attn_norm[l][None], _w_in_layout(w_in[l]), cos_tab, sin_tab, head_gains, seq)
        proj3 = proj.reshape(b, seq, PROJ_W)
        ret_o = _retention(proj3, ret_tabs)
        cmp_in = proj3[:, :, COL_CMPK * LANES:(COL_CMPV + 1) * LANES].reshape(b, ngrp, CMP_STRIDE * 2 * LANES)
        wtop, wbot, pe2, w2bd = _compress_weights(cmp_pe[l], cmp_w1[l], cmp_w2[l])
        kgain = jnp.broadcast_to(jnp.tile(nsa_k_gain[l, 0], 2)[None], (8, LANES))
        kcmp, vcmp = _compress(cmp_in, wtop, wbot, pe2, w2bd, kgain)
        nsa_o = _nsa(proj3, kcmp, vcmp, onehot, overlap, d0, d1, tc, expand)
        gm_o = _gmlp(proj3, gm_ws[l], jnp.repeat(gm_b[l].T, GM_DIM, axis=1))
        x2d = _outproj(x2d, ret_o.reshape(b * seq, RET_W), nsa_o.reshape(b * seq, NSA_W),
                       gm_o.reshape(b * seq, GM_W), _w_out_layout(w_out[l]))
        x2d = _ffn(x2d, ffn_norm[l][None], w_gate_up[l, :, :D_FF].astype(BF16),
                   w_gate_up[l, :, D_FF:].astype(BF16), w_down[l].astype(BF16))
    return x2d.reshape(b, seq, D_MODEL)
```

```python
import functools
import math

import numpy as np
import jax
import jax.numpy as jnp
from jax import lax
from jax.experimental import pallas as pl
from jax.experimental.pallas import tpu as pltpu

F32 = jnp.float32
BF16 = jnp.bfloat16

D_MODEL = 1024
HEAD_DIM = 64
N_RET = 6
N_NSA = 6
N_KV = 2
N_GM = 4
GM_DIM = 64
RET_W = N_RET * HEAD_DIM
NSA_W = N_NSA * HEAD_DIM
GM_W = N_GM * GM_DIM
RET_CHUNK = 128
CMP_LEN = 32
CMP_STRIDE = 16
SEL_BLOCK = 64
N_SEL = 16
WINDOW = 512
GM_CHUNK = 128
N_BUCKETS = 32
MAX_DISTANCE = 128
ROPE_THETA = 10000.0
D_FF = 2816
EPS = 1e-6
BIG = 1e9
NEG = -1e30

LANES = 128
PROJ_W = 3328
COL_RQ, COL_RK, COL_RV, COL_RG = 0, 3, 6, 9
COL_NQ = 12
COL_CMPK, COL_CMPV, COL_SELK, COL_SELV, COL_WINK, COL_WINV = 15, 16, 17, 18, 19, 20
COL_GATE = 21
COL_GMU, COL_GMV = 22, 24
QB = 256
TK = 256
FAR_TILES = 4
VMEM_LIMIT = 56 * 1024 * 1024


def _bucket_thresholds():
    n = np.arange(0, 4 * MAX_DISTANCE)
    max_exact = N_BUCKETS // 2
    nf = np.maximum(n, 1).astype(np.float64)
    large = max_exact + np.floor(np.log(nf / max_exact) / math.log(MAX_DISTANCE / max_exact)
                                 * (N_BUCKETS - max_exact)).astype(np.int64)
    bucket = np.where(n < max_exact, n, np.minimum(large, N_BUCKETS - 1))
    return [int(np.min(n[bucket >= b])) for b in range(N_BUCKETS)]


BUCKET_THR = _bucket_thresholds()


def _cparams(sem):
    return pltpu.CompilerParams(dimension_semantics=sem, vmem_limit_bytes=VMEM_LIMIT)


def _head_ones():
    r = lax.broadcasted_iota(jnp.int32, (LANES, LANES), 0) // HEAD_DIM
    c = lax.broadcasted_iota(jnp.int32, (LANES, LANES), 1) // HEAD_DIM
    return jnp.where(r == c, 1.0, 0.0).astype(BF16)


def _head_sum(x, ones_bd):
    hi = x.astype(BF16)
    lo = (x - hi.astype(F32)).astype(BF16)
    return (jnp.dot(hi, ones_bd, preferred_element_type=F32)
            + jnp.dot(lo, ones_bd, preferred_element_type=F32))


def _head_rmsnorm(x, ones_bd):
    return x * lax.rsqrt(_head_sum(x * x, ones_bd) * (1.0 / HEAD_DIM) + EPS)


def _dot_nt(a, b):
    return lax.dot_general(a, b, (((1,), (1,)), ((), ())), preferred_element_type=F32)


def _dot_tn(a, b):
    return lax.dot_general(a, b, (((0,), (0,)), ((), ())), preferred_element_type=F32)


def _stack_heads(q, low):
    zero = jnp.zeros_like(q)
    return jnp.concatenate([jnp.where(low, q, zero), jnp.where(low, zero, q)], axis=0)


def _inproj_kernel(x_ref, g_ref, w_ref, cos_ref, sin_ref, hg_ref, o_ref, grp_ref, cmp_scr, *, tm):
    x = x_ref[...]
    ms = jnp.mean(x * x, axis=-1, keepdims=True)
    h = (x * lax.rsqrt(ms + EPS) * g_ref[...]).astype(BF16)
    lane = lax.broadcasted_iota(jnp.int32, (tm, LANES), 1)
    first_half = (lane % HEAD_DIM) < (HEAD_DIM // 2)
    ones_bd = _head_ones()
    cos = cos_ref[...]
    sin = sin_ref[...]
    norm_gain = {COL_NQ: 0, COL_NQ + 1: 0, COL_NQ + 2: 0, COL_SELK: 1, COL_WINK: 2}
    for s in range(0, PROJ_W // LANES, 2):
        acc = jnp.dot(h, w_ref[:, s * LANES:(s + 2) * LANES], preferred_element_type=F32)
        for sub in range(2):
            j = s + sub
            a = acc[:, sub * LANES:(sub + 1) * LANES]
            if j < COL_RV:
                swapped = jnp.where(first_half, pltpu.roll(a, LANES - HEAD_DIM // 2, axis=1),
                                    pltpu.roll(a, HEAD_DIM // 2, axis=1))
                a = a * cos + swapped * sin
                if j >= COL_RK:
                    a = a * (HEAD_DIM ** -0.5)
            elif j in norm_gain:
                a = _head_rmsnorm(a, ones_bd) * hg_ref[norm_gain[j]:norm_gain[j] + 1, :]
            elif j in (COL_CMPK, COL_CMPV):
                cmp_scr[j - COL_CMPK] = a
            o_ref[:, j * LANES:(j + 1) * LANES] = a.astype(BF16)
    for l in range(CMP_STRIDE):
        for kv in range(2):
            grp_ref[:, (2 * l + kv) * LANES:(2 * l + kv + 1) * LANES] = (
                cmp_scr[kv, pl.ds(l, tm // CMP_STRIDE, stride=CMP_STRIDE), :].astype(BF16))


def _inproj(x2d, gain, w1, cos_tab, sin_tab, head_gains, seq):
    m = x2d.shape[0]
    tm = 512
    nt = seq // tm
    return pl.pallas_call(
        functools.partial(_inproj_kernel, tm=tm),
        grid=(m // tm,),
        in_specs=[
            pl.BlockSpec((tm, D_MODEL), lambda i: (i, 0)),
            pl.BlockSpec((1, D_MODEL), lambda i: (0, 0)),
            pl.BlockSpec((D_MODEL, PROJ_W), lambda i: (0, 0)),
            pl.BlockSpec((tm, LANES), lambda i: (i % nt, 0)),
            pl.BlockSpec((tm, LANES), lambda i: (i % nt, 0)),
            pl.BlockSpec((8, LANES), lambda i: (0, 0)),
        ],
        out_specs=[pl.BlockSpec((tm, PROJ_W), lambda i: (i, 0)),
                   pl.BlockSpec((tm // CMP_STRIDE, CMP_STRIDE * 2 * LANES), lambda i: (i, 0))],
        out_shape=[jax.ShapeDtypeStruct((m, PROJ_W), BF16),
                   jax.ShapeDtypeStruct((m // CMP_STRIDE, CMP_STRIDE * 2 * LANES), BF16)],
        scratch_shapes=[pltpu.VMEM((2, tm, LANES), F32)],
        compiler_params=_cparams(("arbitrary",)),
        name="inproj",
    )(x2d, gain, w1, cos_tab, sin_tab, head_gains)


def _ret_kernel(q_ref, k_ref, v_ref, g_ref, dec_ref, xi_ref, zeta_ref, cd_ref, o_ref, r_ref):
    @pl.when(pl.program_id(1) == 0)
    def _():
        r_ref[...] = jnp.zeros_like(r_ref)

    lane = lax.broadcasted_iota(jnp.int32, (RET_CHUNK, LANES), 1)
    low = lane < HEAD_DIM
    ones_bd = _head_ones()
    for c in range(RET_W // LANES):
        sl = slice(c * LANES, (c + 1) * LANES)
        q = q_ref[:, sl]
        k = k_ref[:, sl]
        v = v_ref[:, sl]
        g = g_ref[:, sl].astype(F32)
        qs = _stack_heads(q, low)
        scores = _dot_nt(qs, k) * dec_ref[c]
        inner = jnp.dot(scores.astype(BF16), v, preferred_element_type=F32)
        state = r_ref[c]
        cross = jnp.dot(qs, state.astype(BF16), preferred_element_type=F32) * xi_ref[c]
        tot = inner + cross
        o = jnp.where(low, tot[:RET_CHUNK], tot[RET_CHUNK:])
        kz = (k.astype(F32) * zeta_ref[c]).astype(BF16)
        r_ref[c] = cd_ref[c] * state + _dot_tn(kz, v)
        y = _head_rmsnorm(o, ones_bd) * (g * jax.nn.sigmoid(g))
        o_ref[:, sl] = y.astype(BF16)


def _retention(proj3, tabs):
    b, seq, _ = proj3.shape
    dec, xi, zeta, cd = tabs
    ncol = RET_W // LANES
    qkvg = [pl.BlockSpec((None, RET_CHUNK, RET_W), functools.partial(lambda bi, ti, col: (bi, ti, col), col=col))
            for col in range(4)]
    const3 = lambda shape: pl.BlockSpec(shape, lambda bi, ti: (0, 0, 0))
    return pl.pallas_call(
        _ret_kernel,
        grid=(b, seq // RET_CHUNK),
        in_specs=qkvg + [const3((ncol, 2 * RET_CHUNK, LANES)), const3((ncol, 2 * RET_CHUNK, LANES)),
                         const3((ncol, RET_CHUNK, LANES)), const3((ncol, LANES, LANES))],
        out_specs=pl.BlockSpec((None, RET_CHUNK, RET_W), lambda bi, ti: (bi, ti, 0)),
        out_shape=jax.ShapeDtypeStruct((b, seq, RET_W), BF16),
        scratch_shapes=[pltpu.VMEM((ncol, LANES, LANES), F32)],
        compiler_params=_cparams(("arbitrary", "arbitrary")),
        name="retention",
    )(proj3, proj3, proj3, proj3, dec, xi, zeta, cd)


def _compress_kernel(a_ref, wtop_ref, wbot_ref, pe_ref, w2_ref, kg_ref, kc_ref, vc_ref, *, ngrp):
    half = CMP_LEN // 2
    ones_bd = _head_ones()
    for kv in range(2):
        top = jnp.zeros((ngrp, LANES), F32)
        bot = jnp.zeros((ngrp, LANES), F32)
        for l in range(half):
            x = a_ref[:, l * 2 * LANES + kv * LANES:l * 2 * LANES + (kv + 1) * LANES].astype(F32)
            top = top + jnp.dot((x + pe_ref[kv, l]).astype(BF16), wtop_ref[kv, l], preferred_element_type=F32)
            bot = bot + jnp.dot((x + pe_ref[kv, half + l]).astype(BF16), wbot_ref[kv, l],
                                preferred_element_type=F32)
        hid = jax.nn.gelu(top + pltpu.roll(bot, ngrp - 1, axis=0), approximate=True)
        out = jnp.dot(hid.astype(BF16), w2_ref[kv], preferred_element_type=F32)
        if kv == 0:
            kc_ref[...] = (_head_rmsnorm(out, ones_bd) * kg_ref[0:1, :]).astype(BF16)
        else:
            vc_ref[...] = out.astype(BF16)


def _compress(a, wtop, wbot, pe2, w2bd, kgain):
    b, ngrp, width = a.shape
    full = lambda arr: pl.BlockSpec(arr.shape, lambda bi: (0,) * arr.ndim)
    out_spec = pl.BlockSpec((None, ngrp, LANES), lambda bi: (bi, 0, 0))
    return pl.pallas_call(
        functools.partial(_compress_kernel, ngrp=ngrp),
        grid=(b,),
        in_specs=[pl.BlockSpec((None, ngrp, width), lambda bi: (bi, 0, 0)),
                  full(wtop), full(wbot), full(pe2), full(w2bd), full(kgain)],
        out_specs=[out_spec, out_spec],
        out_shape=[jax.ShapeDtypeStruct((b, ngrp, LANES), BF16)] * 2,
        compiler_params=_cparams(("arbitrary",)),
        name="nsa_compress",
    )(a, wtop, wbot, pe2, w2bd, kgain)


def _bias_kernel(rb_ref, d0_ref, d1_ref, tc_ref):
    h = pl.program_id(0)
    far = rb_ref[N_BUCKETS - 1, h]

    def rel(n):
        val = jnp.full(n.shape, rb_ref[0, h] - far, F32)
        for bkt in range(1, N_BUCKETS):
            val = jnp.where(n >= BUCKET_THR[bkt], rb_ref[bkt, h] - far, val)
        return val

    ql = lax.broadcasted_iota(jnp.int32, (QB, TK), 0)
    kl = lax.broadcasted_iota(jnp.int32, (QB, TK), 1)
    d = ql - kl
    d0_ref[...] = jnp.where(d >= 0, rel(d), NEG)
    d1_ref[...] = rel(d + TK)
    ql2 = lax.broadcasted_iota(jnp.int32, (QB, LANES), 0)
    m = lax.broadcasted_iota(jnp.int32, (QB, LANES), 1)
    dc = ql2 - CMP_STRIDE * (m - 16) - (CMP_LEN - 1)
    tc_ref[...] = jnp.where(dc >= 0, rel(dc), NEG).astype(BF16)


def _bias_tiles(rel_bias):
    return pl.pallas_call(
        _bias_kernel,
        grid=(N_NSA,),
        in_specs=[pl.BlockSpec(memory_space=pltpu.SMEM)],
        out_specs=[pl.BlockSpec((None, QB, TK), lambda h: (h, 0, 0)),
                   pl.BlockSpec((None, QB, TK), lambda h: (h, 0, 0)),
                   pl.BlockSpec((None, QB, LANES), lambda h: (h, 0, 0))],
        out_shape=[jax.ShapeDtypeStruct((N_NSA, QB, TK), F32),
                   jax.ShapeDtypeStruct((N_NSA, QB, TK), F32),
                   jax.ShapeDtypeStruct((N_NSA, QB, LANES), BF16)],
        compiler_params=_cparams(("arbitrary",)),
        name="t5_bias_tiles",
    )(rel_bias)


def _nsa_kernel(q_ref, gt_ref, sk_ref, sv_ref, wk_ref, wv_ref, kc_ref, vc_ref, oh_ref, ov_ref, band_ref,
                d0_ref, d1_ref, tc_ref, e_ref, o_ref, acc_ref, m_ref, *, ncmp):
    i = pl.program_id(1)
    t0 = i * QB
    ncol = NSA_W // LANES
    sq = 2 * QB
    low = lax.broadcasted_iota(jnp.int32, (QB, LANES), 1) < HEAD_DIM
    qs = [_stack_heads(q_ref[:, c * LANES:(c + 1) * LANES], low) for c in range(ncol)]

    def stacked(ref, c):
        return jnp.concatenate([ref[c], ref[c + ncol]], axis=0)

    ones_col = jnp.ones((TK, LANES), BF16)

    band = band_ref[pl.ds(pl.multiple_of((QB // CMP_STRIDE) * (pl.num_programs(1) - i), CMP_STRIDE), ncmp), :]
    kc_aug = jnp.concatenate([kc_ref[...], band], axis=1)
    vc_ov = jnp.concatenate([vc_ref[...], ov_ref[...]], axis=1)
    imp = jnp.zeros((sq, LANES), F32)
    o_cmp = []
    for c in range(ncol):
        s = _dot_nt(jnp.concatenate([qs[c], stacked(tc_ref, c)], axis=1), kc_aug)
        mx = jnp.maximum(jnp.max(s, axis=-1, keepdims=True), -1e20)
        e = jnp.exp(s - mx)
        den = jnp.sum(e, axis=-1, keepdims=True)
        inv = jnp.where(den > 0.0, 1.0 / den, 0.0)
        both = jnp.dot(e.astype(BF16), vc_ov, preferred_element_type=F32) * inv
        imp = imp + both[:, LANES:]
        o_cmp.append(jnp.where(low, both[:QB, :LANES], both[QB:, :LANES]))

    imp_t = imp.T
    blk = lax.broadcasted_iota(jnp.int32, (LANES, sq), 0)
    cur = (t0 + lax.broadcasted_iota(jnp.int32, (LANES, sq), 1) % QB) // SEL_BLOCK
    forced = (blk == 0) | (blk == cur) | (blk == cur - 1)
    imp_t = jnp.where(forced, -jnp.inf, jnp.where(blk > cur, -BIG, imp_t))
    blk_f = blk.astype(F32)

    def pick(_, carry):
        val, chosen = carry
        top = jnp.max(val, axis=0, keepdims=True)
        idx = jnp.min(jnp.where(val == top, blk_f, 1e6), axis=0, keepdims=True)
        hit = blk_f == idx
        return jnp.where(hit, -jnp.inf, val), jnp.where(hit, 0.0, chosen)

    _, msel_t = lax.fori_loop(0, N_SEL - 3, pick, (imp_t, jnp.where(forced, 0.0, NEG)))
    msel = msel_t.T.astype(BF16)

    qaug = [jnp.concatenate([qs[c], msel], axis=1) for c in range(ncol)]

    def sel_scores(c, start, width):
        kaug = jnp.concatenate([sk_ref[pl.ds(start, width), :], oh_ref[pl.ds(start, width), :]], axis=1)
        return _dot_nt(qaug[c], kaug)

    def sel_vaug(start, width):
        return jnp.concatenate([sv_ref[pl.ds(start, width), :], jnp.ones((width, LANES), BF16)], axis=1)

    def lanes(x, width):
        return jnp.concatenate([x] * (width // LANES), axis=1)

    start_d = pl.multiple_of(t0, TK)
    start_p = pl.multiple_of(jnp.maximum(t0 - TK, 0), TK)
    off_prev = jnp.where(i >= 1, 0.0, -NEG)
    off_edge = jnp.where(i >= 2, 0.0, -NEG)
    for c in range(ncol):
        s0 = sel_scores(c, start_d, TK) + stacked(d0_ref, c)
        s1 = sel_scores(c, start_p, TK) + stacked(d1_ref, c)
        mx = jnp.maximum(jnp.max(s0, axis=-1, keepdims=True), jnp.max(s1, axis=-1, keepdims=True) - off_prev)
        p0 = jnp.exp(s0 - mx).astype(BF16)
        p1 = jnp.exp(s1 - (mx + off_prev)).astype(BF16)
        acc_ref[c] = (jnp.dot(p0, sel_vaug(start_d, TK), preferred_element_type=F32)
                      + jnp.dot(p1, sel_vaug(start_p, TK), preferred_element_type=F32))
        m_ref[c] = jnp.broadcast_to(mx, (sq, LANES))

    def far_tile(start, width):
        vaug = sel_vaug(start, width)
        for c in range(ncol):
            s = sel_scores(c, start, width)
            m_old = m_ref[c]
            m_new = jnp.maximum(m_old, jnp.max(s, axis=-1, keepdims=True))
            alpha = jnp.exp(m_old - m_new)
            p = jnp.exp(s - lanes(m_new, width)).astype(BF16)
            acc_ref[c] = lanes(alpha, 2 * LANES) * acc_ref[c] + jnp.dot(p, vaug, preferred_element_type=F32)
            m_ref[c] = m_new

    n_far = jnp.maximum(i - 1, 0)
    n_wide = n_far // FAR_TILES

    def far_wide(kt, carry):
        far_tile(pl.multiple_of(kt * (FAR_TILES * TK), FAR_TILES * TK), FAR_TILES * TK)
        return carry

    def far_rest(kt, carry):
        far_tile(pl.multiple_of(kt * TK, TK), TK)
        return carry

    lax.fori_loop(0, n_wide, far_wide, 0)
    lax.fori_loop(n_wide * FAR_TILES, n_far, far_rest, 0)

    start_e = pl.multiple_of(jnp.maximum(t0 - 2 * TK, 0), TK)
    ql_t = lax.broadcasted_iota(jnp.int32, (sq, TK), 0) % QB
    kl_t = lax.broadcasted_iota(jnp.int32, (sq, TK), 1)
    edge = jnp.where(kl_t > ql_t, 0.0, NEG)
    wv_all = jnp.concatenate([
        jnp.concatenate([wv_ref[pl.ds(st, TK), :], ones_col], axis=1) for st in (start_d, start_p, start_e)], axis=0)
    o_win = []
    for c in range(ncol):
        s0 = _dot_nt(qs[c], wk_ref[pl.ds(start_d, TK), :]) + stacked(d0_ref, c)
        s1 = _dot_nt(qs[c], wk_ref[pl.ds(start_p, TK), :]) + stacked(d1_ref, c)
        s2 = _dot_nt(qs[c], wk_ref[pl.ds(start_e, TK), :]) + edge
        mx = jnp.maximum(jnp.maximum(jnp.max(s0, axis=-1, keepdims=True),
                                     jnp.max(s1, axis=-1, keepdims=True) - off_prev),
                         jnp.max(s2, axis=-1, keepdims=True) - off_edge)
        p = jnp.exp(jnp.concatenate([s0 - mx, s1 - (mx + off_prev), s2 - (mx + off_edge)], axis=1)).astype(BF16)
        ow = jnp.dot(p, wv_all, preferred_element_type=F32)
        o_win.append(jnp.where(low, ow[:QB, :LANES] / ow[:QB, LANES:], ow[QB:, :LANES] / ow[QB:, LANES:]))

    gate = jax.nn.sigmoid(gt_ref[...].astype(F32))
    g_hi = gate.astype(BF16)
    g_lo = (gate - g_hi.astype(F32)).astype(BF16)
    gexp = (jnp.dot(g_hi, e_ref[...], preferred_element_type=F32)
            + jnp.dot(g_lo, e_ref[...], preferred_element_type=F32))
    for c in range(ncol):
        acc = acc_ref[c]
        o_sel = jnp.where(low, acc[:QB, :LANES] / acc[:QB, LANES:], acc[QB:, :LANES] / acc[QB:, LANES:])
        gc = lambda br: gexp[:, (br * ncol + c) * LANES:(br * ncol + c + 1) * LANES]
        o_ref[:, c * LANES:(c + 1) * LANES] = (gc(0) * o_cmp[c] + gc(1) * o_sel + gc(2) * o_win[c]).astype(BF16)


def _nsa(proj3, kcmp, vcmp, onehot, overlap, band, d0, d1, tc, expand):
    b, seq, _ = proj3.shape
    ncmp = kcmp.shape[1]
    col = lambda width, cidx: pl.BlockSpec((None, QB, width), lambda bi, qi: (bi, qi, cidx))
    res = lambda cidx: pl.BlockSpec((None, seq, LANES), lambda bi, qi: (bi, 0, cidx))
    full = lambda arr: pl.BlockSpec(arr.shape, lambda bi, qi: (0,) * arr.ndim)
    cmp_spec = pl.BlockSpec((None, ncmp, LANES), lambda bi, qi: (bi, 0, 0))
    return pl.pallas_call(
        functools.partial(_nsa_kernel, ncmp=ncmp),
        grid=(b, seq // QB),
        in_specs=[col(NSA_W, COL_NQ * LANES // NSA_W), col(LANES, COL_GATE),
                  res(COL_SELK), res(COL_SELV), res(COL_WINK), res(COL_WINV),
                  cmp_spec, cmp_spec, full(onehot), full(overlap), full(band), full(d0), full(d1), full(tc),
                  full(expand)],
        out_specs=pl.BlockSpec((None, QB, NSA_W), lambda bi, qi: (bi, qi, 0)),
        out_shape=jax.ShapeDtypeStruct((b, seq, NSA_W), BF16),
        scratch_shapes=[pltpu.VMEM((NSA_W // LANES, 2 * QB, 2 * LANES), F32),
                        pltpu.VMEM((NSA_W // LANES, 2 * QB, LANES), F32)],
        compiler_params=_cparams(("arbitrary", "arbitrary")),
        name="nsa_attention",
    )(proj3, proj3, proj3, proj3, proj3, proj3, kcmp, vcmp, onehot, overlap, band, d0, d1, tc, expand)


def _gmlp_kernel(u_ref, v_ref, ws_ref, b_ref, o_ref):
    lane = lax.broadcasted_iota(jnp.int32, (GM_CHUNK, LANES), 1)
    low = lane < GM_DIM
    row = lax.broadcasted_iota(jnp.int32, (GM_CHUNK, GM_CHUNK), 0)
    colm = lax.broadcasted_iota(jnp.int32, (GM_CHUNK, GM_CHUNK), 1)
    tril = colm <= row
    ones_bd = _head_ones()
    for cc in range(GM_W // LANES):
        sl = slice(cc * LANES, (cc + 1) * LANES)
        u = jax.nn.gelu(u_ref[:, sl].astype(F32), approximate=True)
        v = jax.nn.gelu(v_ref[:, sl].astype(F32), approximate=True)
        vn = _head_rmsnorm(v, ones_bd).astype(BF16)
        w = jnp.concatenate([jnp.where(tril, ws_ref[2 * cc], 0.0), jnp.where(tril, ws_ref[2 * cc + 1], 0.0)],
                            axis=0).astype(BF16)
        r = jnp.dot(w, vn, preferred_element_type=F32)
        sv = jnp.where(low, r[:GM_CHUNK], r[GM_CHUNK:]) + b_ref[:, sl]
        o_ref[:, sl] = (u * sv).astype(BF16)


def _gmlp(proj3, ws, bexp):
    b, seq, _ = proj3.shape
    return pl.pallas_call(
        _gmlp_kernel,
        grid=(b, seq // GM_CHUNK),
        in_specs=[pl.BlockSpec((None, GM_CHUNK, GM_W), lambda bi, ti: (bi, ti, COL_GMU * LANES // GM_W)),
                  pl.BlockSpec((None, GM_CHUNK, GM_W), lambda bi, ti: (bi, ti, COL_GMV * LANES // GM_W)),
                  pl.BlockSpec((N_GM, GM_CHUNK, GM_CHUNK), lambda bi, ti: (0, 0, 0)),
                  pl.BlockSpec((GM_CHUNK, GM_W), lambda bi, ti: (0, 0))],
        out_specs=pl.BlockSpec((None, GM_CHUNK, GM_W), lambda bi, ti: (bi, ti, 0)),
        out_shape=jax.ShapeDtypeStruct((b, seq, GM_W), BF16),
        compiler_params=_cparams(("arbitrary", "arbitrary")),
        name="gmlp",
    )(proj3, proj3, ws, bexp)


def _outproj_kernel(x_ref, r_ref, n_ref, g_ref, w_ref, o_ref):
    acc = jnp.dot(r_ref[...], w_ref[0:RET_W, :], preferred_element_type=F32)
    acc = acc + jnp.dot(n_ref[...], w_ref[RET_W:RET_W + NSA_W, :], preferred_element_type=F32)
    acc = acc + jnp.dot(g_ref[...], w_ref[RET_W + NSA_W:, :], preferred_element_type=F32)
    o_ref[...] = x_ref[...] + acc


def _outproj(x2d, ret_o, nsa_o, gm_o, w_out):
    m = x2d.shape[0]
    tm = 512
    row = lambda width: pl.BlockSpec((tm, width), lambda i: (i, 0))
    return pl.pallas_call(
        _outproj_kernel,
        grid=(m // tm,),
        in_specs=[row(D_MODEL), row(RET_W), row(NSA_W), row(GM_W),
                  pl.BlockSpec((D_MODEL, D_MODEL), lambda i: (0, 0))],
        out_specs=row(D_MODEL),
        out_shape=jax.ShapeDtypeStruct((m, D_MODEL), F32),
        compiler_params=_cparams(("arbitrary",)),
        name="outproj",
    )(x2d, ret_o, nsa_o, gm_o, w_out)


FFN_CHUNK = 256


def _ffn_kernel(x_ref, g_ref, wg_ref, wu_ref, wd_ref, o_ref):
    x = x_ref[...]
    ms = jnp.mean(x * x, axis=-1, keepdims=True)
    h = (x * lax.rsqrt(ms + EPS) * g_ref[...]).astype(BF16)
    acc = x
    for f in range(0, D_FF, FFN_CHUNK):
        gate = jnp.dot(h, wg_ref[:, f:f + FFN_CHUNK], preferred_element_type=F32)
        up = jnp.dot(h, wu_ref[:, f:f + FFN_CHUNK], preferred_element_type=F32)
        act = (gate * jax.nn.sigmoid(gate) * up).astype(BF16)
        acc = acc + jnp.dot(act, wd_ref[f:f + FFN_CHUNK, :], preferred_element_type=F32)
    o_ref[...] = acc


def _ffn(x2d, gain, wg, wu, wd):
    m = x2d.shape[0]
    tm = 512
    const = lambda arr: pl.BlockSpec(arr.shape, lambda i: (0, 0), pipeline_mode=pl.Buffered(1))
    return pl.pallas_call(
        _ffn_kernel,
        grid=(m // tm,),
        in_specs=[pl.BlockSpec((tm, D_MODEL), lambda i: (i, 0)),
                  pl.BlockSpec((1, D_MODEL), lambda i: (0, 0)),
                  const(wg), const(wu), const(wd)],
        out_specs=pl.BlockSpec((tm, D_MODEL), lambda i: (i, 0)),
        out_shape=jax.ShapeDtypeStruct((m, D_MODEL), F32),
        compiler_params=_cparams(("arbitrary",)),
        name="ffn",
    )(x2d, gain, wg, wu, wd)


def _nq_perm():
    idx = np.zeros(NSA_W, np.int64)
    for c in range(NSA_W // LANES):
        for half in range(2):
            head = c + (N_NSA // N_KV) * half
            idx[c * LANES + half * HEAD_DIM:c * LANES + (half + 1) * HEAD_DIM] = head * HEAD_DIM + np.arange(HEAD_DIM)
    return idx


def _w_in_layout(w):
    off_nq = 4 * RET_W
    off_kv = off_nq + NSA_W
    off_gate = off_kv + 3 * 2 * N_KV * HEAD_DIM
    off_gm = off_gate + 3 * N_NSA
    pad = jnp.zeros((D_MODEL, LANES - 3 * N_NSA), w.dtype)
    return jnp.concatenate([w[:, :off_nq], w[:, off_nq + _nq_perm()], w[:, off_kv:off_gate],
                            w[:, off_gate:off_gm], pad, w[:, off_gm:]], axis=1).astype(BF16)


def _w_out_layout(w):
    return jnp.concatenate([w[:RET_W], w[RET_W + _nq_perm()], w[RET_W + NSA_W:]], axis=0).astype(BF16)


def _gate_expand():
    e = np.zeros((LANES, 3 * NSA_W), np.float32)
    ncol = NSA_W // LANES
    for br in range(3):
        for c in range(ncol):
            for half in range(2):
                head = c + (N_NSA // N_KV) * half
                dst = (br * ncol + c) * LANES + half * HEAD_DIM
                e[head * 3 + br, dst:dst + HEAD_DIM] = 1.0
    return jnp.asarray(e, BF16)


def _rope_tables(seq):
    half = HEAD_DIM // 2
    inv = ROPE_THETA ** (-jnp.arange(half, dtype=F32) / half)
    ang = jnp.arange(seq, dtype=jnp.int32).astype(F32)[:, None] * inv[None, :]
    cos = jnp.tile(jnp.cos(ang), (1, LANES // half))
    sin = jnp.tile(jnp.concatenate([-jnp.sin(ang), jnp.sin(ang)], axis=1), (1, LANES // HEAD_DIM))
    return cos, sin


def _retention_tables():
    c = RET_CHUNK
    log_gamma = jnp.log(1.0 - 2.0 ** (-5.0 - jnp.arange(N_RET, dtype=F32)))
    idx = jnp.arange(c, dtype=F32)
    diff = idx[:, None] - idx[None, :]
    decay = jnp.where(diff >= 0, jnp.exp(jnp.maximum(diff, 0.0)[None] * log_gamma[:, None, None]), 0.0)
    zeta = jnp.exp((c - 1 - idx)[None, :] * log_gamma[:, None])
    xi = jnp.exp((idx + 1)[None, :] * log_gamma[:, None])
    chunk_decay = jnp.exp(c * log_gamma)
    ncol = RET_W // LANES
    dec = decay.reshape(ncol, 2 * c, c)
    xi_st = jnp.broadcast_to(xi.reshape(ncol, 2 * c, 1), (ncol, 2 * c, LANES))
    zeta_l = jnp.repeat(zeta.reshape(ncol, 2, c).transpose(0, 2, 1), HEAD_DIM, axis=2)
    cd = jnp.broadcast_to(jnp.repeat(chunk_decay.reshape(ncol, 2), HEAD_DIM, axis=1)[:, :, None],
                          (ncol, LANES, LANES))
    return dec, xi_st, zeta_l, cd


def _overlap_table(seq):
    n_cmp = (seq - CMP_LEN) // CMP_STRIDE + 1
    n_slc = seq // SEL_BLOCK
    cs = np.arange(n_cmp)[:, None] * CMP_STRIDE
    ss = np.arange(n_slc)[None, :] * SEL_BLOCK
    ov = np.clip(np.minimum(cs + CMP_LEN, ss + SEL_BLOCK) - np.maximum(cs, ss), 0, None) // CMP_STRIDE
    full = np.zeros((seq // CMP_STRIDE, LANES), np.float32)
    full[:n_cmp, :n_slc] = ov
    return jnp.asarray(full, BF16)


def _cmp_band(seq):
    nq = seq // QB
    ncmp = seq // CMP_STRIDE
    x = np.arange(2 * ncmp)[:, None]
    m = np.arange(LANES)[None, :]
    base = (QB // CMP_STRIDE) * nq - 16
    band = np.where(m < LANES - 1, x == m + base, x >= m + base)
    return jnp.asarray(band.astype(np.float32), BF16)


def _block_onehot(seq):
    oh = (np.arange(seq)[:, None] // SEL_BLOCK) == np.arange(LANES)[None, :]
    return jnp.asarray(oh.astype(np.float32), BF16)


def _compress_weights(cmp_pe, cmp_w1, cmp_w2):
    def block_diag2(w):
        zero = jnp.zeros_like(w)
        return jnp.concatenate([jnp.concatenate([w, zero], axis=-1), jnp.concatenate([zero, w], axis=-1)],
                               axis=-2).astype(BF16)

    w1bd = block_diag2(cmp_w1.reshape(2, CMP_LEN, HEAD_DIM, HEAD_DIM))
    w2bd = block_diag2(cmp_w2)
    pe2 = jnp.tile(cmp_pe, (1, 1, N_KV))[:, :, None, :]
    return w1bd[:, :CMP_LEN // 2], w1bd[:, CMP_LEN // 2:], pe2, w2bd


def kernel(x, attn_norm, w_in, w_out, nsa_q_gain, nsa_k_gain, cmp_pe, cmp_w1, cmp_w2, gm_ws, gm_b, ffn_norm,
           w_gate_up, w_down, rel_bias):
    b, seq, _ = x.shape
    assert seq % (2 * TK) == 0 and N_SEL <= seq // SEL_BLOCK <= LANES, "selection blocks must fit one lane row"
    depth = w_in.shape[0]
    cos_tab, sin_tab = _rope_tables(seq)
    ret_tabs = _retention_tables()
    overlap = _overlap_table(seq)
    onehot = _block_onehot(seq)
    band = _cmp_band(seq)
    expand = _gate_expand()
    d0, d1, tc = _bias_tiles(rel_bias)
    ngrp = seq // CMP_STRIDE
    x2d = x.reshape(b * seq, D_MODEL)
    for l in range(depth):
        zero_row = jnp.zeros((5, LANES), F32)
        head_gains = jnp.concatenate([jnp.tile(nsa_q_gain[l] * HEAD_DIM ** -0.5, 2)[None],
                                      jnp.tile(nsa_k_gain[l, 1], 2)[None],
                                      jnp.tile(nsa_k_gain[l, 2], 2)[None], zero_row], axis=0)
        proj, cmp_grp = _inproj(x2d, attn_norm[l][None], _w_in_layout(w_in[l]), cos_tab, sin_tab, head_gains, seq)
        proj3 = proj.reshape(b, seq, PROJ_W)
        ret_o = _retention(proj3, ret_tabs)
        cmp_in = cmp_grp.reshape(b, ngrp, CMP_STRIDE * 2 * LANES)
        wtop, wbot, pe2, w2bd = _compress_weights(cmp_pe[l], cmp_w1[l], cmp_w2[l])
        kgain = jnp.broadcast_to(jnp.tile(nsa_k_gain[l, 0], 2)[None], (8, LANES))
        kcmp, vcmp = _compress(cmp_in, wtop, wbot, pe2, w2bd, kgain)
        nsa_o = _nsa(proj3, kcmp, vcmp, onehot, overlap, band, d0, d1, tc, expand)
        gm_o = _gmlp(proj3, gm_ws[l], jnp.repeat(gm_b[l].T, GM_DIM, axis=1))
        x2d = _outproj(x2d, ret_o.reshape(b * seq, RET_W), nsa_o.reshape(b * seq, NSA_W),
                       gm_o.reshape(b * seq, GM_W), _w_out_layout(w_out[l]))
        x2d = _ffn(x2d, ffn_norm[l][None], w_gate_up[l, :, :D_FF].astype(BF16),
                   w_gate_up[l, :, D_FF:].astype(BF16), w_down[l].astype(BF16))
    return x2d.reshape(b, seq, D_MODEL)
```

```python
import functools
import math

import numpy as np
import jax
import jax.numpy as jnp
from jax import lax
from jax.experimental import pallas as pl
from jax.experimental.pallas import tpu as pltpu

F32 = jnp.float32
BF16 = jnp.bfloat16

D_MODEL = 1024
HEAD_DIM = 64
N_RET = 6
N_NSA = 6
N_KV = 2
N_GM = 4
GM_DIM = 64
RET_W = N_RET * HEAD_DIM
NSA_W = N_NSA * HEAD_DIM
GM_W = N_GM * GM_DIM
RET_CHUNK = 128
CMP_LEN = 32
CMP_STRIDE = 16
SEL_BLOCK = 64
N_SEL = 16
WINDOW = 512
GM_CHUNK = 128
N_BUCKETS = 32
MAX_DISTANCE = 128
ROPE_THETA = 10000.0
D_FF = 2816
EPS = 1e-6
BIG = 1e9
NEG = -1e30

LANES = 128
PROJ_W = 3328
COL_RQ, COL_RK, COL_RV, COL_RG = 0, 3, 6, 9
COL_NQ = 12
COL_CMPK, COL_CMPV, COL_SELK, COL_SELV, COL_WINK, COL_WINV = 15, 16, 17, 18, 19, 20
COL_GATE = 21
COL_GMU, COL_GMV = 22, 24
QB = 256
TK = 256
FAR_TILES = 4
INPROJ_SPLIT = 2
RET_STEP = 512
GM_STEP = 512
VMEM_LIMIT = 56 * 1024 * 1024


def _bucket_thresholds():
    n = np.arange(0, 4 * MAX_DISTANCE)
    max_exact = N_BUCKETS // 2
    nf = np.maximum(n, 1).astype(np.float64)
    large = max_exact + np.floor(np.log(nf / max_exact) / math.log(MAX_DISTANCE / max_exact)
                                 * (N_BUCKETS - max_exact)).astype(np.int64)
    bucket = np.where(n < max_exact, n, np.minimum(large, N_BUCKETS - 1))
    return [int(np.min(n[bucket >= b])) for b in range(N_BUCKETS)]


BUCKET_THR = _bucket_thresholds()


def _cparams(sem):
    return pltpu.CompilerParams(dimension_semantics=sem, vmem_limit_bytes=VMEM_LIMIT)


def _head_ones():
    r = lax.broadcasted_iota(jnp.int32, (LANES, LANES), 0) // HEAD_DIM
    c = lax.broadcasted_iota(jnp.int32, (LANES, LANES), 1) // HEAD_DIM
    return jnp.where(r == c, 1.0, 0.0).astype(BF16)


def _head_sum(x, ones_bd):
    return jnp.dot(x.astype(BF16), ones_bd, preferred_element_type=F32)


def _head_rmsnorm(x, ones_bd):
    return x * lax.rsqrt(_head_sum(x * x, ones_bd) * (1.0 / HEAD_DIM) + EPS)


def _dot_nt(a, b):
    return lax.dot_general(a, b, (((1,), (1,)), ((), ())), preferred_element_type=F32)


def _dot_tn(a, b):
    return lax.dot_general(a, b, (((0,), (0,)), ((), ())), preferred_element_type=F32)


def _stack_heads(q, low):
    zero = jnp.zeros_like(q)
    return jnp.concatenate([jnp.where(low, q, zero), jnp.where(low, zero, q)], axis=0)


def _inproj_kernel(x_ref, g_ref, w_ref, cos_ref, sin_ref, hg_ref, o_ref, grp_ref, cmp_scr, *, tm):
    sub_rows = tm // INPROJ_SPLIT
    lane = lax.broadcasted_iota(jnp.int32, (sub_rows, LANES), 1)
    first_half = (lane % HEAD_DIM) < (HEAD_DIM // 2)
    ones_bd = _head_ones()
    norm_gain = {COL_NQ: 0, COL_NQ + 1: 0, COL_NQ + 2: 0, COL_SELK: 1, COL_WINK: 2}
    for r0 in range(0, tm, sub_rows):
        rows = slice(r0, r0 + sub_rows)
        x = x_ref[rows, :]
        ms = jnp.mean(x * x, axis=-1, keepdims=True)
        h = (x * lax.rsqrt(ms + EPS) * g_ref[...]).astype(BF16)
        cos = cos_ref[rows, :]
        sin = sin_ref[rows, :]
        for s in range(0, PROJ_W // LANES, 2):
            acc = _dot_nt(h, w_ref[s * LANES:(s + 2) * LANES, :])
            for sub in range(2):
                j = s + sub
                a = acc[:, sub * LANES:(sub + 1) * LANES]
                if j < COL_RV:
                    swapped = jnp.where(first_half, pltpu.roll(a, LANES - HEAD_DIM // 2, axis=1),
                                        pltpu.roll(a, HEAD_DIM // 2, axis=1))
                    a = a * cos + swapped * sin
                    if j >= COL_RK:
                        a = a * (HEAD_DIM ** -0.5)
                elif j in norm_gain:
                    a = _head_rmsnorm(a, ones_bd) * hg_ref[norm_gain[j]:norm_gain[j] + 1, :]
                elif j in (COL_CMPK, COL_CMPV):
                    cmp_scr[j - COL_CMPK, rows, :] = a
                o_ref[rows, j * LANES:(j + 1) * LANES] = a.astype(BF16)
    for l in range(CMP_STRIDE):
        for kv in range(2):
            grp_ref[:, (2 * l + kv) * LANES:(2 * l + kv + 1) * LANES] = (
                cmp_scr[kv, pl.ds(l, tm // CMP_STRIDE, stride=CMP_STRIDE), :].astype(BF16))


def _inproj(x2d, gain, w1, cos_tab, sin_tab, head_gains, seq):
    m = x2d.shape[0]
    tm = 1024
    nt = seq // tm
    return pl.pallas_call(
        functools.partial(_inproj_kernel, tm=tm),
        grid=(m // tm,),
        in_specs=[
            pl.BlockSpec((tm, D_MODEL), lambda i: (i, 0)),
            pl.BlockSpec((1, D_MODEL), lambda i: (0, 0)),
            pl.BlockSpec((PROJ_W, D_MODEL), lambda i: (0, 0), pipeline_mode=pl.Buffered(1)),
            pl.BlockSpec((tm, LANES), lambda i: (i % nt, 0)),
            pl.BlockSpec((tm, LANES), lambda i: (i % nt, 0)),
            pl.BlockSpec((8, LANES), lambda i: (0, 0)),
        ],
        out_specs=[pl.BlockSpec((tm, PROJ_W), lambda i: (i, 0)),
                   pl.BlockSpec((tm // CMP_STRIDE, CMP_STRIDE * 2 * LANES), lambda i: (i, 0))],
        out_shape=[jax.ShapeDtypeStruct((m, PROJ_W), BF16),
                   jax.ShapeDtypeStruct((m // CMP_STRIDE, CMP_STRIDE * 2 * LANES), BF16)],
        scratch_shapes=[pltpu.VMEM((2, tm, LANES), F32)],
        compiler_params=_cparams(("arbitrary",)),
        name="inproj",
    )(x2d, gain, w1, cos_tab, sin_tab, head_gains)


def _ret_kernel(q_ref, k_ref, v_ref, g_ref, dec_ref, xi_ref, zeta_ref, cd_ref, o_ref, r_ref):
    @pl.when(pl.program_id(1) == 0)
    def _():
        r_ref[...] = jnp.zeros_like(r_ref)

    lane = lax.broadcasted_iota(jnp.int32, (RET_CHUNK, LANES), 1)
    low = lane < HEAD_DIM
    ones_bd = _head_ones()
    for c in range(RET_W // LANES):
        sl = slice(c * LANES, (c + 1) * LANES)
        state = r_ref[c]
        for ch in range(RET_STEP // RET_CHUNK):
            rows = slice(ch * RET_CHUNK, (ch + 1) * RET_CHUNK)
            q = q_ref[rows, sl]
            k = k_ref[rows, sl]
            v = v_ref[rows, sl]
            g = g_ref[rows, sl].astype(F32)
            qs = _stack_heads(q, low)
            scores = _dot_nt(qs, k) * dec_ref[c]
            inner = jnp.dot(scores.astype(BF16), v, preferred_element_type=F32)
            cross = jnp.dot(qs, state.astype(BF16), preferred_element_type=F32) * xi_ref[c]
            tot = inner + cross
            o = jnp.where(low, tot[:RET_CHUNK], tot[RET_CHUNK:])
            kz = (k.astype(F32) * zeta_ref[c]).astype(BF16)
            state = cd_ref[c] * state + _dot_tn(kz, v)
            y = _head_rmsnorm(o, ones_bd) * (g * jax.nn.sigmoid(g))
            o_ref[rows, sl] = y.astype(BF16)
        r_ref[c] = state


def _retention(proj3, tabs):
    b, seq, _ = proj3.shape
    dec, xi, zeta, cd = tabs
    ncol = RET_W // LANES
    qkvg = [pl.BlockSpec((None, RET_STEP, RET_W), functools.partial(lambda bi, ti, col: (bi, ti, col), col=col))
            for col in range(4)]
    const3 = lambda shape: pl.BlockSpec(shape, lambda bi, ti: (0, 0, 0))
    return pl.pallas_call(
        _ret_kernel,
        grid=(b, seq // RET_STEP),
        in_specs=qkvg + [const3((ncol, 2 * RET_CHUNK, LANES)), const3((ncol, 2 * RET_CHUNK, LANES)),
                         const3((ncol, RET_CHUNK, LANES)), const3((ncol, LANES, LANES))],
        out_specs=pl.BlockSpec((None, RET_STEP, RET_W), lambda bi, ti: (bi, ti, 0)),
        out_shape=jax.ShapeDtypeStruct((b, seq, RET_W), BF16),
        scratch_shapes=[pltpu.VMEM((ncol, LANES, LANES), F32)],
        compiler_params=_cparams(("arbitrary", "arbitrary")),
        name="retention",
    )(proj3, proj3, proj3, proj3, dec, xi, zeta, cd)


def _compress_kernel(a_ref, wtop_ref, wbot_ref, pe_ref, w2_ref, kg_ref, kc_ref, vc_ref, *, ngrp):
    half = CMP_LEN // 2
    ones_bd = _head_ones()
    for kv in range(2):
        top = jnp.zeros((ngrp, LANES), F32)
        bot = jnp.zeros((ngrp, LANES), F32)
        for l in range(half):
            x = a_ref[:, l * 2 * LANES + kv * LANES:l * 2 * LANES + (kv + 1) * LANES].astype(F32)
            top = top + jnp.dot((x + pe_ref[kv, l]).astype(BF16), wtop_ref[kv, l], preferred_element_type=F32)
            bot = bot + jnp.dot((x + pe_ref[kv, half + l]).astype(BF16), wbot_ref[kv, l],
                                preferred_element_type=F32)
        hid = jax.nn.gelu(top + pltpu.roll(bot, ngrp - 1, axis=0), approximate=True)
        out = jnp.dot(hid.astype(BF16), w2_ref[kv], preferred_element_type=F32)
        if kv == 0:
            kc_ref[...] = (_head_rmsnorm(out, ones_bd) * kg_ref[0:1, :]).astype(BF16)
        else:
            vc_ref[...] = out.astype(BF16)


def _compress(a, wtop, wbot, pe2, w2bd, kgain):
    b, ngrp, width = a.shape
    full = lambda arr: pl.BlockSpec(arr.shape, lambda bi: (0,) * arr.ndim)
    out_spec = pl.BlockSpec((None, ngrp, LANES), lambda bi: (bi, 0, 0))
    return pl.pallas_call(
        functools.partial(_compress_kernel, ngrp=ngrp),
        grid=(b,),
        in_specs=[pl.BlockSpec((None, ngrp, width), lambda bi: (bi, 0, 0)),
                  full(wtop), full(wbot), full(pe2), full(w2bd), full(kgain)],
        out_specs=[out_spec, out_spec],
        out_shape=[jax.ShapeDtypeStruct((b, ngrp, LANES), BF16)] * 2,
        compiler_params=_cparams(("arbitrary",)),
        name="nsa_compress",
    )(a, wtop, wbot, pe2, w2bd, kgain)


def _bias_kernel(rb_ref, d0_ref, d1_ref, tc_ref):
    h = pl.program_id(0)
    far = rb_ref[N_BUCKETS - 1, h]

    def rel(n):
        val = jnp.full(n.shape, rb_ref[0, h] - far, F32)
        for bkt in range(1, N_BUCKETS):
            val = jnp.where(n >= BUCKET_THR[bkt], rb_ref[bkt, h] - far, val)
        return val

    ql = lax.broadcasted_iota(jnp.int32, (QB, TK), 0)
    kl = lax.broadcasted_iota(jnp.int32, (QB, TK), 1)
    d = ql - kl
    d0_ref[...] = jnp.where(d >= 0, rel(d), NEG)
    d1_ref[...] = rel(d + TK)
    ql2 = lax.broadcasted_iota(jnp.int32, (QB, LANES), 0)
    m = lax.broadcasted_iota(jnp.int32, (QB, LANES), 1)
    dc = ql2 - CMP_STRIDE * (m - 16) - (CMP_LEN - 1)
    tc_ref[...] = jnp.where(dc >= 0, rel(dc), NEG).astype(BF16)


def _bias_tiles(rel_bias):
    return pl.pallas_call(
        _bias_kernel,
        grid=(N_NSA,),
        in_specs=[pl.BlockSpec(memory_space=pltpu.SMEM)],
        out_specs=[pl.BlockSpec((None, QB, TK), lambda h: (h, 0, 0)),
                   pl.BlockSpec((None, QB, TK), lambda h: (h, 0, 0)),
                   pl.BlockSpec((None, QB, LANES), lambda h: (h, 0, 0))],
        out_shape=[jax.ShapeDtypeStruct((N_NSA, QB, TK), F32),
                   jax.ShapeDtypeStruct((N_NSA, QB, TK), F32),
                   jax.ShapeDtypeStruct((N_NSA, QB, LANES), BF16)],
        compiler_params=_cparams(("arbitrary",)),
        name="t5_bias_tiles",
    )(rel_bias)


def _nsa_kernel(q_ref, gt_ref, sk_ref, sv_ref, wk_ref, wv_ref, kc_ref, vc_ref, oh_ref, ov_ref, band_ref,
                d0_ref, d1_ref, tc_ref, e_ref, o_ref, acc_ref, m_ref, *, ncmp):
    i = pl.program_id(1)
    t0 = i * QB
    ncol = NSA_W // LANES
    sq = 2 * QB
    low = lax.broadcasted_iota(jnp.int32, (QB, LANES), 1) < HEAD_DIM
    qs = [_stack_heads(q_ref[:, c * LANES:(c + 1) * LANES], low) for c in range(ncol)]

    def stacked(ref, c):
        return jnp.concatenate([ref[c], ref[c + ncol]], axis=0)

    ones_col = jnp.ones((TK, LANES), BF16)

    band = band_ref[pl.ds(pl.multiple_of((QB // CMP_STRIDE) * (pl.num_programs(1) - i), CMP_STRIDE), ncmp), :]
    kc_aug = jnp.concatenate([kc_ref[...], band], axis=1)
    vc_ov = jnp.concatenate([vc_ref[...], ov_ref[...]], axis=1)
    imp = jnp.zeros((sq, LANES), F32)
    o_cmp = []
    s_cmp = [_dot_nt(jnp.concatenate([qs[c], stacked(tc_ref, c)], axis=1), kc_aug) for c in range(ncol)]

    start_d = pl.multiple_of(t0, TK)
    start_p = pl.multiple_of(jnp.maximum(t0 - TK, 0), TK)
    start_e = pl.multiple_of(jnp.maximum(t0 - 2 * TK, 0), TK)
    s_win = [[_dot_nt(qs[c], wk_ref[pl.ds(st, TK), :]) for st in (start_d, start_p, start_e)] for c in range(ncol)]

    for c in range(ncol):
        s = s_cmp[c]
        mx = jnp.maximum(jnp.max(s, axis=-1, keepdims=True), -1e20)
        e = jnp.exp(s - mx)
        den = jnp.sum(e, axis=-1, keepdims=True)
        inv = jnp.where(den > 0.0, 1.0 / den, 0.0)
        both = jnp.dot(e.astype(BF16), vc_ov, preferred_element_type=F32) * inv
        imp = imp + both[:, LANES:]
        o_cmp.append(jnp.where(low, both[:QB, :LANES], both[QB:, :LANES]))

    off_prev = jnp.where(i >= 1, 0.0, -NEG)
    off_edge = jnp.where(i >= 2, 0.0, -NEG)
    ql_t = lax.broadcasted_iota(jnp.int32, (sq, TK), 0) % QB
    kl_t = lax.broadcasted_iota(jnp.int32, (sq, TK), 1)
    edge = jnp.where(kl_t > ql_t, 0.0, NEG)
    wv_all = jnp.concatenate([
        jnp.concatenate([wv_ref[pl.ds(st, TK), :], ones_col], axis=1) for st in (start_d, start_p, start_e)], axis=0)
    o_win = []
    for c in range(ncol):
        s0 = s_win[c][0] + stacked(d0_ref, c)
        s1 = s_win[c][1] + stacked(d1_ref, c)
        s2 = s_win[c][2] + edge
        mx = jnp.maximum(jnp.maximum(jnp.max(s0, axis=-1, keepdims=True),
                                     jnp.max(s1, axis=-1, keepdims=True) - off_prev),
                         jnp.max(s2, axis=-1, keepdims=True) - off_edge)
        p = jnp.exp(jnp.concatenate([s0 - mx, s1 - (mx + off_prev), s2 - (mx + off_edge)], axis=1)).astype(BF16)
        ow = jnp.dot(p, wv_all, preferred_element_type=F32)
        o_win.append(jnp.where(low, ow[:QB, :LANES] / ow[:QB, LANES:], ow[QB:, :LANES] / ow[QB:, LANES:]))

    imp_t = imp.T
    blk = lax.broadcasted_iota(jnp.int32, (LANES, sq), 0)
    cur = (t0 + lax.broadcasted_iota(jnp.int32, (LANES, sq), 1) % QB) // SEL_BLOCK
    forced = (blk == 0) | (blk == cur) | (blk == cur - 1)
    imp_t = jnp.where(forced, -jnp.inf, jnp.where(blk > cur, -BIG, imp_t))
    blk_f = blk.astype(F32)

    def pick(_, carry):
        val, chosen = carry
        top = jnp.max(val, axis=0, keepdims=True)
        idx = jnp.min(jnp.where(val == top, blk_f, 1e6), axis=0, keepdims=True)
        hit = blk_f == idx
        return jnp.where(hit, -jnp.inf, val), jnp.where(hit, 0.0, chosen)

    _, msel_t = lax.fori_loop(0, N_SEL - 3, pick, (imp_t, jnp.where(forced, 0.0, NEG)))
    msel = msel_t.T.astype(BF16)

    qaug = [jnp.concatenate([qs[c], msel], axis=1) for c in range(ncol)]

    def sel_scores(c, start, width):
        kaug = jnp.concatenate([sk_ref[pl.ds(start, width), :], oh_ref[pl.ds(start, width), :]], axis=1)
        return _dot_nt(qaug[c], kaug)

    def sel_vaug(start, width):
        return jnp.concatenate([sv_ref[pl.ds(start, width), :], jnp.ones((width, LANES), BF16)], axis=1)

    def lanes(x, width):
        return jnp.concatenate([x] * (width // LANES), axis=1)

    s_near = [(sel_scores(c, start_d, TK), sel_scores(c, start_p, TK)) for c in range(ncol)]
    for c in range(ncol):
        s0 = s_near[c][0] + stacked(d0_ref, c)
        s1 = s_near[c][1] + stacked(d1_ref, c)
        mx = jnp.maximum(jnp.max(s0, axis=-1, keepdims=True), jnp.max(s1, axis=-1, keepdims=True) - off_prev)
        p0 = jnp.exp(s0 - mx).astype(BF16)
        p1 = jnp.exp(s1 - (mx + off_prev)).astype(BF16)
        acc_ref[c] = (jnp.dot(p0, sel_vaug(start_d, TK), preferred_element_type=F32)
                      + jnp.dot(p1, sel_vaug(start_p, TK), preferred_element_type=F32))
        m_ref[c] = jnp.broadcast_to(mx, (sq, LANES))

    def far_tile(start, width):
        vaug = sel_vaug(start, width)
        scores = [sel_scores(c, start, width) for c in range(ncol)]
        for c in range(ncol):
            s = scores[c]
            m_old = m_ref[c]
            m_new = jnp.maximum(m_old, jnp.max(s, axis=-1, keepdims=True))
            alpha = jnp.exp(m_old - m_new)
            p = jnp.exp(s - lanes(m_new, width)).astype(BF16)
            acc_ref[c] = lanes(alpha, 2 * LANES) * acc_ref[c] + jnp.dot(p, vaug, preferred_element_type=F32)
            m_ref[c] = m_new

    n_far = jnp.maximum(i - 1, 0)
    n_wide = n_far // FAR_TILES

    def far_wide(kt, carry):
        far_tile(pl.multiple_of(kt * (FAR_TILES * TK), FAR_TILES * TK), FAR_TILES * TK)
        return carry

    def far_rest(kt, carry):
        far_tile(pl.multiple_of(kt * TK, TK), TK)
        return carry

    lax.fori_loop(0, n_wide, far_wide, 0)
    lax.fori_loop(n_wide * FAR_TILES, n_far, far_rest, 0)

    gate = jax.nn.sigmoid(gt_ref[...].astype(F32)).astype(BF16)
    gexp = jnp.dot(gate, e_ref[...], preferred_element_type=F32)
    for c in range(ncol):
        acc = acc_ref[c]
        o_sel = jnp.where(low, acc[:QB, :LANES] / acc[:QB, LANES:], acc[QB:, :LANES] / acc[QB:, LANES:])
        gc = lambda br: gexp[:, (br * ncol + c) * LANES:(br * ncol + c + 1) * LANES]
        o_ref[:, c * LANES:(c + 1) * LANES] = (gc(0) * o_cmp[c] + gc(1) * o_sel + gc(2) * o_win[c]).astype(BF16)


def _nsa(proj3, kcmp, vcmp, onehot, overlap, band, d0, d1, tc, expand):
    b, seq, _ = proj3.shape
    ncmp = kcmp.shape[1]
    col = lambda width, cidx: pl.BlockSpec((None, QB, width), lambda bi, qi: (bi, qi, cidx))
    res = lambda cidx: pl.BlockSpec((None, seq, LANES), lambda bi, qi: (bi, 0, cidx))
    full = lambda arr: pl.BlockSpec(arr.shape, lambda bi, qi: (0,) * arr.ndim)
    cmp_spec = pl.BlockSpec((None, ncmp, LANES), lambda bi, qi: (bi, 0, 0))
    return pl.pallas_call(
        functools.partial(_nsa_kernel, ncmp=ncmp),
        grid=(b, seq // QB),
        in_specs=[col(NSA_W, COL_NQ * LANES // NSA_W), col(LANES, COL_GATE),
                  res(COL_SELK), res(COL_SELV), res(COL_WINK), res(COL_WINV),
                  cmp_spec, cmp_spec, full(onehot), full(overlap), full(band), full(d0), full(d1), full(tc),
                  full(expand)],
        out_specs=pl.BlockSpec((None, QB, NSA_W), lambda bi, qi: (bi, qi, 0)),
        out_shape=jax.ShapeDtypeStruct((b, seq, NSA_W), BF16),
        scratch_shapes=[pltpu.VMEM((NSA_W // LANES, 2 * QB, 2 * LANES), F32),
                        pltpu.VMEM((NSA_W // LANES, 2 * QB, LANES), F32)],
        compiler_params=_cparams(("arbitrary", "arbitrary")),
        name="nsa_attention",
    )(proj3, proj3, proj3, proj3, proj3, proj3, kcmp, vcmp, onehot, overlap, band, d0, d1, tc, expand)


def _gmlp_kernel(u_ref, v_ref, ws_ref, b_ref, o_ref):
    lane = lax.broadcasted_iota(jnp.int32, (GM_CHUNK, LANES), 1)
    low = lane < GM_DIM
    row = lax.broadcasted_iota(jnp.int32, (GM_CHUNK, GM_CHUNK), 0)
    colm = lax.broadcasted_iota(jnp.int32, (GM_CHUNK, GM_CHUNK), 1)
    tril = colm <= row
    ones_bd = _head_ones()
    for cc in range(GM_W // LANES):
        sl = slice(cc * LANES, (cc + 1) * LANES)
        w = jnp.concatenate([jnp.where(tril, ws_ref[2 * cc], 0.0), jnp.where(tril, ws_ref[2 * cc + 1], 0.0)],
                            axis=0).astype(BF16)
        for ch in range(GM_STEP // GM_CHUNK):
            rows = slice(ch * GM_CHUNK, (ch + 1) * GM_CHUNK)
            u = jax.nn.gelu(u_ref[rows, sl].astype(F32), approximate=True)
            v = jax.nn.gelu(v_ref[rows, sl].astype(F32), approximate=True)
            vn = _head_rmsnorm(v, ones_bd).astype(BF16)
            r = jnp.dot(w, vn, preferred_element_type=F32)
            sv = jnp.where(low, r[:GM_CHUNK], r[GM_CHUNK:]) + b_ref[:, sl]
            o_ref[rows, sl] = (u * sv).astype(BF16)


def _gmlp(proj3, ws, bexp):
    b, seq, _ = proj3.shape
    return pl.pallas_call(
        _gmlp_kernel,
        grid=(b, seq // GM_STEP),
        in_specs=[pl.BlockSpec((None, GM_STEP, GM_W), lambda bi, ti: (bi, ti, COL_GMU * LANES // GM_W)),
                  pl.BlockSpec((None, GM_STEP, GM_W), lambda bi, ti: (bi, ti, COL_GMV * LANES // GM_W)),
                  pl.BlockSpec((N_GM, GM_CHUNK, GM_CHUNK), lambda bi, ti: (0, 0, 0)),
                  pl.BlockSpec((GM_CHUNK, GM_W), lambda bi, ti: (0, 0))],
        out_specs=pl.BlockSpec((None, GM_STEP, GM_W), lambda bi, ti: (bi, ti, 0)),
        out_shape=jax.ShapeDtypeStruct((b, seq, GM_W), BF16),
        compiler_params=_cparams(("arbitrary", "arbitrary")),
        name="gmlp",
    )(proj3, proj3, ws, bexp)


FFN_CHUNK = 256


def _ffn_kernel(x_ref, r_ref, n_ref, m_ref, wo_ref, g_ref, wg_ref, wu_ref, wd_ref, o_ref):
    mix = jnp.dot(r_ref[...], wo_ref[0:RET_W, :], preferred_element_type=F32)
    mix = mix + jnp.dot(n_ref[...], wo_ref[RET_W:RET_W + NSA_W, :], preferred_element_type=F32)
    mix = mix + jnp.dot(m_ref[...], wo_ref[RET_W + NSA_W:, :], preferred_element_type=F32)
    x = x_ref[...] + mix
    ms = jnp.mean(x * x, axis=-1, keepdims=True)
    h = (x * lax.rsqrt(ms + EPS) * g_ref[...]).astype(BF16)
    acc = x
    for f in range(0, D_FF, FFN_CHUNK):
        gate = jnp.dot(h, wg_ref[:, f:f + FFN_CHUNK], preferred_element_type=F32)
        up = jnp.dot(h, wu_ref[:, f:f + FFN_CHUNK], preferred_element_type=F32)
        act = (gate * jax.nn.sigmoid(gate) * up).astype(BF16)
        acc = acc + jnp.dot(act, wd_ref[f:f + FFN_CHUNK, :], preferred_element_type=F32)
    o_ref[...] = acc


def _outproj_ffn(x2d, ret_o, nsa_o, gm_o, w_out, gain, wg, wu, wd):
    m = x2d.shape[0]
    tm = 512
    row = lambda width: pl.BlockSpec((tm, width), lambda i: (i, 0))
    const = lambda arr: pl.BlockSpec(arr.shape, lambda i: (0, 0), pipeline_mode=pl.Buffered(1))
    return pl.pallas_call(
        _ffn_kernel,
        grid=(m // tm,),
        in_specs=[row(D_MODEL), row(RET_W), row(NSA_W), row(GM_W), const(w_out),
                  pl.BlockSpec((1, D_MODEL), lambda i: (0, 0)),
                  const(wg), const(wu), const(wd)],
        out_specs=row(D_MODEL),
        out_shape=jax.ShapeDtypeStruct((m, D_MODEL), F32),
        compiler_params=_cparams(("arbitrary",)),
        name="outproj_ffn",
    )(x2d, ret_o, nsa_o, gm_o, w_out, gain, wg, wu, wd)


def _nq_head_order():
    return [c + (N_NSA // N_KV) * half for c in range(NSA_W // LANES) for half in range(2)]


def _w_in_layout(w):
    wt = w.T
    off_nq = 4 * RET_W
    off_kv = off_nq + NSA_W
    off_gate = off_kv + 3 * 2 * N_KV * HEAD_DIM
    off_gm = off_gate + 3 * N_NSA
    pad = jnp.zeros((LANES - 3 * N_NSA, D_MODEL), w.dtype)
    heads = [wt[off_nq + h * HEAD_DIM:off_nq + (h + 1) * HEAD_DIM] for h in _nq_head_order()]
    return jnp.concatenate([wt[:off_nq]] + heads + [wt[off_kv:off_gm], pad, wt[off_gm:]], axis=0).astype(BF16)


def _w_out_layout(w):
    heads = [w[RET_W + h * HEAD_DIM:RET_W + (h + 1) * HEAD_DIM] for h in _nq_head_order()]
    return jnp.concatenate([w[:RET_W]] + heads + [w[RET_W + NSA_W:]], axis=0).astype(BF16)


def _gate_expand():
    e = np.zeros((LANES, 3 * NSA_W), np.float32)
    ncol = NSA_W // LANES
    for br in range(3):
        for c in range(ncol):
            for half in range(2):
                head = c + (N_NSA // N_KV) * half
                dst = (br * ncol + c) * LANES + half * HEAD_DIM
                e[head * 3 + br, dst:dst + HEAD_DIM] = 1.0
    return jnp.asarray(e, BF16)


def _rope_tables(seq):
    half = HEAD_DIM // 2
    inv = ROPE_THETA ** (-jnp.arange(half, dtype=F32) / half)
    ang = jnp.arange(seq, dtype=jnp.int32).astype(F32)[:, None] * inv[None, :]
    cos = jnp.tile(jnp.cos(ang), (1, LANES // half))
    sin = jnp.tile(jnp.concatenate([-jnp.sin(ang), jnp.sin(ang)], axis=1), (1, LANES // HEAD_DIM))
    return cos, sin


def _retention_tables():
    c = RET_CHUNK
    log_gamma = jnp.log(1.0 - 2.0 ** (-5.0 - jnp.arange(N_RET, dtype=F32)))
    idx = jnp.arange(c, dtype=F32)
    diff = idx[:, None] - idx[None, :]
    decay = jnp.where(diff >= 0, jnp.exp(jnp.maximum(diff, 0.0)[None] * log_gamma[:, None, None]), 0.0)
    zeta = jnp.exp((c - 1 - idx)[None, :] * log_gamma[:, None])
    xi = jnp.exp((idx + 1)[None, :] * log_gamma[:, None])
    chunk_decay = jnp.exp(c * log_gamma)
    ncol = RET_W // LANES
    dec = decay.reshape(ncol, 2 * c, c)
    xi_st = jnp.broadcast_to(xi.reshape(ncol, 2 * c, 1), (ncol, 2 * c, LANES))
    zeta_l = jnp.repeat(zeta.reshape(ncol, 2, c).transpose(0, 2, 1), HEAD_DIM, axis=2)
    cd = jnp.broadcast_to(jnp.repeat(chunk_decay.reshape(ncol, 2), HEAD_DIM, axis=1)[:, :, None],
                          (ncol, LANES, LANES))
    return dec, xi_st, zeta_l, cd


def _overlap_table(seq):
    n_cmp = (seq - CMP_LEN) // CMP_STRIDE + 1
    n_slc = seq // SEL_BLOCK
    cs = np.arange(n_cmp)[:, None] * CMP_STRIDE
    ss = np.arange(n_slc)[None, :] * SEL_BLOCK
    ov = np.clip(np.minimum(cs + CMP_LEN, ss + SEL_BLOCK) - np.maximum(cs, ss), 0, None) // CMP_STRIDE
    full = np.zeros((seq // CMP_STRIDE, LANES), np.float32)
    full[:n_cmp, :n_slc] = ov
    return jnp.asarray(full, BF16)


def _cmp_band(seq):
    nq = seq // QB
    ncmp = seq // CMP_STRIDE
    x = np.arange(2 * ncmp)[:, None]
    m = np.arange(LANES)[None, :]
    base = (QB // CMP_STRIDE) * nq - 16
    band = np.where(m < LANES - 1, x == m + base, x >= m + base)
    return jnp.asarray(band.astype(np.float32), BF16)


def _block_onehot(seq):
    oh = (np.arange(seq)[:, None] // SEL_BLOCK) == np.arange(LANES)[None, :]
    return jnp.asarray(oh.astype(np.float32), BF16)


def _compress_weights(cmp_pe, cmp_w1, cmp_w2):
    def block_diag2(w):
        zero = jnp.zeros_like(w)
        return jnp.concatenate([jnp.concatenate([w, zero], axis=-1), jnp.concatenate([zero, w], axis=-1)],
                               axis=-2).astype(BF16)

    w1bd = block_diag2(cmp_w1.reshape(2, CMP_LEN, HEAD_DIM, HEAD_DIM))
    w2bd = block_diag2(cmp_w2)
    pe2 = jnp.tile(cmp_pe, (1, 1, N_KV))[:, :, None, :]
    return w1bd[:, :CMP_LEN // 2], w1bd[:, CMP_LEN // 2:], pe2, w2bd


def kernel(x, attn_norm, w_in, w_out, nsa_q_gain, nsa_k_gain, cmp_pe, cmp_w1, cmp_w2, gm_ws, gm_b, ffn_norm,
           w_gate_up, w_down, rel_bias):
    b, seq, _ = x.shape
    assert seq % (2 * TK) == 0 and N_SEL <= seq // SEL_BLOCK <= LANES, "selection blocks must fit one lane row"
    depth = w_in.shape[0]
    cos_tab, sin_tab = _rope_tables(seq)
    ret_tabs = _retention_tables()
    overlap = _overlap_table(seq)
    onehot = _block_onehot(seq)
    band = _cmp_band(seq)
    expand = _gate_expand()
    d0, d1, tc = _bias_tiles(rel_bias)
    ngrp = seq // CMP_STRIDE
    x2d = x.reshape(b * seq, D_MODEL)
    for l in range(depth):
        zero_row = jnp.zeros((5, LANES), F32)
        head_gains = jnp.concatenate([jnp.tile(nsa_q_gain[l] * HEAD_DIM ** -0.5, 2)[None],
                                      jnp.tile(nsa_k_gain[l, 1], 2)[None],
                                      jnp.tile(nsa_k_gain[l, 2], 2)[None], zero_row], axis=0)
        proj, cmp_grp = _inproj(x2d, attn_norm[l][None], _w_in_layout(w_in[l]), cos_tab, sin_tab, head_gains, seq)
        proj3 = proj.reshape(b, seq, PROJ_W)
        ret_o = _retention(proj3, ret_tabs)
        cmp_in = cmp_grp.reshape(b, ngrp, CMP_STRIDE * 2 * LANES)
        wtop, wbot, pe2, w2bd = _compress_weights(cmp_pe[l], cmp_w1[l], cmp_w2[l])
        kgain = jnp.broadcast_to(jnp.tile(nsa_k_gain[l, 0], 2)[None], (8, LANES))
        kcmp, vcmp = _compress(cmp_in, wtop, wbot, pe2, w2bd, kgain)
        nsa_o = _nsa(proj3, kcmp, vcmp, onehot, overlap, band, d0, d1, tc, expand)
        gm_o = _gmlp(proj3, gm_ws[l], jnp.repeat(gm_b[l].T, GM_DIM, axis=1))
        x2d = _outproj_ffn(x2d, ret_o.reshape(b * seq, RET_W), nsa_o.reshape(b * seq, NSA_W),
                           gm_o.reshape(b * seq, GM_W), _w_out_layout(w_out[l]), ffn_norm[l][None],
                           w_gate_up[l, :, :D_FF].astype(BF16), w_gate_up[l, :, D_FF:].astype(BF16),
                           w_down[l].astype(BF16))
    return x2d.reshape(b, seq, D_MODEL)
```

```python
import functools
import math

import numpy as np
import jax
import jax.numpy as jnp
from jax import lax
from jax.experimental import pallas as pl
from jax.experimental.pallas import tpu as pltpu

F32 = jnp.float32
BF16 = jnp.bfloat16

D_MODEL = 1024
HEAD_DIM = 64
N_RET = 6
N_NSA = 6
N_KV = 2
N_GM = 4
GM_DIM = 64
RET_W = N_RET * HEAD_DIM
NSA_W = N_NSA * HEAD_DIM
GM_W = N_GM * GM_DIM
RET_CHUNK = 128
CMP_LEN = 32
CMP_STRIDE = 16
SEL_BLOCK = 64
N_SEL = 16
WINDOW = 512
GM_CHUNK = 128
N_BUCKETS = 32
MAX_DISTANCE = 128
ROPE_THETA = 10000.0
D_FF = 2816
EPS = 1e-6
BIG = 1e9
NEG = -1e30

LANES = 128
PROJ_W = 3328
COL_RQ, COL_RK, COL_RV, COL_RG = 0, 3, 6, 9
COL_NQ = 12
COL_CMPK, COL_CMPV, COL_SELK, COL_SELV, COL_WINK, COL_WINV = 15, 16, 17, 18, 19, 20
COL_GATE = 21
COL_GMU, COL_GMV = 22, 24
QB = 256
TK = 256
FAR_TILES = 4
INPROJ_SPLIT = 2
RET_STEP = 512
GM_STEP = 512
VMEM_LIMIT = 56 * 1024 * 1024


def _bucket_thresholds():
    n = np.arange(0, 4 * MAX_DISTANCE)
    max_exact = N_BUCKETS // 2
    nf = np.maximum(n, 1).astype(np.float64)
    large = max_exact + np.floor(np.log(nf / max_exact) / math.log(MAX_DISTANCE / max_exact)
                                 * (N_BUCKETS - max_exact)).astype(np.int64)
    bucket = np.where(n < max_exact, n, np.minimum(large, N_BUCKETS - 1))
    return [int(np.min(n[bucket >= b])) for b in range(N_BUCKETS)]


BUCKET_THR = _bucket_thresholds()


def _cparams(sem):
    return pltpu.CompilerParams(dimension_semantics=sem, vmem_limit_bytes=VMEM_LIMIT)


def _head_ones():
    r = lax.broadcasted_iota(jnp.int32, (LANES, LANES), 0) // HEAD_DIM
    c = lax.broadcasted_iota(jnp.int32, (LANES, LANES), 1) // HEAD_DIM
    return jnp.where(r == c, 1.0, 0.0).astype(BF16)


def _head_sum(x, ones_bd):
    return jnp.dot(x.astype(BF16), ones_bd, preferred_element_type=F32)


def _head_rmsnorm(x, ones_bd):
    return x * lax.rsqrt(_head_sum(x * x, ones_bd) * (1.0 / HEAD_DIM) + EPS)


def _dot_nt(a, b):
    return lax.dot_general(a, b, (((1,), (1,)), ((), ())), preferred_element_type=F32)


def _dot_tn(a, b):
    return lax.dot_general(a, b, (((0,), (0,)), ((), ())), preferred_element_type=F32)


def _stack_heads(q, low):
    zero = jnp.zeros_like(q)
    return jnp.concatenate([jnp.where(low, q, zero), jnp.where(low, zero, q)], axis=0)


def _inproj_kernel(x_ref, g_ref, w_ref, cos_ref, sin_ref, hg_ref, o_ref, grp_ref, cmp_scr, *, tm):
    sub_rows = tm // INPROJ_SPLIT
    lane = lax.broadcasted_iota(jnp.int32, (sub_rows, LANES), 1)
    first_half = (lane % HEAD_DIM) < (HEAD_DIM // 2)
    ones_bd = _head_ones()
    norm_gain = {COL_NQ: 0, COL_NQ + 1: 0, COL_NQ + 2: 0, COL_SELK: 1, COL_WINK: 2}
    deferred = []
    for r0 in range(0, tm, sub_rows):
        rows = slice(r0, r0 + sub_rows)
        x = x_ref[rows, :]
        ms = jnp.mean(x * x, axis=-1, keepdims=True)
        h = (x * lax.rsqrt(ms + EPS) * g_ref[...]).astype(BF16)
        cos = cos_ref[rows, :]
        sin = sin_ref[rows, :]
        for s in range(0, PROJ_W // LANES, 2):
            acc = _dot_nt(h, w_ref[s * LANES:(s + 2) * LANES, :])
            for sub in range(2):
                j = s + sub
                a = acc[:, sub * LANES:(sub + 1) * LANES]
                if j < COL_RV:
                    swapped = jnp.where(first_half, pltpu.roll(a, LANES - HEAD_DIM // 2, axis=1),
                                        pltpu.roll(a, HEAD_DIM // 2, axis=1))
                    a = a * cos + swapped * sin
                    if j >= COL_RK:
                        a = a * (HEAD_DIM ** -0.5)
                elif j in norm_gain:
                    deferred.append((rows, j, a))
                    continue
                elif j in (COL_CMPK, COL_CMPV):
                    cmp_scr[j - COL_CMPK, rows, :] = a
                o_ref[rows, j * LANES:(j + 1) * LANES] = a.astype(BF16)
    for rows, j, a in deferred:
        a = _head_rmsnorm(a, ones_bd) * hg_ref[norm_gain[j]:norm_gain[j] + 1, :]
        o_ref[rows, j * LANES:(j + 1) * LANES] = a.astype(BF16)
    for l in range(CMP_STRIDE):
        for kv in range(2):
            grp_ref[:, (2 * l + kv) * LANES:(2 * l + kv + 1) * LANES] = (
                cmp_scr[kv, pl.ds(l, tm // CMP_STRIDE, stride=CMP_STRIDE), :].astype(BF16))


def _inproj(x2d, gain, w1, cos_tab, sin_tab, head_gains, seq):
    m = x2d.shape[0]
    tm = 1024
    nt = seq // tm
    return pl.pallas_call(
        functools.partial(_inproj_kernel, tm=tm),
        grid=(m // tm,),
        in_specs=[
            pl.BlockSpec((tm, D_MODEL), lambda i: (i, 0)),
            pl.BlockSpec((1, D_MODEL), lambda i: (0, 0)),
            pl.BlockSpec((PROJ_W, D_MODEL), lambda i: (0, 0), pipeline_mode=pl.Buffered(1)),
            pl.BlockSpec((tm, LANES), lambda i: (i % nt, 0)),
            pl.BlockSpec((tm, LANES), lambda i: (i % nt, 0)),
            pl.BlockSpec((8, LANES), lambda i: (0, 0)),
        ],
        out_specs=[pl.BlockSpec((tm, PROJ_W), lambda i: (i, 0)),
                   pl.BlockSpec((tm // CMP_STRIDE, CMP_STRIDE * 2 * LANES), lambda i: (i, 0))],
        out_shape=[jax.ShapeDtypeStruct((m, PROJ_W), BF16),
                   jax.ShapeDtypeStruct((m // CMP_STRIDE, CMP_STRIDE * 2 * LANES), BF16)],
        scratch_shapes=[pltpu.VMEM((2, tm, LANES), F32)],
        compiler_params=_cparams(("arbitrary",)),
        name="inproj",
    )(x2d, gain, w1, cos_tab, sin_tab, head_gains)


def _ret_kernel(q_ref, k_ref, v_ref, g_ref, dec_ref, xi_ref, zeta_ref, cd_ref, o_ref, r_ref):
    @pl.when(pl.program_id(1) == 0)
    def _():
        r_ref[...] = jnp.zeros_like(r_ref)

    lane = lax.broadcasted_iota(jnp.int32, (RET_CHUNK, LANES), 1)
    low = lane < HEAD_DIM
    ones_bd = _head_ones()
    for c in range(RET_W // LANES):
        sl = slice(c * LANES, (c + 1) * LANES)
        state = r_ref[c]
        for ch in range(RET_STEP // RET_CHUNK):
            rows = slice(ch * RET_CHUNK, (ch + 1) * RET_CHUNK)
            q = q_ref[rows, sl]
            k = k_ref[rows, sl]
            v = v_ref[rows, sl]
            g = g_ref[rows, sl].astype(F32)
            qs = _stack_heads(q, low)
            scores = _dot_nt(qs, k) * dec_ref[c]
            inner = jnp.dot(scores.astype(BF16), v, preferred_element_type=F32)
            cross = jnp.dot(qs, state.astype(BF16), preferred_element_type=F32) * xi_ref[c]
            tot = inner + cross
            o = jnp.where(low, tot[:RET_CHUNK], tot[RET_CHUNK:])
            kz = (k.astype(F32) * zeta_ref[c]).astype(BF16)
            state = cd_ref[c] * state + _dot_tn(kz, v)
            y = _head_rmsnorm(o, ones_bd) * (g * jax.nn.sigmoid(g))
            o_ref[rows, sl] = y.astype(BF16)
        r_ref[c] = state


def _retention(proj3, tabs):
    b, seq, _ = proj3.shape
    dec, xi, zeta, cd = tabs
    ncol = RET_W // LANES
    qkvg = [pl.BlockSpec((None, RET_STEP, RET_W), functools.partial(lambda bi, ti, col: (bi, ti, col), col=col))
            for col in range(4)]
    const3 = lambda shape: pl.BlockSpec(shape, lambda bi, ti: (0, 0, 0))
    return pl.pallas_call(
        _ret_kernel,
        grid=(b, seq // RET_STEP),
        in_specs=qkvg + [const3((ncol, 2 * RET_CHUNK, LANES)), const3((ncol, 2 * RET_CHUNK, LANES)),
                         const3((ncol, RET_CHUNK, LANES)), const3((ncol, LANES, LANES))],
        out_specs=pl.BlockSpec((None, RET_STEP, RET_W), lambda bi, ti: (bi, ti, 0)),
        out_shape=jax.ShapeDtypeStruct((b, seq, RET_W), BF16),
        scratch_shapes=[pltpu.VMEM((ncol, LANES, LANES), F32)],
        compiler_params=_cparams(("arbitrary", "arbitrary")),
        name="retention",
    )(proj3, proj3, proj3, proj3, dec, xi, zeta, cd)


def _compress_kernel(a_ref, wtop_ref, wbot_ref, pe_ref, w2_ref, kg_ref, kc_ref, vc_ref, *, ngrp):
    half = CMP_LEN // 2
    ones_bd = _head_ones()
    for kv in range(2):
        top = jnp.zeros((ngrp, LANES), F32)
        bot = jnp.zeros((ngrp, LANES), F32)
        for l in range(half):
            x = a_ref[:, l * 2 * LANES + kv * LANES:l * 2 * LANES + (kv + 1) * LANES].astype(F32)
            top = top + jnp.dot((x + pe_ref[kv, l]).astype(BF16), wtop_ref[kv, l], preferred_element_type=F32)
            bot = bot + jnp.dot((x + pe_ref[kv, half + l]).astype(BF16), wbot_ref[kv, l],
                                preferred_element_type=F32)
        hid = jax.nn.gelu(top + pltpu.roll(bot, ngrp - 1, axis=0), approximate=True)
        out = jnp.dot(hid.astype(BF16), w2_ref[kv], preferred_element_type=F32)
        if kv == 0:
            kc_ref[...] = (_head_rmsnorm(out, ones_bd) * kg_ref[0:1, :]).astype(BF16)
        else:
            vc_ref[...] = out.astype(BF16)


def _compress(a, wtop, wbot, pe2, w2bd, kgain):
    b, ngrp, width = a.shape
    full = lambda arr: pl.BlockSpec(arr.shape, lambda bi: (0,) * arr.ndim)
    out_spec = pl.BlockSpec((None, ngrp, LANES), lambda bi: (bi, 0, 0))
    return pl.pallas_call(
        functools.partial(_compress_kernel, ngrp=ngrp),
        grid=(b,),
        in_specs=[pl.BlockSpec((None, ngrp, width), lambda bi: (bi, 0, 0)),
                  full(wtop), full(wbot), full(pe2), full(w2bd), full(kgain)],
        out_specs=[out_spec, out_spec],
        out_shape=[jax.ShapeDtypeStruct((b, ngrp, LANES), BF16)] * 2,
        compiler_params=_cparams(("arbitrary",)),
        name="nsa_compress",
    )(a, wtop, wbot, pe2, w2bd, kgain)


def _bias_kernel(rb_ref, d0_ref, d1_ref, tc_ref):
    h = pl.program_id(0)
    far = rb_ref[N_BUCKETS - 1, h]

    def rel(n):
        val = jnp.full(n.shape, rb_ref[0, h] - far, F32)
        for bkt in range(1, N_BUCKETS):
            val = jnp.where(n >= BUCKET_THR[bkt], rb_ref[bkt, h] - far, val)
        return val

    ql = lax.broadcasted_iota(jnp.int32, (QB, TK), 0)
    kl = lax.broadcasted_iota(jnp.int32, (QB, TK), 1)
    d = ql - kl
    d0_ref[...] = jnp.where(d >= 0, rel(d), NEG)
    d1_ref[...] = rel(d + TK)
    ql2 = lax.broadcasted_iota(jnp.int32, (QB, LANES), 0)
    m = lax.broadcasted_iota(jnp.int32, (QB, LANES), 1)
    dc = ql2 - CMP_STRIDE * (m - 16) - (CMP_LEN - 1)
    tc_ref[...] = jnp.where(dc >= 0, rel(dc), NEG).astype(BF16)


def _bias_tiles(rel_bias):
    return pl.pallas_call(
        _bias_kernel,
        grid=(N_NSA,),
        in_specs=[pl.BlockSpec(memory_space=pltpu.SMEM)],
        out_specs=[pl.BlockSpec((None, QB, TK), lambda h: (h, 0, 0)),
                   pl.BlockSpec((None, QB, TK), lambda h: (h, 0, 0)),
                   pl.BlockSpec((None, QB, LANES), lambda h: (h, 0, 0))],
        out_shape=[jax.ShapeDtypeStruct((N_NSA, QB, TK), F32),
                   jax.ShapeDtypeStruct((N_NSA, QB, TK), F32),
                   jax.ShapeDtypeStruct((N_NSA, QB, LANES), BF16)],
        compiler_params=_cparams(("arbitrary",)),
        name="t5_bias_tiles",
    )(rel_bias)


def _nsa_kernel(q_ref, gt_ref, sk_ref, sv_ref, wk_ref, wv_ref, kc_ref, vc_ref, oh_ref, ov_ref, band_ref,
                d0_ref, d1_ref, tc_ref, e_ref, o_ref, acc_ref, m_ref, *, ncmp):
    i = pl.program_id(1)
    t0 = i * QB
    ncol = NSA_W // LANES
    sq = 2 * QB
    low = lax.broadcasted_iota(jnp.int32, (QB, LANES), 1) < HEAD_DIM
    qs = [_stack_heads(q_ref[:, c * LANES:(c + 1) * LANES], low) for c in range(ncol)]

    def stacked(ref, c):
        return jnp.concatenate([ref[c], ref[c + ncol]], axis=0)

    ones_col = jnp.ones((TK, LANES), BF16)

    band = band_ref[pl.ds(pl.multiple_of((QB // CMP_STRIDE) * (pl.num_programs(1) - i), CMP_STRIDE), ncmp), :]
    kc_aug = jnp.concatenate([kc_ref[...], band], axis=1)
    vc_ov = jnp.concatenate([vc_ref[...], ov_ref[...]], axis=1)
    imp = jnp.zeros((sq, LANES), F32)
    o_cmp = []
    s_cmp = [_dot_nt(jnp.concatenate([qs[c], stacked(tc_ref, c)], axis=1), kc_aug) for c in range(ncol)]

    start_d = pl.multiple_of(t0, TK)
    start_p = pl.multiple_of(jnp.maximum(t0 - TK, 0), TK)
    start_e = pl.multiple_of(jnp.maximum(t0 - 2 * TK, 0), TK)
    s_win = [[_dot_nt(qs[c], wk_ref[pl.ds(st, TK), :]) for st in (start_d, start_p, start_e)] for c in range(ncol)]

    for c in range(ncol):
        s = s_cmp[c]
        mx = jnp.maximum(jnp.max(s, axis=-1, keepdims=True), -1e20)
        e = jnp.exp(s - mx)
        den = jnp.sum(e, axis=-1, keepdims=True)
        inv = jnp.where(den > 0.0, 1.0 / den, 0.0)
        both = jnp.dot(e.astype(BF16), vc_ov, preferred_element_type=F32) * inv
        imp = imp + both[:, LANES:]
        o_cmp.append(jnp.where(low, both[:QB, :LANES], both[QB:, :LANES]))

    off_prev = jnp.where(i >= 1, 0.0, -NEG)
    off_edge = jnp.where(i >= 2, 0.0, -NEG)
    ql_t = lax.broadcasted_iota(jnp.int32, (sq, TK), 0) % QB
    kl_t = lax.broadcasted_iota(jnp.int32, (sq, TK), 1)
    edge = jnp.where(kl_t > ql_t, 0.0, NEG)
    wv_all = jnp.concatenate([
        jnp.concatenate([wv_ref[pl.ds(st, TK), :], ones_col], axis=1) for st in (start_d, start_p, start_e)], axis=0)
    o_win = []
    for c in range(ncol):
        s0 = s_win[c][0] + stacked(d0_ref, c)
        s1 = s_win[c][1] + stacked(d1_ref, c)
        s2 = s_win[c][2] + edge
        mx = jnp.maximum(jnp.maximum(jnp.max(s0, axis=-1, keepdims=True),
                                     jnp.max(s1, axis=-1, keepdims=True) - off_prev),
                         jnp.max(s2, axis=-1, keepdims=True) - off_edge)
        p = jnp.exp(jnp.concatenate([s0 - mx, s1 - (mx + off_prev), s2 - (mx + off_edge)], axis=1)).astype(BF16)
        ow = jnp.dot(p, wv_all, preferred_element_type=F32)
        o_win.append(jnp.where(low, ow[:QB, :LANES] / ow[:QB, LANES:], ow[QB:, :LANES] / ow[QB:, LANES:]))

    imp_t = imp.T
    blk = lax.broadcasted_iota(jnp.int32, (LANES, sq), 0)
    cur = (t0 + lax.broadcasted_iota(jnp.int32, (LANES, sq), 1) % QB) // SEL_BLOCK
    forced = (blk == 0) | (blk == cur) | (blk == cur - 1)
    imp_t = jnp.where(forced, -jnp.inf, jnp.where(blk > cur, -BIG, imp_t))
    blk_f = blk.astype(F32)

    val = imp_t
    for _ in range(N_SEL - 3):
        top = jnp.max(val, axis=0, keepdims=True)
        idx = jnp.min(jnp.where(val == top, blk_f, 1e6), axis=0, keepdims=True)
        val = jnp.where(blk_f == idx, -jnp.inf, val)
    msel = jnp.where(val == -jnp.inf, 0.0, NEG).T.astype(BF16)

    qaug = [jnp.concatenate([qs[c], msel], axis=1) for c in range(ncol)]

    def sel_scores(c, start, width):
        kaug = jnp.concatenate([sk_ref[pl.ds(start, width), :], oh_ref[pl.ds(start, width), :]], axis=1)
        return _dot_nt(qaug[c], kaug)

    def sel_vaug(start, width):
        return jnp.concatenate([sv_ref[pl.ds(start, width), :], jnp.ones((width, LANES), BF16)], axis=1)

    def lanes(x, width):
        return jnp.concatenate([x] * (width // LANES), axis=1)

    s_near = [(sel_scores(c, start_d, TK), sel_scores(c, start_p, TK)) for c in range(ncol)]
    for c in range(ncol):
        s0 = s_near[c][0] + stacked(d0_ref, c)
        s1 = s_near[c][1] + stacked(d1_ref, c)
        mx = jnp.maximum(jnp.max(s0, axis=-1, keepdims=True), jnp.max(s1, axis=-1, keepdims=True) - off_prev)
        p0 = jnp.exp(s0 - mx).astype(BF16)
        p1 = jnp.exp(s1 - (mx + off_prev)).astype(BF16)
        acc_ref[c] = (jnp.dot(p0, sel_vaug(start_d, TK), preferred_element_type=F32)
                      + jnp.dot(p1, sel_vaug(start_p, TK), preferred_element_type=F32))
        m_ref[c] = jnp.broadcast_to(mx, (sq, LANES))

    def far_tile(start, width):
        vaug = sel_vaug(start, width)
        scores = [sel_scores(c, start, width) for c in range(ncol)]
        for c in range(ncol):
            s = scores[c]
            m_old = m_ref[c]
            m_new = jnp.maximum(m_old, jnp.max(s, axis=-1, keepdims=True))
            alpha = jnp.exp(m_old - m_new)
            p = jnp.exp(s - lanes(m_new, width)).astype(BF16)
            acc_ref[c] = lanes(alpha, 2 * LANES) * acc_ref[c] + jnp.dot(p, vaug, preferred_element_type=F32)
            m_ref[c] = m_new

    n_far = jnp.maximum(i - 1, 0)
    n_wide = n_far // FAR_TILES

    def far_wide(kt, carry):
        far_tile(pl.multiple_of(kt * (FAR_TILES * TK), FAR_TILES * TK), FAR_TILES * TK)
        return carry

    def far_rest(kt, carry):
        far_tile(pl.multiple_of(kt * TK, TK), TK)
        return carry

    lax.fori_loop(0, n_wide, far_wide, 0)
    lax.fori_loop(n_wide * FAR_TILES, n_far, far_rest, 0)

    gate = jax.nn.sigmoid(gt_ref[...].astype(F32)).astype(BF16)
    gexp = jnp.dot(gate, e_ref[...], preferred_element_type=F32)
    for c in range(ncol):
        acc = acc_ref[c]
        o_sel = jnp.where(low, acc[:QB, :LANES] / acc[:QB, LANES:], acc[QB:, :LANES] / acc[QB:, LANES:])
        gc = lambda br: gexp[:, (br * ncol + c) * LANES:(br * ncol + c + 1) * LANES]
        o_ref[:, c * LANES:(c + 1) * LANES] = (gc(0) * o_cmp[c] + gc(1) * o_sel + gc(2) * o_win[c]).astype(BF16)


def _nsa(proj3, kcmp, vcmp, onehot, overlap, band, d0, d1, tc, expand):
    b, seq, _ = proj3.shape
    ncmp = kcmp.shape[1]
    col = lambda width, cidx: pl.BlockSpec((None, QB, width), lambda bi, qi: (bi, qi, cidx))
    res = lambda cidx: pl.BlockSpec((None, seq, LANES), lambda bi, qi: (bi, 0, cidx))
    full = lambda arr: pl.BlockSpec(arr.shape, lambda bi, qi: (0,) * arr.ndim)
    cmp_spec = pl.BlockSpec((None, ncmp, LANES), lambda bi, qi: (bi, 0, 0))
    return pl.pallas_call(
        functools.partial(_nsa_kernel, ncmp=ncmp),
        grid=(b, seq // QB),
        in_specs=[col(NSA_W, COL_NQ * LANES // NSA_W), col(LANES, COL_GATE),
                  res(COL_SELK), res(COL_SELV), res(COL_WINK), res(COL_WINV),
                  cmp_spec, cmp_spec, full(onehot), full(overlap), full(band), full(d0), full(d1), full(tc),
                  full(expand)],
        out_specs=pl.BlockSpec((None, QB, NSA_W), lambda bi, qi: (bi, qi, 0)),
        out_shape=jax.ShapeDtypeStruct((b, seq, NSA_W), BF16),
        scratch_shapes=[pltpu.VMEM((NSA_W // LANES, 2 * QB, 2 * LANES), F32),
                        pltpu.VMEM((NSA_W // LANES, 2 * QB, LANES), F32)],
        compiler_params=_cparams(("arbitrary", "arbitrary")),
        name="nsa_attention",
    )(proj3, proj3, proj3, proj3, proj3, proj3, kcmp, vcmp, onehot, overlap, band, d0, d1, tc, expand)


def _gmlp_kernel(u_ref, v_ref, ws_ref, b_ref, o_ref):
    lane = lax.broadcasted_iota(jnp.int32, (GM_CHUNK, LANES), 1)
    low = lane < GM_DIM
    row = lax.broadcasted_iota(jnp.int32, (GM_CHUNK, GM_CHUNK), 0)
    colm = lax.broadcasted_iota(jnp.int32, (GM_CHUNK, GM_CHUNK), 1)
    tril = colm <= row
    ones_bd = _head_ones()
    for cc in range(GM_W // LANES):
        sl = slice(cc * LANES, (cc + 1) * LANES)
        w = jnp.concatenate([jnp.where(tril, ws_ref[2 * cc], 0.0), jnp.where(tril, ws_ref[2 * cc + 1], 0.0)],
                            axis=0).astype(BF16)
        for ch in range(GM_STEP // GM_CHUNK):
            rows = slice(ch * GM_CHUNK, (ch + 1) * GM_CHUNK)
            u = jax.nn.gelu(u_ref[rows, sl].astype(F32), approximate=True)
            v = jax.nn.gelu(v_ref[rows, sl].astype(F32), approximate=True)
            vn = _head_rmsnorm(v, ones_bd).astype(BF16)
            r = jnp.dot(w, vn, preferred_element_type=F32)
            sv = jnp.where(low, r[:GM_CHUNK], r[GM_CHUNK:]) + b_ref[:, sl]
            o_ref[rows, sl] = (u * sv).astype(BF16)


def _gmlp(proj3, ws, bexp):
    b, seq, _ = proj3.shape
    return pl.pallas_call(
        _gmlp_kernel,
        grid=(b, seq // GM_STEP),
        in_specs=[pl.BlockSpec((None, GM_STEP, GM_W), lambda bi, ti: (bi, ti, COL_GMU * LANES // GM_W)),
                  pl.BlockSpec((None, GM_STEP, GM_W), lambda bi, ti: (bi, ti, COL_GMV * LANES // GM_W)),
                  pl.BlockSpec((N_GM, GM_CHUNK, GM_CHUNK), lambda bi, ti: (0, 0, 0)),
                  pl.BlockSpec((GM_CHUNK, GM_W), lambda bi, ti: (0, 0))],
        out_specs=pl.BlockSpec((None, GM_STEP, GM_W), lambda bi, ti: (bi, ti, 0)),
        out_shape=jax.ShapeDtypeStruct((b, seq, GM_W), BF16),
        compiler_params=_cparams(("arbitrary", "arbitrary")),
        name="gmlp",
    )(proj3, proj3, ws, bexp)


FFN_CHUNK = 256


def _ffn_kernel(x_ref, r_ref, n_ref, m_ref, wo_ref, g_ref, wg_ref, wu_ref, wd_ref, o_ref):
    mix = jnp.dot(r_ref[...], wo_ref[0:RET_W, :], preferred_element_type=F32)
    mix = mix + jnp.dot(n_ref[...], wo_ref[RET_W:RET_W + NSA_W, :], preferred_element_type=F32)
    mix = mix + jnp.dot(m_ref[...], wo_ref[RET_W + NSA_W:, :], preferred_element_type=F32)
    x = x_ref[...] + mix
    ms = jnp.mean(x * x, axis=-1, keepdims=True)
    h = (x * lax.rsqrt(ms + EPS) * g_ref[...]).astype(BF16)
    acc = x
    for f in range(0, D_FF, FFN_CHUNK):
        gate = jnp.dot(h, wg_ref[:, f:f + FFN_CHUNK], preferred_element_type=F32)
        up = jnp.dot(h, wu_ref[:, f:f + FFN_CHUNK], preferred_element_type=F32)
        act = (gate * jax.nn.sigmoid(gate) * up).astype(BF16)
        acc = acc + jnp.dot(act, wd_ref[f:f + FFN_CHUNK, :], preferred_element_type=F32)
    o_ref[...] = acc


def _outproj_ffn(x2d, ret_o, nsa_o, gm_o, w_out, gain, wg, wu, wd):
    m = x2d.shape[0]
    tm = 512
    row = lambda width: pl.BlockSpec((tm, width), lambda i: (i, 0))
    const = lambda arr: pl.BlockSpec(arr.shape, lambda i: (0, 0), pipeline_mode=pl.Buffered(1))
    return pl.pallas_call(
        _ffn_kernel,
        grid=(m // tm,),
        in_specs=[row(D_MODEL), row(RET_W), row(NSA_W), row(GM_W), const(w_out),
                  pl.BlockSpec((1, D_MODEL), lambda i: (0, 0)),
                  const(wg), const(wu), const(wd)],
        out_specs=row(D_MODEL),
        out_shape=jax.ShapeDtypeStruct((m, D_MODEL), F32),
        compiler_params=_cparams(("arbitrary",)),
        name="outproj_ffn",
    )(x2d, ret_o, nsa_o, gm_o, w_out, gain, wg, wu, wd)


def _nq_head_order():
    return [c + (N_NSA // N_KV) * half for c in range(NSA_W // LANES) for half in range(2)]


def _w_in_layout(w):
    wt = w.T
    off_nq = 4 * RET_W
    off_kv = off_nq + NSA_W
    off_gate = off_kv + 3 * 2 * N_KV * HEAD_DIM
    off_gm = off_gate + 3 * N_NSA
    pad = jnp.zeros((LANES - 3 * N_NSA, D_MODEL), w.dtype)
    heads = [wt[off_nq + h * HEAD_DIM:off_nq + (h + 1) * HEAD_DIM] for h in _nq_head_order()]
    return jnp.concatenate([wt[:off_nq]] + heads + [wt[off_kv:off_gm], pad, wt[off_gm:]], axis=0).astype(BF16)


def _w_out_layout(w):
    heads = [w[RET_W + h * HEAD_DIM:RET_W + (h + 1) * HEAD_DIM] for h in _nq_head_order()]
    return jnp.concatenate([w[:RET_W]] + heads + [w[RET_W + NSA_W:]], axis=0).astype(BF16)


def _gate_expand():
    e = np.zeros((LANES, 3 * NSA_W), np.float32)
    ncol = NSA_W // LANES
    for br in range(3):
        for c in range(ncol):
            for half in range(2):
                head = c + (N_NSA // N_KV) * half
                dst = (br * ncol + c) * LANES + half * HEAD_DIM
                e[head * 3 + br, dst:dst + HEAD_DIM] = 1.0
    return jnp.asarray(e, BF16)


def _rope_tables(seq):
    half = HEAD_DIM // 2
    inv = ROPE_THETA ** (-jnp.arange(half, dtype=F32) / half)
    ang = jnp.arange(seq, dtype=jnp.int32).astype(F32)[:, None] * inv[None, :]
    cos = jnp.tile(jnp.cos(ang), (1, LANES // half))
    sin = jnp.tile(jnp.concatenate([-jnp.sin(ang), jnp.sin(ang)], axis=1), (1, LANES // HEAD_DIM))
    return cos, sin


def _retention_tables():
    c = RET_CHUNK
    log_gamma = jnp.log(1.0 - 2.0 ** (-5.0 - jnp.arange(N_RET, dtype=F32)))
    idx = jnp.arange(c, dtype=F32)
    diff = idx[:, None] - idx[None, :]
    decay = jnp.where(diff >= 0, jnp.exp(jnp.maximum(diff, 0.0)[None] * log_gamma[:, None, None]), 0.0)
    zeta = jnp.exp((c - 1 - idx)[None, :] * log_gamma[:, None])
    xi = jnp.exp((idx + 1)[None, :] * log_gamma[:, None])
    chunk_decay = jnp.exp(c * log_gamma)
    ncol = RET_W // LANES
    dec = decay.reshape(ncol, 2 * c, c)
    xi_st = jnp.broadcast_to(xi.reshape(ncol, 2 * c, 1), (ncol, 2 * c, LANES))
    zeta_l = jnp.repeat(zeta.reshape(ncol, 2, c).transpose(0, 2, 1), HEAD_DIM, axis=2)
    cd = jnp.broadcast_to(jnp.repeat(chunk_decay.reshape(ncol, 2), HEAD_DIM, axis=1)[:, :, None],
                          (ncol, LANES, LANES))
    return dec, xi_st, zeta_l, cd


def _overlap_table(seq):
    n_cmp = (seq - CMP_LEN) // CMP_STRIDE + 1
    n_slc = seq // SEL_BLOCK
    cs = np.arange(n_cmp)[:, None] * CMP_STRIDE
    ss = np.arange(n_slc)[None, :] * SEL_BLOCK
    ov = np.clip(np.minimum(cs + CMP_LEN, ss + SEL_BLOCK) - np.maximum(cs, ss), 0, None) // CMP_STRIDE
    full = np.zeros((seq // CMP_STRIDE, LANES), np.float32)
    full[:n_cmp, :n_slc] = ov
    return jnp.asarray(full, BF16)


def _cmp_band(seq):
    nq = seq // QB
    ncmp = seq // CMP_STRIDE
    x = np.arange(2 * ncmp)[:, None]
    m = np.arange(LANES)[None, :]
    base = (QB // CMP_STRIDE) * nq - 16
    band = np.where(m < LANES - 1, x == m + base, x >= m + base)
    return jnp.asarray(band.astype(np.float32), BF16)


def _block_onehot(seq):
    oh = (np.arange(seq)[:, None] // SEL_BLOCK) == np.arange(LANES)[None, :]
    return jnp.asarray(oh.astype(np.float32), BF16)


def _compress_weights(cmp_pe, cmp_w1, cmp_w2):
    def block_diag2(w):
        zero = jnp.zeros_like(w)
        return jnp.concatenate([jnp.concatenate([w, zero], axis=-1), jnp.concatenate([zero, w], axis=-1)],
                               axis=-2).astype(BF16)

    w1bd = block_diag2(cmp_w1.reshape(2, CMP_LEN, HEAD_DIM, HEAD_DIM))
    w2bd = block_diag2(cmp_w2)
    pe2 = jnp.tile(cmp_pe, (1, 1, N_KV))[:, :, None, :]
    return w1bd[:, :CMP_LEN // 2], w1bd[:, CMP_LEN // 2:], pe2, w2bd


def kernel(x, attn_norm, w_in, w_out, nsa_q_gain, nsa_k_gain, cmp_pe, cmp_w1, cmp_w2, gm_ws, gm_b, ffn_norm,
           w_gate_up, w_down, rel_bias):
    b, seq, _ = x.shape
    assert seq % (2 * TK) == 0 and N_SEL <= seq // SEL_BLOCK <= LANES, "selection blocks must fit one lane row"
    depth = w_in.shape[0]
    cos_tab, sin_tab = _rope_tables(seq)
    ret_tabs = _retention_tables()
    overlap = _overlap_table(seq)
    onehot = _block_onehot(seq)
    band = _cmp_band(seq)
    expand = _gate_expand()
    d0, d1, tc = _bias_tiles(rel_bias)
    ngrp = seq // CMP_STRIDE
    x2d = x.reshape(b * seq, D_MODEL)
    for l in range(depth):
        zero_row = jnp.zeros((5, LANES), F32)
        head_gains = jnp.concatenate([jnp.tile(nsa_q_gain[l] * HEAD_DIM ** -0.5, 2)[None],
                                      jnp.tile(nsa_k_gain[l, 1], 2)[None],
                                      jnp.tile(nsa_k_gain[l, 2], 2)[None], zero_row], axis=0)
        proj, cmp_grp = _inproj(x2d, attn_norm[l][None], _w_in_layout(w_in[l]), cos_tab, sin_tab, head_gains, seq)
        proj3 = proj.reshape(b, seq, PROJ_W)
        ret_o = _retention(proj3, ret_tabs)
        cmp_in = cmp_grp.reshape(b, ngrp, CMP_STRIDE * 2 * LANES)
        wtop, wbot, pe2, w2bd = _compress_weights(cmp_pe[l], cmp_w1[l], cmp_w2[l])
        kgain = jnp.broadcast_to(jnp.tile(nsa_k_gain[l, 0], 2)[None], (8, LANES))
        kcmp, vcmp = _compress(cmp_in, wtop, wbot, pe2, w2bd, kgain)
        nsa_o = _nsa(proj3, kcmp, vcmp, onehot, overlap, band, d0, d1, tc, expand)
        gm_o = _gmlp(proj3, gm_ws[l], jnp.repeat(gm_b[l].T, GM_DIM, axis=1))
        x2d = _outproj_ffn(x2d, ret_o.reshape(b * seq, RET_W), nsa_o.reshape(b * seq, NSA_W),
                           gm_o.reshape(b * seq, GM_W), _w_out_layout(w_out[l]), ffn_norm[l][None],
                           w_gate_up[l, :, :D_FF].astype(BF16), w_gate_up[l, :, D_FF:].astype(BF16),
                           w_down[l].astype(BF16))
    return x2d.reshape(b, seq, D_MODEL)
```

```python
import functools
import math

import numpy as np
import jax
import jax.numpy as jnp
from jax import lax
from jax.experimental import pallas as pl
from jax.experimental.pallas import tpu as pltpu

F32 = jnp.float32
BF16 = jnp.bfloat16

D_MODEL = 1024
HEAD_DIM = 64
N_RET = 6
N_NSA = 6
N_KV = 2
N_GM = 4
GM_DIM = 64
RET_W = N_RET * HEAD_DIM
NSA_W = N_NSA * HEAD_DIM
GM_W = N_GM * GM_DIM
RET_CHUNK = 128
CMP_LEN = 32
CMP_STRIDE = 16
SEL_BLOCK = 64
N_SEL = 16
WINDOW = 512
GM_CHUNK = 128
N_BUCKETS = 32
MAX_DISTANCE = 128
ROPE_THETA = 10000.0
D_FF = 2816
EPS = 1e-6
BIG = 1e9
NEG = -1e30

LANES = 128
PROJ_W = 3328
COL_RQ, COL_RK, COL_RV, COL_RG = 0, 3, 6, 9
COL_NQ = 12
COL_CMPK, COL_CMPV, COL_SELK, COL_SELV, COL_WINK, COL_WINV = 15, 16, 17, 18, 19, 20
COL_GATE = 21
COL_GMU, COL_GMV = 22, 24
QB = 256
TK = 256
FAR_TILES = 4
INPROJ_SPLIT = 2
RET_STEP = 512
GM_STEP = 512
VMEM_LIMIT = 56 * 1024 * 1024


def _bucket_thresholds():
    n = np.arange(0, 4 * MAX_DISTANCE)
    max_exact = N_BUCKETS // 2
    nf = np.maximum(n, 1).astype(np.float64)
    large = max_exact + np.floor(np.log(nf / max_exact) / math.log(MAX_DISTANCE / max_exact)
                                 * (N_BUCKETS - max_exact)).astype(np.int64)
    bucket = np.where(n < max_exact, n, np.minimum(large, N_BUCKETS - 1))
    return [int(np.min(n[bucket >= b])) for b in range(N_BUCKETS)]


BUCKET_THR = _bucket_thresholds()


def _cparams(sem):
    return pltpu.CompilerParams(dimension_semantics=sem, vmem_limit_bytes=VMEM_LIMIT)


def _head_ones():
    r = lax.broadcasted_iota(jnp.int32, (LANES, LANES), 0) // HEAD_DIM
    c = lax.broadcasted_iota(jnp.int32, (LANES, LANES), 1) // HEAD_DIM
    return jnp.where(r == c, 1.0, 0.0).astype(BF16)


def _head_sum(x, ones_bd):
    return jnp.dot(x.astype(BF16), ones_bd, preferred_element_type=F32)


def _head_rmsnorm(x, ones_bd):
    return x * lax.rsqrt(_head_sum(x * x, ones_bd) * (1.0 / HEAD_DIM) + EPS)


def _dot_nt(a, b):
    return lax.dot_general(a, b, (((1,), (1,)), ((), ())), preferred_element_type=F32)


def _dot_tn(a, b):
    return lax.dot_general(a, b, (((0,), (0,)), ((), ())), preferred_element_type=F32)


def _stack_heads(q, low):
    zero = jnp.zeros_like(q)
    return jnp.concatenate([jnp.where(low, q, zero), jnp.where(low, zero, q)], axis=0)


def _inproj_kernel(x_ref, g_ref, wa_ref, wb_ref, cos_ref, sin_ref, hg_ref, o_ref, grp_ref, cmp_scr, *, tm):
    sub_rows = tm // INPROJ_SPLIT
    lane = lax.broadcasted_iota(jnp.int32, (sub_rows, LANES), 1)
    first_half = (lane % HEAD_DIM) < (HEAD_DIM // 2)
    ones_bd = _head_ones()
    norm_gain = {COL_NQ: 0, COL_NQ + 1: 0, COL_NQ + 2: 0, COL_SELK: 1, COL_WINK: 2}
    deferred = []
    for r0 in range(0, tm, sub_rows):
        rows = slice(r0, r0 + sub_rows)
        x = x_ref[rows, :]
        ms = jnp.mean(x * x, axis=-1, keepdims=True)
        h = (x * lax.rsqrt(ms + EPS) * g_ref[...]).astype(BF16)
        cos = cos_ref[rows, :]
        sin = sin_ref[rows, :]
        for s in range(0, PROJ_W // LANES, 2):
            if s < COL_GMU:
                acc = _dot_nt(h, wa_ref[s * LANES:(s + 2) * LANES, :])
            else:
                acc = _dot_nt(h, wb_ref[(s - COL_GMU) * LANES:(s - COL_GMU + 2) * LANES, :])
            for sub in range(2):
                j = s + sub
                a = acc[:, sub * LANES:(sub + 1) * LANES]
                if j < COL_RV:
                    swapped = jnp.where(first_half, pltpu.roll(a, LANES - HEAD_DIM // 2, axis=1),
                                        pltpu.roll(a, HEAD_DIM // 2, axis=1))
                    a = a * cos + swapped * sin
                    if j >= COL_RK:
                        a = a * (HEAD_DIM ** -0.5)
                elif j in norm_gain:
                    deferred.append((rows, j, a))
                    continue
                elif j in (COL_CMPK, COL_CMPV):
                    cmp_scr[j - COL_CMPK, rows, :] = a
                o_ref[rows, j * LANES:(j + 1) * LANES] = a.astype(BF16)
    for rows, j, a in deferred:
        a = _head_rmsnorm(a, ones_bd) * hg_ref[norm_gain[j]:norm_gain[j] + 1, :]
        o_ref[rows, j * LANES:(j + 1) * LANES] = a.astype(BF16)
    for l in range(CMP_STRIDE):
        for kv in range(2):
            grp_ref[:, (2 * l + kv) * LANES:(2 * l + kv + 1) * LANES] = (
                cmp_scr[kv, pl.ds(l, tm // CMP_STRIDE, stride=CMP_STRIDE), :].astype(BF16))


def _inproj(x2d, gain, w_main, w_gm, cos_tab, sin_tab, head_gains, seq):
    m = x2d.shape[0]
    tm = 1024
    nt = seq // tm
    const = lambda arr: pl.BlockSpec(arr.shape, lambda i: (0, 0), pipeline_mode=pl.Buffered(1))
    return pl.pallas_call(
        functools.partial(_inproj_kernel, tm=tm),
        grid=(m // tm,),
        in_specs=[
            pl.BlockSpec((tm, D_MODEL), lambda i: (i, 0)),
            pl.BlockSpec((1, D_MODEL), lambda i: (0, 0)),
            const(w_main), const(w_gm),
            pl.BlockSpec((tm, LANES), lambda i: (i % nt, 0)),
            pl.BlockSpec((tm, LANES), lambda i: (i % nt, 0)),
            pl.BlockSpec((8, LANES), lambda i: (0, 0)),
        ],
        out_specs=[pl.BlockSpec((tm, PROJ_W), lambda i: (i, 0)),
                   pl.BlockSpec((tm // CMP_STRIDE, CMP_STRIDE * 2 * LANES), lambda i: (i, 0))],
        out_shape=[jax.ShapeDtypeStruct((m, PROJ_W), BF16),
                   jax.ShapeDtypeStruct((m // CMP_STRIDE, CMP_STRIDE * 2 * LANES), BF16)],
        scratch_shapes=[pltpu.VMEM((2, tm, LANES), F32)],
        compiler_params=_cparams(("arbitrary",)),
        name="inproj",
    )(x2d, gain, w_main, w_gm, cos_tab, sin_tab, head_gains)


def _ret_kernel(q_ref, k_ref, v_ref, g_ref, dec_ref, xi_ref, zeta_ref, cd_ref, o_ref, r_ref):
    @pl.when(pl.program_id(1) == 0)
    def _():
        r_ref[...] = jnp.zeros_like(r_ref)

    lane = lax.broadcasted_iota(jnp.int32, (RET_CHUNK, LANES), 1)
    low = lane < HEAD_DIM
    ones_bd = _head_ones()
    for c in range(RET_W // LANES):
        sl = slice(c * LANES, (c + 1) * LANES)
        state = r_ref[c]
        for ch in range(RET_STEP // RET_CHUNK):
            rows = slice(ch * RET_CHUNK, (ch + 1) * RET_CHUNK)
            q = q_ref[rows, sl]
            k = k_ref[rows, sl]
            v = v_ref[rows, sl]
            g = g_ref[rows, sl].astype(F32)
            qs = _stack_heads(q, low)
            scores = _dot_nt(qs, k) * dec_ref[c]
            inner = jnp.dot(scores.astype(BF16), v, preferred_element_type=F32)
            cross = jnp.dot(qs, state.astype(BF16), preferred_element_type=F32) * xi_ref[c]
            tot = inner + cross
            o = jnp.where(low, tot[:RET_CHUNK], tot[RET_CHUNK:])
            kz = (k.astype(F32) * zeta_ref[c]).astype(BF16)
            state = cd_ref[c] * state + _dot_tn(kz, v)
            y = _head_rmsnorm(o, ones_bd) * (g * jax.nn.sigmoid(g))
            o_ref[rows, sl] = y.astype(BF16)
        r_ref[c] = state


def _retention(proj3, tabs):
    b, seq, _ = proj3.shape
    dec, xi, zeta, cd = tabs
    ncol = RET_W // LANES
    qkvg = [pl.BlockSpec((None, RET_STEP, RET_W), functools.partial(lambda bi, ti, col: (bi, ti, col), col=col))
            for col in range(4)]
    const3 = lambda shape: pl.BlockSpec(shape, lambda bi, ti: (0, 0, 0))
    return pl.pallas_call(
        _ret_kernel,
        grid=(b, seq // RET_STEP),
        in_specs=qkvg + [const3((ncol, 2 * RET_CHUNK, LANES)), const3((ncol, 2 * RET_CHUNK, LANES)),
                         const3((ncol, RET_CHUNK, LANES)), const3((ncol, LANES, LANES))],
        out_specs=pl.BlockSpec((None, RET_STEP, RET_W), lambda bi, ti: (bi, ti, 0)),
        out_shape=jax.ShapeDtypeStruct((b, seq, RET_W), BF16),
        scratch_shapes=[pltpu.VMEM((ncol, LANES, LANES), F32)],
        compiler_params=_cparams(("arbitrary", "arbitrary")),
        name="retention",
    )(proj3, proj3, proj3, proj3, dec, xi, zeta, cd)


def _compress_kernel(a_ref, wtop_ref, wbot_ref, pe_ref, w2_ref, kg_ref, kc_ref, vc_ref, *, ngrp):
    half = CMP_LEN // 2
    ones_bd = _head_ones()
    for kv in range(2):
        top = jnp.zeros((ngrp, LANES), F32)
        bot = jnp.zeros((ngrp, LANES), F32)
        for l in range(half):
            x = a_ref[:, l * 2 * LANES + kv * LANES:l * 2 * LANES + (kv + 1) * LANES].astype(F32)
            top = top + jnp.dot((x + pe_ref[kv, l]).astype(BF16), wtop_ref[kv, l], preferred_element_type=F32)
            bot = bot + jnp.dot((x + pe_ref[kv, half + l]).astype(BF16), wbot_ref[kv, l],
                                preferred_element_type=F32)
        hid = jax.nn.gelu(top + pltpu.roll(bot, ngrp - 1, axis=0), approximate=True)
        out = jnp.dot(hid.astype(BF16), w2_ref[kv], preferred_element_type=F32)
        if kv == 0:
            kc_ref[...] = (_head_rmsnorm(out, ones_bd) * kg_ref[0:1, :]).astype(BF16)
        else:
            vc_ref[...] = out.astype(BF16)


def _compress(a, wtop, wbot, pe2, w2bd, kgain):
    b, ngrp, width = a.shape
    full = lambda arr: pl.BlockSpec(arr.shape, lambda bi: (0,) * arr.ndim)
    out_spec = pl.BlockSpec((None, ngrp, LANES), lambda bi: (bi, 0, 0))
    return pl.pallas_call(
        functools.partial(_compress_kernel, ngrp=ngrp),
        grid=(b,),
        in_specs=[pl.BlockSpec((None, ngrp, width), lambda bi: (bi, 0, 0)),
                  full(wtop), full(wbot), full(pe2), full(w2bd), full(kgain)],
        out_specs=[out_spec, out_spec],
        out_shape=[jax.ShapeDtypeStruct((b, ngrp, LANES), BF16)] * 2,
        compiler_params=_cparams(("arbitrary",)),
        name="nsa_compress",
    )(a, wtop, wbot, pe2, w2bd, kgain)


def _bias_kernel(rb_ref, d0_ref, d1_ref, tc_ref):
    h = pl.program_id(0)
    far = rb_ref[N_BUCKETS - 1, h]

    def rel(n):
        val = jnp.full(n.shape, rb_ref[0, h] - far, F32)
        for bkt in range(1, N_BUCKETS):
            val = jnp.where(n >= BUCKET_THR[bkt], rb_ref[bkt, h] - far, val)
        return val

    ql = lax.broadcasted_iota(jnp.int32, (QB, TK), 0)
    kl = lax.broadcasted_iota(jnp.int32, (QB, TK), 1)
    d = ql - kl
    d0_ref[...] = jnp.where(d >= 0, rel(d), NEG)
    d1_ref[...] = rel(d + TK)
    ql2 = lax.broadcasted_iota(jnp.int32, (QB, LANES), 0)
    m = lax.broadcasted_iota(jnp.int32, (QB, LANES), 1)
    dc = ql2 - CMP_STRIDE * (m - 16) - (CMP_LEN - 1)
    tc_ref[...] = jnp.where(dc >= 0, rel(dc), NEG).astype(BF16)


def _bias_tiles(rel_bias):
    return pl.pallas_call(
        _bias_kernel,
        grid=(N_NSA,),
        in_specs=[pl.BlockSpec(memory_space=pltpu.SMEM)],
        out_specs=[pl.BlockSpec((None, QB, TK), lambda h: (h, 0, 0)),
                   pl.BlockSpec((None, QB, TK), lambda h: (h, 0, 0)),
                   pl.BlockSpec((None, QB, LANES), lambda h: (h, 0, 0))],
        out_shape=[jax.ShapeDtypeStruct((N_NSA, QB, TK), F32),
                   jax.ShapeDtypeStruct((N_NSA, QB, TK), F32),
                   jax.ShapeDtypeStruct((N_NSA, QB, LANES), BF16)],
        compiler_params=_cparams(("arbitrary",)),
        name="t5_bias_tiles",
    )(rel_bias)


def _nsa_kernel(q_ref, gt_ref, sk_ref, sv_ref, wk_ref, wv_ref, kc_ref, vc_ref, oh_ref, ov_ref, band_ref,
                d0_ref, d1_ref, tc_ref, e_ref, o_ref, acc_ref, m_ref, *, ncmp):
    i = pl.program_id(1)
    t0 = i * QB
    ncol = NSA_W // LANES
    sq = 2 * QB
    low = lax.broadcasted_iota(jnp.int32, (QB, LANES), 1) < HEAD_DIM
    qs = [_stack_heads(q_ref[:, c * LANES:(c + 1) * LANES], low) for c in range(ncol)]

    def stacked(ref, c):
        return jnp.concatenate([ref[c], ref[c + ncol]], axis=0)

    ones_col = jnp.ones((TK, LANES), BF16)

    band = band_ref[pl.ds(pl.multiple_of((QB // CMP_STRIDE) * (pl.num_programs(1) - i), CMP_STRIDE), ncmp), :]
    kc_aug = jnp.concatenate([kc_ref[...], band], axis=1)
    vc_ov = jnp.concatenate([vc_ref[...], ov_ref[...]], axis=1)
    imp = jnp.zeros((sq, LANES), F32)
    o_cmp = []
    s_cmp = [_dot_nt(jnp.concatenate([qs[c], stacked(tc_ref, c)], axis=1), kc_aug) for c in range(ncol)]

    start_d = pl.multiple_of(t0, TK)
    start_p = pl.multiple_of(jnp.maximum(t0 - TK, 0), TK)
    start_e = pl.multiple_of(jnp.maximum(t0 - 2 * TK, 0), TK)
    s_win = [[_dot_nt(qs[c], wk_ref[pl.ds(st, TK), :]) for st in (start_d, start_p, start_e)] for c in range(ncol)]

    for c in range(ncol):
        s = s_cmp[c]
        mx = jnp.maximum(jnp.max(s, axis=-1, keepdims=True), -1e20)
        e = jnp.exp(s - mx)
        den = jnp.sum(e, axis=-1, keepdims=True)
        inv = jnp.where(den > 0.0, 1.0 / den, 0.0)
        both = jnp.dot(e.astype(BF16), vc_ov, preferred_element_type=F32) * inv
        imp = imp + both[:, LANES:]
        o_cmp.append(jnp.where(low, both[:QB, :LANES], both[QB:, :LANES]))

    off_prev = jnp.where(i >= 1, 0.0, -NEG)
    off_edge = jnp.where(i >= 2, 0.0, -NEG)
    ql_t = lax.broadcasted_iota(jnp.int32, (sq, TK), 0) % QB
    kl_t = lax.broadcasted_iota(jnp.int32, (sq, TK), 1)
    edge = jnp.where(kl_t > ql_t, 0.0, NEG)
    wv_all = jnp.concatenate([
        jnp.concatenate([wv_ref[pl.ds(st, TK), :], ones_col], axis=1) for st in (start_d, start_p, start_e)], axis=0)
    o_win = []
    for c in range(ncol):
        s0 = s_win[c][0] + stacked(d0_ref, c)
        s1 = s_win[c][1] + stacked(d1_ref, c)
        s2 = s_win[c][2] + edge
        mx = jnp.maximum(jnp.maximum(jnp.max(s0, axis=-1, keepdims=True),
                                     jnp.max(s1, axis=-1, keepdims=True) - off_prev),
                         jnp.max(s2, axis=-1, keepdims=True) - off_edge)
        p = jnp.exp(jnp.concatenate([s0 - mx, s1 - (mx + off_prev), s2 - (mx + off_edge)], axis=1)).astype(BF16)
        ow = jnp.dot(p, wv_all, preferred_element_type=F32)
        o_win.append(jnp.where(low, ow[:QB, :LANES] / ow[:QB, LANES:], ow[QB:, :LANES] / ow[QB:, LANES:]))

    imp_t = imp.T
    blk = lax.broadcasted_iota(jnp.int32, (LANES, sq), 0)
    cur = (t0 + lax.broadcasted_iota(jnp.int32, (LANES, sq), 1) % QB) // SEL_BLOCK
    forced = (blk == 0) | (blk == cur) | (blk == cur - 1)
    imp_t = jnp.where(forced, -jnp.inf, jnp.where(blk > cur, -BIG, imp_t))
    blk_f = blk.astype(F32)

    val = imp_t
    for _ in range(N_SEL - 3):
        top = jnp.max(val, axis=0, keepdims=True)
        idx = jnp.min(jnp.where(val == top, blk_f, 1e6), axis=0, keepdims=True)
        val = jnp.where(blk_f == idx, -jnp.inf, val)
    msel = jnp.where(val == -jnp.inf, 0.0, NEG).T.astype(BF16)

    qaug = [jnp.concatenate([qs[c], msel], axis=1) for c in range(ncol)]

    def sel_scores(c, start, width):
        kaug = jnp.concatenate([sk_ref[pl.ds(start, width), :], oh_ref[pl.ds(start, width), :]], axis=1)
        return _dot_nt(qaug[c], kaug)

    def sel_vaug(start, width):
        return jnp.concatenate([sv_ref[pl.ds(start, width), :], jnp.ones((width, LANES), BF16)], axis=1)

    def lanes(x, width):
        return jnp.concatenate([x] * (width // LANES), axis=1)

    s_near = [(sel_scores(c, start_d, TK), sel_scores(c, start_p, TK)) for c in range(ncol)]
    for c in range(ncol):
        s0 = s_near[c][0] + stacked(d0_ref, c)
        s1 = s_near[c][1] + stacked(d1_ref, c)
        mx = jnp.maximum(jnp.max(s0, axis=-1, keepdims=True), jnp.max(s1, axis=-1, keepdims=True) - off_prev)
        p0 = jnp.exp(s0 - mx).astype(BF16)
        p1 = jnp.exp(s1 - (mx + off_prev)).astype(BF16)
        acc_ref[c] = (jnp.dot(p0, sel_vaug(start_d, TK), preferred_element_type=F32)
                      + jnp.dot(p1, sel_vaug(start_p, TK), preferred_element_type=F32))
        m_ref[c] = jnp.broadcast_to(mx, (sq, LANES))

    def far_tile(kt0, tiles):
        start = pl.multiple_of(kt0 * TK, tiles * TK)
        width = tiles * TK
        vaug = sel_vaug(start, width)
        scores = [sel_scores(c, start, width) for c in range(ncol)]
        for c in range(ncol):
            s = scores[c]
            m_old = m_ref[c]
            m_new = jnp.maximum(m_old, jnp.max(s, axis=-1, keepdims=True))
            alpha = jnp.exp(m_old - m_new)
            p = jnp.exp(s - lanes(m_new, width)).astype(BF16)
            acc_ref[c] = lanes(alpha, 2 * LANES) * acc_ref[c] + jnp.dot(p, vaug, preferred_element_type=F32)
            m_ref[c] = m_new

    n_far = jnp.maximum(i - 1, 0)
    n_wide = n_far // FAR_TILES
    n_half = n_far // (FAR_TILES // 2)

    def far_loop(tiles):
        def body(kt, carry):
            far_tile(kt * tiles, tiles)
            return carry
        return body

    lax.fori_loop(0, n_wide, far_loop(FAR_TILES), 0)
    lax.fori_loop(n_wide * 2, n_half, far_loop(FAR_TILES // 2), 0)
    lax.fori_loop(n_half * (FAR_TILES // 2), n_far, far_loop(1), 0)

    gate = jax.nn.sigmoid(gt_ref[...].astype(F32)).astype(BF16)
    gexp = jnp.dot(gate, e_ref[...], preferred_element_type=F32)
    for c in range(ncol):
        acc = acc_ref[c]
        o_sel = jnp.where(low, acc[:QB, :LANES] / acc[:QB, LANES:], acc[QB:, :LANES] / acc[QB:, LANES:])
        gc = lambda br: gexp[:, (br * ncol + c) * LANES:(br * ncol + c + 1) * LANES]
        o_ref[:, c * LANES:(c + 1) * LANES] = (gc(0) * o_cmp[c] + gc(1) * o_sel + gc(2) * o_win[c]).astype(BF16)


def _nsa(proj3, kcmp, vcmp, onehot, overlap, band, d0, d1, tc, expand):
    b, seq, _ = proj3.shape
    ncmp = kcmp.shape[1]
    col = lambda width, cidx: pl.BlockSpec((None, QB, width), lambda bi, qi: (bi, qi, cidx))
    res = lambda cidx: pl.BlockSpec((None, seq, LANES), lambda bi, qi: (bi, 0, cidx))
    full = lambda arr: pl.BlockSpec(arr.shape, lambda bi, qi: (0,) * arr.ndim)
    cmp_spec = pl.BlockSpec((None, ncmp, LANES), lambda bi, qi: (bi, 0, 0))
    return pl.pallas_call(
        functools.partial(_nsa_kernel, ncmp=ncmp),
        grid=(b, seq // QB),
        in_specs=[col(NSA_W, COL_NQ * LANES // NSA_W), col(LANES, COL_GATE),
                  res(COL_SELK), res(COL_SELV), res(COL_WINK), res(COL_WINV),
                  cmp_spec, cmp_spec, full(onehot), full(overlap), full(band), full(d0), full(d1), full(tc),
                  full(expand)],
        out_specs=pl.BlockSpec((None, QB, NSA_W), lambda bi, qi: (bi, qi, 0)),
        out_shape=jax.ShapeDtypeStruct((b, seq, NSA_W), BF16),
        scratch_shapes=[pltpu.VMEM((NSA_W // LANES, 2 * QB, 2 * LANES), F32),
                        pltpu.VMEM((NSA_W // LANES, 2 * QB, LANES), F32)],
        compiler_params=_cparams(("arbitrary", "arbitrary")),
        name="nsa_attention",
    )(proj3, proj3, proj3, proj3, proj3, proj3, kcmp, vcmp, onehot, overlap, band, d0, d1, tc, expand)


def _gmlp_kernel(u_ref, v_ref, ws_ref, b_ref, o_ref):
    lane = lax.broadcasted_iota(jnp.int32, (GM_CHUNK, LANES), 1)
    low = lane < GM_DIM
    row = lax.broadcasted_iota(jnp.int32, (GM_CHUNK, GM_CHUNK), 0)
    colm = lax.broadcasted_iota(jnp.int32, (GM_CHUNK, GM_CHUNK), 1)
    tril = colm <= row
    ones_bd = _head_ones()
    for cc in range(GM_W // LANES):
        sl = slice(cc * LANES, (cc + 1) * LANES)
        w = jnp.concatenate([jnp.where(tril, ws_ref[2 * cc], 0.0), jnp.where(tril, ws_ref[2 * cc + 1], 0.0)],
                            axis=0).astype(BF16)
        for ch in range(GM_STEP // GM_CHUNK):
            rows = slice(ch * GM_CHUNK, (ch + 1) * GM_CHUNK)
            u = jax.nn.gelu(u_ref[rows, sl].astype(F32), approximate=True)
            v = jax.nn.gelu(v_ref[rows, sl].astype(F32), approximate=True)
            vn = _head_rmsnorm(v, ones_bd).astype(BF16)
            r = jnp.dot(w, vn, preferred_element_type=F32)
            sv = jnp.where(low, r[:GM_CHUNK], r[GM_CHUNK:]) + b_ref[:, sl]
            o_ref[rows, sl] = (u * sv).astype(BF16)


def _gmlp(proj3, ws, bexp):
    b, seq, _ = proj3.shape
    return pl.pallas_call(
        _gmlp_kernel,
        grid=(b, seq // GM_STEP),
        in_specs=[pl.BlockSpec((None, GM_STEP, GM_W), lambda bi, ti: (bi, ti, COL_GMU * LANES // GM_W)),
                  pl.BlockSpec((None, GM_STEP, GM_W), lambda bi, ti: (bi, ti, COL_GMV * LANES // GM_W)),
                  pl.BlockSpec((N_GM, GM_CHUNK, GM_CHUNK), lambda bi, ti: (0, 0, 0)),
                  pl.BlockSpec((GM_CHUNK, GM_W), lambda bi, ti: (0, 0))],
        out_specs=pl.BlockSpec((None, GM_STEP, GM_W), lambda bi, ti: (bi, ti, 0)),
        out_shape=jax.ShapeDtypeStruct((b, seq, GM_W), BF16),
        compiler_params=_cparams(("arbitrary", "arbitrary")),
        name="gmlp",
    )(proj3, proj3, ws, bexp)


FFN_CHUNK = 256
FFN_SPLIT = 2


def _ffn_kernel(x_ref, r_ref, n_ref, m_ref, wo_ref, g_ref, wgu_ref, wd_ref, o_ref, *, tm):
    sub_rows = tm // FFN_SPLIT
    for r0 in range(0, tm, sub_rows):
        rows = slice(r0, r0 + sub_rows)
        mix = jnp.dot(r_ref[rows, :], wo_ref[0:RET_W, :], preferred_element_type=F32)
        mix = mix + jnp.dot(n_ref[rows, :], wo_ref[RET_W:RET_W + NSA_W, :], preferred_element_type=F32)
        mix = mix + jnp.dot(m_ref[rows, :], wo_ref[RET_W + NSA_W:, :], preferred_element_type=F32)
        x = x_ref[rows, :] + mix
        ms = jnp.mean(x * x, axis=-1, keepdims=True)
        h = (x * lax.rsqrt(ms + EPS) * g_ref[...]).astype(BF16)
        acc = x
        for f in range(0, D_FF, FFN_CHUNK):
            gate = jnp.dot(h, wgu_ref[:, f:f + FFN_CHUNK], preferred_element_type=F32)
            up = jnp.dot(h, wgu_ref[:, D_FF + f:D_FF + f + FFN_CHUNK], preferred_element_type=F32)
            act = (gate * jax.nn.sigmoid(gate) * up).astype(BF16)
            acc = acc + jnp.dot(act, wd_ref[f:f + FFN_CHUNK, :], preferred_element_type=F32)
        o_ref[rows, :] = acc


def _outproj_ffn(x2d, ret_o, nsa_o, gm_o, w_out, gain, wgu, wd):
    m = x2d.shape[0]
    tm = 1024
    row = lambda width: pl.BlockSpec((tm, width), lambda i: (i, 0))
    const = lambda arr: pl.BlockSpec(arr.shape, lambda i: (0, 0), pipeline_mode=pl.Buffered(1))
    return pl.pallas_call(
        functools.partial(_ffn_kernel, tm=tm),
        grid=(m // tm,),
        in_specs=[row(D_MODEL), row(RET_W), row(NSA_W), row(GM_W), const(w_out),
                  pl.BlockSpec((1, D_MODEL), lambda i: (0, 0)),
                  const(wgu), const(wd)],
        out_specs=row(D_MODEL),
        out_shape=jax.ShapeDtypeStruct((m, D_MODEL), F32),
        compiler_params=_cparams(("arbitrary",)),
        name="outproj_ffn",
    )(x2d, ret_o, nsa_o, gm_o, w_out, gain, wgu, wd)


def _nq_head_order():
    return [c + (N_NSA // N_KV) * half for c in range(NSA_W // LANES) for half in range(2)]


def _w_in_layout(w):
    wt = w.T
    off_nq = 4 * RET_W
    off_kv = off_nq + NSA_W
    off_gate = off_kv + 3 * 2 * N_KV * HEAD_DIM
    off_gm = off_gate + 3 * N_NSA
    heads = [wt[off_nq + h * HEAD_DIM:off_nq + (h + 1) * HEAD_DIM] for h in _nq_head_order()]
    main = jnp.concatenate([wt[:off_nq]] + heads + [wt[off_kv:off_gate + LANES]], axis=0).astype(BF16)
    return main, wt[off_gm:].astype(BF16)


def _w_out_layout(w):
    heads = [w[RET_W + h * HEAD_DIM:RET_W + (h + 1) * HEAD_DIM] for h in _nq_head_order()]
    return jnp.concatenate([w[:RET_W]] + heads + [w[RET_W + NSA_W:]], axis=0).astype(BF16)


def _gate_expand():
    e = np.zeros((LANES, 3 * NSA_W), np.float32)
    ncol = NSA_W // LANES
    for br in range(3):
        for c in range(ncol):
            for half in range(2):
                head = c + (N_NSA // N_KV) * half
                dst = (br * ncol + c) * LANES + half * HEAD_DIM
                e[head * 3 + br, dst:dst + HEAD_DIM] = 1.0
    return jnp.asarray(e, BF16)


def _rope_tables(seq):
    half = HEAD_DIM // 2
    inv = ROPE_THETA ** (-jnp.arange(half, dtype=F32) / half)
    ang = jnp.arange(seq, dtype=jnp.int32).astype(F32)[:, None] * inv[None, :]
    cos = jnp.tile(jnp.cos(ang), (1, LANES // half))
    sin = jnp.tile(jnp.concatenate([-jnp.sin(ang), jnp.sin(ang)], axis=1), (1, LANES // HEAD_DIM))
    return cos, sin


def _retention_tables():
    c = RET_CHUNK
    log_gamma = jnp.log(1.0 - 2.0 ** (-5.0 - jnp.arange(N_RET, dtype=F32)))
    idx = jnp.arange(c, dtype=F32)
    diff = idx[:, None] - idx[None, :]
    decay = jnp.where(diff >= 0, jnp.exp(jnp.maximum(diff, 0.0)[None] * log_gamma[:, None, None]), 0.0)
    zeta = jnp.exp((c - 1 - idx)[None, :] * log_gamma[:, None])
    xi = jnp.exp((idx + 1)[None, :] * log_gamma[:, None])
    chunk_decay = jnp.exp(c * log_gamma)
    ncol = RET_W // LANES
    dec = decay.reshape(ncol, 2 * c, c)
    xi_st = jnp.broadcast_to(xi.reshape(ncol, 2 * c, 1), (ncol, 2 * c, LANES))
    zeta_l = jnp.repeat(zeta.reshape(ncol, 2, c).transpose(0, 2, 1), HEAD_DIM, axis=2)
    cd = jnp.broadcast_to(jnp.repeat(chunk_decay.reshape(ncol, 2), HEAD_DIM, axis=1)[:, :, None],
                          (ncol, LANES, LANES))
    return dec, xi_st, zeta_l, cd


def _overlap_table(seq):
    n_cmp = (seq - CMP_LEN) // CMP_STRIDE + 1
    n_slc = seq // SEL_BLOCK
    cs = np.arange(n_cmp)[:, None] * CMP_STRIDE
    ss = np.arange(n_slc)[None, :] * SEL_BLOCK
    ov = np.clip(np.minimum(cs + CMP_LEN, ss + SEL_BLOCK) - np.maximum(cs, ss), 0, None) // CMP_STRIDE
    full = np.zeros((seq // CMP_STRIDE, LANES), np.float32)
    full[:n_cmp, :n_slc] = ov
    return jnp.asarray(full, BF16)


def _cmp_band(seq):
    nq = seq // QB
    ncmp = seq // CMP_STRIDE
    x = np.arange(2 * ncmp)[:, None]
    m = np.arange(LANES)[None, :]
    base = (QB // CMP_STRIDE) * nq - 16
    band = np.where(m < LANES - 1, x == m + base, x >= m + base)
    return jnp.asarray(band.astype(np.float32), BF16)


def _block_onehot(seq):
    oh = (np.arange(seq)[:, None] // SEL_BLOCK) == np.arange(LANES)[None, :]
    return jnp.asarray(oh.astype(np.float32), BF16)


def _compress_weights(cmp_pe, cmp_w1, cmp_w2):
    def block_diag2(w):
        zero = jnp.zeros_like(w)
        return jnp.concatenate([jnp.concatenate([w, zero], axis=-1), jnp.concatenate([zero, w], axis=-1)],
                               axis=-2).astype(BF16)

    w1bd = block_diag2(cmp_w1.reshape(2, CMP_LEN, HEAD_DIM, HEAD_DIM))
    w2bd = block_diag2(cmp_w2)
    pe2 = jnp.tile(cmp_pe, (1, 1, N_KV))[:, :, None, :]
    return w1bd[:, :CMP_LEN // 2], w1bd[:, CMP_LEN // 2:], pe2, w2bd


def kernel(x, attn_norm, w_in, w_out, nsa_q_gain, nsa_k_gain, cmp_pe, cmp_w1, cmp_w2, gm_ws, gm_b, ffn_norm,
           w_gate_up, w_down, rel_bias):
    b, seq, _ = x.shape
    assert seq % (2 * TK) == 0 and N_SEL <= seq // SEL_BLOCK <= LANES, "selection blocks must fit one lane row"
    depth = w_in.shape[0]
    cos_tab, sin_tab = _rope_tables(seq)
    ret_tabs = _retention_tables()
    overlap = _overlap_table(seq)
    onehot = _block_onehot(seq)
    band = _cmp_band(seq)
    expand = _gate_expand()
    d0, d1, tc = _bias_tiles(rel_bias)
    ngrp = seq // CMP_STRIDE
    x2d = x.reshape(b * seq, D_MODEL)
    for l in range(depth):
        zero_row = jnp.zeros((5, LANES), F32)
        head_gains = jnp.concatenate([jnp.tile(nsa_q_gain[l] * HEAD_DIM ** -0.5, 2)[None],
                                      jnp.tile(nsa_k_gain[l, 1], 2)[None],
                                      jnp.tile(nsa_k_gain[l, 2], 2)[None], zero_row], axis=0)
        proj, cmp_grp = _inproj(x2d, attn_norm[l][None], *_w_in_layout(w_in[l]), cos_tab, sin_tab, head_gains, seq)
        proj3 = proj.reshape(b, seq, PROJ_W)
        ret_o = _retention(proj3, ret_tabs)
        cmp_in = cmp_grp.reshape(b, ngrp, CMP_STRIDE * 2 * LANES)
        wtop, wbot, pe2, w2bd = _compress_weights(cmp_pe[l], cmp_w1[l], cmp_w2[l])
        kgain = jnp.broadcast_to(jnp.tile(nsa_k_gain[l, 0], 2)[None], (8, LANES))
        kcmp, vcmp = _compress(cmp_in, wtop, wbot, pe2, w2bd, kgain)
        nsa_o = _nsa(proj3, kcmp, vcmp, onehot, overlap, band, d0, d1, tc, expand)
        gm_o = _gmlp(proj3, gm_ws[l], jnp.repeat(gm_b[l].T, GM_DIM, axis=1))
        x2d = _outproj_ffn(x2d, ret_o.reshape(b * seq, RET_W), nsa_o.reshape(b * seq, NSA_W),
                           gm_o.reshape(b * seq, GM_W), _w_out_layout(w_out[l]), ffn_norm[l][None],
                           w_gate_up[l].astype(BF16), w_down[l].astype(BF16))
    return x2d.reshape(b, seq, D_MODEL)
```

```python
import functools
import math

import numpy as np
import jax
import jax.numpy as jnp
from jax import lax
from jax.experimental import pallas as pl
from jax.experimental.pallas import tpu as pltpu

F32 = jnp.float32
BF16 = jnp.bfloat16

D_MODEL = 1024
HEAD_DIM = 64
N_RET = 6
N_NSA = 6
N_KV = 2
N_GM = 4
GM_DIM = 64
RET_W = N_RET * HEAD_DIM
NSA_W = N_NSA * HEAD_DIM
GM_W = N_GM * GM_DIM
RET_CHUNK = 128
CMP_LEN = 32
CMP_STRIDE = 16
SEL_BLOCK = 64
N_SEL = 16
WINDOW = 512
GM_CHUNK = 128
N_BUCKETS = 32
MAX_DISTANCE = 128
ROPE_THETA = 10000.0
D_FF = 2816
EPS = 1e-6
BIG = 1e9
NEG = -1e30
LOG2E = math.log2(math.e)

LANES = 128
PROJ_W = 3328
COL_RQ, COL_RK, COL_RV, COL_RG = 0, 3, 6, 9
COL_NQ = 12
COL_CMPK, COL_CMPV, COL_SELK, COL_SELV, COL_WINK, COL_WINV = 15, 16, 17, 18, 19, 20
COL_GATE = 21
COL_GMU, COL_GMV = 22, 24
QB = 256
TK = 256
FAR_TILES = 8
INPROJ_SPLIT = 2
RET_STEP = 512
GM_STEP = 512
VMEM_LIMIT = 56 * 1024 * 1024


def _bucket_thresholds():
    n = np.arange(0, 4 * MAX_DISTANCE)
    max_exact = N_BUCKETS // 2
    nf = np.maximum(n, 1).astype(np.float64)
    large = max_exact + np.floor(np.log(nf / max_exact) / math.log(MAX_DISTANCE / max_exact)
                                 * (N_BUCKETS - max_exact)).astype(np.int64)
    bucket = np.where(n < max_exact, n, np.minimum(large, N_BUCKETS - 1))
    return [int(np.min(n[bucket >= b])) for b in range(N_BUCKETS)]


BUCKET_THR = _bucket_thresholds()


def _cparams(sem):
    return pltpu.CompilerParams(dimension_semantics=sem, vmem_limit_bytes=VMEM_LIMIT)


def _head_ones():
    r = lax.broadcasted_iota(jnp.int32, (LANES, LANES), 0) // HEAD_DIM
    c = lax.broadcasted_iota(jnp.int32, (LANES, LANES), 1) // HEAD_DIM
    return jnp.where(r == c, 1.0, 0.0).astype(BF16)


def _head_sum(x, ones_bd):
    return jnp.dot(x.astype(BF16), ones_bd, preferred_element_type=F32)


def _head_rmsnorm(x, ones_bd):
    return x * lax.rsqrt(_head_sum(x * x, ones_bd) * (1.0 / HEAD_DIM) + EPS)


def _dot_nt(a, b):
    return lax.dot_general(a, b, (((1,), (1,)), ((), ())), preferred_element_type=F32)


def _dot_tn(a, b):
    return lax.dot_general(a, b, (((0,), (0,)), ((), ())), preferred_element_type=F32)


def _stack_heads(q, low):
    zero = jnp.zeros_like(q)
    return jnp.concatenate([jnp.where(low, q, zero), jnp.where(low, zero, q)], axis=0)


def _inproj_kernel(x_ref, g_ref, wa_ref, wb_ref, cos_ref, sin_ref, hg_ref, o_ref, grp_ref, cmp_scr, *, tm):
    sub_rows = tm // INPROJ_SPLIT
    lane = lax.broadcasted_iota(jnp.int32, (sub_rows, LANES), 1)
    first_half = (lane % HEAD_DIM) < (HEAD_DIM // 2)
    ones_bd = _head_ones()
    norm_gain = {COL_NQ: 0, COL_NQ + 1: 0, COL_NQ + 2: 0, COL_SELK: 1, COL_WINK: 2}
    deferred = []
    for r0 in range(0, tm, sub_rows):
        rows = slice(r0, r0 + sub_rows)
        x = x_ref[rows, :]
        ms = jnp.mean(x * x, axis=-1, keepdims=True)
        h = (x * lax.rsqrt(ms + EPS) * g_ref[...]).astype(BF16)
        cos = cos_ref[rows, :]
        sin = sin_ref[rows, :]
        for s in range(0, PROJ_W // LANES, 2):
            if s < COL_GMU:
                acc = _dot_nt(h, wa_ref[s * LANES:(s + 2) * LANES, :])
            else:
                acc = _dot_nt(h, wb_ref[(s - COL_GMU) * LANES:(s - COL_GMU + 2) * LANES, :])
            for sub in range(2):
                j = s + sub
                a = acc[:, sub * LANES:(sub + 1) * LANES]
                if j < COL_RV:
                    swapped = jnp.where(first_half, pltpu.roll(a, LANES - HEAD_DIM // 2, axis=1),
                                        pltpu.roll(a, HEAD_DIM // 2, axis=1))
                    a = a * cos + swapped * sin
                    if j >= COL_RK:
                        a = a * (HEAD_DIM ** -0.5)
                elif j in norm_gain:
                    deferred.append((rows, j, a))
                    continue
                elif j in (COL_CMPK, COL_CMPV):
                    cmp_scr[j - COL_CMPK, rows, :] = a
                o_ref[rows, j * LANES:(j + 1) * LANES] = a.astype(BF16)
    for rows, j, a in deferred:
        a = _head_rmsnorm(a, ones_bd) * hg_ref[norm_gain[j]:norm_gain[j] + 1, :]
        o_ref[rows, j * LANES:(j + 1) * LANES] = a.astype(BF16)
    for l in range(CMP_STRIDE):
        for kv in range(2):
            grp_ref[:, (2 * l + kv) * LANES:(2 * l + kv + 1) * LANES] = (
                cmp_scr[kv, pl.ds(l, tm // CMP_STRIDE, stride=CMP_STRIDE), :].astype(BF16))


def _inproj(x2d, gain, w_main, w_gm, cos_tab, sin_tab, head_gains, seq):
    m = x2d.shape[0]
    tm = 1024
    nt = seq // tm
    const = lambda arr: pl.BlockSpec(arr.shape, lambda i: (0, 0), pipeline_mode=pl.Buffered(1))
    return pl.pallas_call(
        functools.partial(_inproj_kernel, tm=tm),
        grid=(m // tm,),
        in_specs=[
            pl.BlockSpec((tm, D_MODEL), lambda i: (i, 0)),
            pl.BlockSpec((1, D_MODEL), lambda i: (0, 0)),
            const(w_main), const(w_gm),
            pl.BlockSpec((tm, LANES), lambda i: (i % nt, 0)),
            pl.BlockSpec((tm, LANES), lambda i: (i % nt, 0)),
            pl.BlockSpec((8, LANES), lambda i: (0, 0)),
        ],
        out_specs=[pl.BlockSpec((tm, PROJ_W), lambda i: (i, 0)),
                   pl.BlockSpec((tm // CMP_STRIDE, CMP_STRIDE * 2 * LANES), lambda i: (i, 0))],
        out_shape=[jax.ShapeDtypeStruct((m, PROJ_W), BF16),
                   jax.ShapeDtypeStruct((m // CMP_STRIDE, CMP_STRIDE * 2 * LANES), BF16)],
        scratch_shapes=[pltpu.VMEM((2, tm, LANES), F32)],
        compiler_params=_cparams(("arbitrary",)),
        name="inproj",
    )(x2d, gain, w_main, w_gm, cos_tab, sin_tab, head_gains)


def _ret_kernel(q_ref, k_ref, v_ref, g_ref, dec_ref, xi_ref, zeta_ref, cd_ref, o_ref, r_ref):
    @pl.when(pl.program_id(1) == 0)
    def _():
        r_ref[...] = jnp.zeros_like(r_ref)

    lane = lax.broadcasted_iota(jnp.int32, (RET_CHUNK, LANES), 1)
    low = lane < HEAD_DIM
    ones_bd = _head_ones()
    for c in range(RET_W // LANES):
        sl = slice(c * LANES, (c + 1) * LANES)
        state = r_ref[c]
        for ch in range(RET_STEP // RET_CHUNK):
            rows = slice(ch * RET_CHUNK, (ch + 1) * RET_CHUNK)
            q = q_ref[rows, sl]
            k = k_ref[rows, sl]
            v = v_ref[rows, sl]
            g = g_ref[rows, sl].astype(F32)
            qs = _stack_heads(q, low)
            scores = _dot_nt(qs, k) * dec_ref[c]
            inner = jnp.dot(scores.astype(BF16), v, preferred_element_type=F32)
            cross = jnp.dot(qs, state.astype(BF16), preferred_element_type=F32) * xi_ref[c]
            tot = inner + cross
            o = jnp.where(low, tot[:RET_CHUNK], tot[RET_CHUNK:])
            kz = (k.astype(F32) * zeta_ref[c]).astype(BF16)
            state = cd_ref[c] * state + _dot_tn(kz, v)
            y = _head_rmsnorm(o, ones_bd) * (g * jax.nn.sigmoid(g))
            o_ref[rows, sl] = y.astype(BF16)
        r_ref[c] = state


def _retention(proj3, tabs):
    b, seq, _ = proj3.shape
    dec, xi, zeta, cd = tabs
    ncol = RET_W // LANES
    qkvg = [pl.BlockSpec((None, RET_STEP, RET_W), functools.partial(lambda bi, ti, col: (bi, ti, col), col=col))
            for col in range(4)]
    const3 = lambda shape: pl.BlockSpec(shape, lambda bi, ti: (0, 0, 0))
    return pl.pallas_call(
        _ret_kernel,
        grid=(b, seq // RET_STEP),
        in_specs=qkvg + [const3((ncol, 2 * RET_CHUNK, LANES)), const3((ncol, 2 * RET_CHUNK, LANES)),
                         const3((ncol, RET_CHUNK, LANES)), const3((ncol, LANES, LANES))],
        out_specs=pl.BlockSpec((None, RET_STEP, RET_W), lambda bi, ti: (bi, ti, 0)),
        out_shape=jax.ShapeDtypeStruct((b, seq, RET_W), BF16),
        scratch_shapes=[pltpu.VMEM((ncol, LANES, LANES), F32)],
        compiler_params=_cparams(("arbitrary", "arbitrary")),
        name="retention",
    )(proj3, proj3, proj3, proj3, dec, xi, zeta, cd)


def _compress_kernel(a_ref, wtop_ref, wbot_ref, pe_ref, w2_ref, kg_ref, kc_ref, vc_ref, *, ngrp):
    half = CMP_LEN // 2
    ones_bd = _head_ones()
    for kv in range(2):
        top = jnp.zeros((ngrp, LANES), F32)
        bot = jnp.zeros((ngrp, LANES), F32)
        for l in range(half):
            x = a_ref[:, l * 2 * LANES + kv * LANES:l * 2 * LANES + (kv + 1) * LANES].astype(F32)
            top = top + jnp.dot((x + pe_ref[kv, l]).astype(BF16), wtop_ref[kv, l], preferred_element_type=F32)
            bot = bot + jnp.dot((x + pe_ref[kv, half + l]).astype(BF16), wbot_ref[kv, l],
                                preferred_element_type=F32)
        hid = jax.nn.gelu(top + pltpu.roll(bot, ngrp - 1, axis=0), approximate=True)
        out = jnp.dot(hid.astype(BF16), w2_ref[kv], preferred_element_type=F32)
        if kv == 0:
            kc_ref[...] = (_head_rmsnorm(out, ones_bd) * kg_ref[0:1, :]).astype(BF16)
        else:
            vc_ref[...] = out.astype(BF16)


def _compress(a, wtop, wbot, pe2, w2bd, kgain):
    b, ngrp, width = a.shape
    full = lambda arr: pl.BlockSpec(arr.shape, lambda bi: (0,) * arr.ndim)
    out_spec = pl.BlockSpec((None, ngrp, LANES), lambda bi: (bi, 0, 0))
    return pl.pallas_call(
        functools.partial(_compress_kernel, ngrp=ngrp),
        grid=(b,),
        in_specs=[pl.BlockSpec((None, ngrp, width), lambda bi: (bi, 0, 0)),
                  full(wtop), full(wbot), full(pe2), full(w2bd), full(kgain)],
        out_specs=[out_spec, out_spec],
        out_shape=[jax.ShapeDtypeStruct((b, ngrp, LANES), BF16)] * 2,
        compiler_params=_cparams(("arbitrary",)),
        name="nsa_compress",
    )(a, wtop, wbot, pe2, w2bd, kgain)


def _bias_kernel(rb_ref, d0_ref, d1_ref, tc_ref):
    h = pl.program_id(0)
    far = rb_ref[N_BUCKETS - 1, h]

    def rel(n):
        val = jnp.full(n.shape, (rb_ref[0, h] - far) * LOG2E, F32)
        for bkt in range(1, N_BUCKETS):
            val = jnp.where(n >= BUCKET_THR[bkt], (rb_ref[bkt, h] - far) * LOG2E, val)
        return val

    ql = lax.broadcasted_iota(jnp.int32, (QB, TK), 0)
    kl = lax.broadcasted_iota(jnp.int32, (QB, TK), 1)
    d = ql - kl
    d0_ref[...] = jnp.where(d >= 0, rel(d), NEG)
    d1_ref[...] = rel(d + TK)
    ql2 = lax.broadcasted_iota(jnp.int32, (QB, LANES), 0)
    m = lax.broadcasted_iota(jnp.int32, (QB, LANES), 1)
    dc = ql2 - CMP_STRIDE * (m - 16) - (CMP_LEN - 1)
    tc_ref[...] = jnp.where(dc >= 0, rel(dc), NEG).astype(BF16)


def _bias_tiles(rel_bias):
    return pl.pallas_call(
        _bias_kernel,
        grid=(N_NSA,),
        in_specs=[pl.BlockSpec(memory_space=pltpu.SMEM)],
        out_specs=[pl.BlockSpec((None, QB, TK), lambda h: (h, 0, 0)),
                   pl.BlockSpec((None, QB, TK), lambda h: (h, 0, 0)),
                   pl.BlockSpec((None, QB, LANES), lambda h: (h, 0, 0))],
        out_shape=[jax.ShapeDtypeStruct((N_NSA, QB, TK), F32),
                   jax.ShapeDtypeStruct((N_NSA, QB, TK), F32),
                   jax.ShapeDtypeStruct((N_NSA, QB, LANES), BF16)],
        compiler_params=_cparams(("arbitrary",)),
        name="t5_bias_tiles",
    )(rel_bias)


def _nsa_kernel(q_ref, gt_ref, sk_ref, sv_ref, wk_ref, wv_ref, kc_ref, vc_ref, oh_ref, ov_ref, band_ref,
                d0_ref, d1_ref, tc_ref, e_ref, o_ref, acc_ref, m_ref, *, ncmp):
    i = pl.program_id(1)
    t0 = i * QB
    ncol = NSA_W // LANES
    sq = 2 * QB
    low = lax.broadcasted_iota(jnp.int32, (QB, LANES), 1) < HEAD_DIM
    qs = [_stack_heads(q_ref[:, c * LANES:(c + 1) * LANES], low) for c in range(ncol)]

    def stacked(ref, c):
        return jnp.concatenate([ref[c], ref[c + ncol]], axis=0)

    ones_col = jnp.ones((TK, LANES), BF16)

    band = band_ref[pl.ds(pl.multiple_of((QB // CMP_STRIDE) * (pl.num_programs(1) - i), CMP_STRIDE), ncmp), :]
    kc_aug = jnp.concatenate([kc_ref[...], band], axis=1)
    vc_ov = jnp.concatenate([vc_ref[...], ov_ref[...]], axis=1)
    imp = jnp.zeros((sq, LANES), F32)
    o_cmp = []
    s_cmp = [_dot_nt(jnp.concatenate([qs[c], stacked(tc_ref, c)], axis=1), kc_aug) for c in range(ncol)]

    start_d = pl.multiple_of(t0, TK)
    start_p = pl.multiple_of(jnp.maximum(t0 - TK, 0), TK)
    start_e = pl.multiple_of(jnp.maximum(t0 - 2 * TK, 0), TK)
    s_win = [[_dot_nt(qs[c], wk_ref[pl.ds(st, TK), :]) for st in (start_d, start_p, start_e)] for c in range(ncol)]

    for c in range(ncol):
        s = s_cmp[c]
        mx = jnp.maximum(jnp.max(s, axis=-1, keepdims=True), -1e20)
        e = jnp.exp2(s - mx)
        den = jnp.sum(e, axis=-1, keepdims=True)
        inv = jnp.where(den > 0.0, 1.0 / den, 0.0)
        both = jnp.dot(e.astype(BF16), vc_ov, preferred_element_type=F32) * inv
        imp = imp + both[:, LANES:]
        o_cmp.append(jnp.where(low, both[:QB, :LANES], both[QB:, :LANES]))

    off_prev = jnp.where(i >= 1, 0.0, -NEG)
    off_edge = jnp.where(i >= 2, 0.0, -NEG)
    ql_t = lax.broadcasted_iota(jnp.int32, (sq, TK), 0) % QB
    kl_t = lax.broadcasted_iota(jnp.int32, (sq, TK), 1)
    edge = jnp.where(kl_t > ql_t, 0.0, NEG)
    wv_all = jnp.concatenate([
        jnp.concatenate([wv_ref[pl.ds(st, TK), :], ones_col], axis=1) for st in (start_d, start_p, start_e)], axis=0)
    o_win = []
    for c in range(ncol):
        s0 = s_win[c][0] + stacked(d0_ref, c)
        s1 = s_win[c][1] + stacked(d1_ref, c)
        s2 = s_win[c][2] + edge
        mx = jnp.maximum(jnp.maximum(jnp.max(s0, axis=-1, keepdims=True),
                                     jnp.max(s1, axis=-1, keepdims=True) - off_prev),
                         jnp.max(s2, axis=-1, keepdims=True) - off_edge)
        p = jnp.exp2(jnp.concatenate([s0 - mx, s1 - (mx + off_prev), s2 - (mx + off_edge)], axis=1)).astype(BF16)
        ow = jnp.dot(p, wv_all, preferred_element_type=F32)
        o_win.append(jnp.where(low, ow[:QB, :LANES] / ow[:QB, LANES:], ow[QB:, :LANES] / ow[QB:, LANES:]))

    imp_t = imp.T
    blk = lax.broadcasted_iota(jnp.int32, (LANES, sq), 0)
    cur = (t0 + lax.broadcasted_iota(jnp.int32, (LANES, sq), 1) % QB) // SEL_BLOCK
    forced = (blk == 0) | (blk == cur) | (blk == cur - 1)
    imp_t = jnp.where(forced, -jnp.inf, jnp.where(blk > cur, -BIG, imp_t))
    blk_f = blk.astype(F32)

    val = imp_t
    for _ in range(N_SEL - 3):
        top = jnp.max(val, axis=0, keepdims=True)
        idx = jnp.min(jnp.where(val == top, blk_f, 1e6), axis=0, keepdims=True)
        val = jnp.where(blk_f == idx, -jnp.inf, val)
    msel = jnp.where(val == -jnp.inf, 0.0, NEG).T.astype(BF16)

    qaug = [jnp.concatenate([qs[c], msel], axis=1) for c in range(ncol)]

    def sel_scores(c, start, width):
        kaug = jnp.concatenate([sk_ref[pl.ds(start, width), :], oh_ref[pl.ds(start, width), :]], axis=1)
        return _dot_nt(qaug[c], kaug)

    def sel_vaug(start, width):
        return jnp.concatenate([sv_ref[pl.ds(start, width), :], jnp.ones((width, LANES), BF16)], axis=1)

    def lanes(x, width):
        return jnp.concatenate([x] * (width // LANES), axis=1)

    s_near = [(sel_scores(c, start_d, TK), sel_scores(c, start_p, TK)) for c in range(ncol)]
    for c in range(ncol):
        s0 = s_near[c][0] + stacked(d0_ref, c)
        s1 = s_near[c][1] + stacked(d1_ref, c)
        mx = jnp.maximum(jnp.max(s0, axis=-1, keepdims=True), jnp.max(s1, axis=-1, keepdims=True) - off_prev)
        p0 = jnp.exp2(s0 - mx).astype(BF16)
        p1 = jnp.exp2(s1 - (mx + off_prev)).astype(BF16)
        acc_ref[c] = (jnp.dot(p0, sel_vaug(start_d, TK), preferred_element_type=F32)
                      + jnp.dot(p1, sel_vaug(start_p, TK), preferred_element_type=F32))
        m_ref[c] = jnp.broadcast_to(mx, (sq, LANES))

    def far_tile(kt0, tiles):
        start = pl.multiple_of(kt0 * TK, tiles * TK)
        width = tiles * TK
        vaug = sel_vaug(start, width)
        scores = [sel_scores(c, start, width) for c in range(ncol)]
        for c in range(ncol):
            s = scores[c]
            m_old = m_ref[c]
            m_new = jnp.maximum(m_old, jnp.max(s, axis=-1, keepdims=True))
            alpha = jnp.exp2(m_old - m_new)
            p = jnp.exp2(s - lanes(m_new, width)).astype(BF16)
            acc_ref[c] = lanes(alpha, 2 * LANES) * acc_ref[c] + jnp.dot(p, vaug, preferred_element_type=F32)
            m_ref[c] = m_new

    n_far = jnp.maximum(i - 1, 0)

    def far_loop(tiles):
        def body(kt, carry):
            far_tile(kt * tiles, tiles)
            return carry
        return body

    tiles, done = FAR_TILES, 0
    while tiles >= 1:
        count = n_far // tiles
        lax.fori_loop(done, count, far_loop(tiles), 0)
        tiles, done = tiles // 2, count * 2

    gate = jax.nn.sigmoid(gt_ref[...].astype(F32)).astype(BF16)
    gexp = jnp.dot(gate, e_ref[...], preferred_element_type=F32)
    for c in range(ncol):
        acc = acc_ref[c]
        o_sel = jnp.where(low, acc[:QB, :LANES] / acc[:QB, LANES:], acc[QB:, :LANES] / acc[QB:, LANES:])
        gc = lambda br: gexp[:, (br * ncol + c) * LANES:(br * ncol + c + 1) * LANES]
        o_ref[:, c * LANES:(c + 1) * LANES] = (gc(0) * o_cmp[c] + gc(1) * o_sel + gc(2) * o_win[c]).astype(BF16)


def _nsa(proj3, kcmp, vcmp, onehot, overlap, band, d0, d1, tc, expand):
    b, seq, _ = proj3.shape
    ncmp = kcmp.shape[1]
    col = lambda width, cidx: pl.BlockSpec((None, QB, width), lambda bi, qi: (bi, qi, cidx))
    res = lambda cidx: pl.BlockSpec((None, seq, LANES), lambda bi, qi: (bi, 0, cidx))
    full = lambda arr: pl.BlockSpec(arr.shape, lambda bi, qi: (0,) * arr.ndim)
    cmp_spec = pl.BlockSpec((None, ncmp, LANES), lambda bi, qi: (bi, 0, 0))
    return pl.pallas_call(
        functools.partial(_nsa_kernel, ncmp=ncmp),
        grid=(b, seq // QB),
        in_specs=[col(NSA_W, COL_NQ * LANES // NSA_W), col(LANES, COL_GATE),
                  res(COL_SELK), res(COL_SELV), res(COL_WINK), res(COL_WINV),
                  cmp_spec, cmp_spec, full(onehot), full(overlap), full(band), full(d0), full(d1), full(tc),
                  full(expand)],
        out_specs=pl.BlockSpec((None, QB, NSA_W), lambda bi, qi: (bi, qi, 0)),
        out_shape=jax.ShapeDtypeStruct((b, seq, NSA_W), BF16),
        scratch_shapes=[pltpu.VMEM((NSA_W // LANES, 2 * QB, 2 * LANES), F32),
                        pltpu.VMEM((NSA_W // LANES, 2 * QB, LANES), F32)],
        compiler_params=_cparams(("arbitrary", "arbitrary")),
        name="nsa_attention",
    )(proj3, proj3, proj3, proj3, proj3, proj3, kcmp, vcmp, onehot, overlap, band, d0, d1, tc, expand)


def _gmlp_kernel(u_ref, v_ref, ws_ref, b_ref, o_ref):
    lane = lax.broadcasted_iota(jnp.int32, (GM_CHUNK, LANES), 1)
    low = lane < GM_DIM
    row = lax.broadcasted_iota(jnp.int32, (GM_CHUNK, GM_CHUNK), 0)
    colm = lax.broadcasted_iota(jnp.int32, (GM_CHUNK, GM_CHUNK), 1)
    tril = colm <= row
    ones_bd = _head_ones()
    for cc in range(GM_W // LANES):
        sl = slice(cc * LANES, (cc + 1) * LANES)
        w = jnp.concatenate([jnp.where(tril, ws_ref[2 * cc], 0.0), jnp.where(tril, ws_ref[2 * cc + 1], 0.0)],
                            axis=0).astype(BF16)
        for ch in range(GM_STEP // GM_CHUNK):
            rows = slice(ch * GM_CHUNK, (ch + 1) * GM_CHUNK)
            u = jax.nn.gelu(u_ref[rows, sl].astype(F32), approximate=True)
            v = jax.nn.gelu(v_ref[rows, sl].astype(F32), approximate=True)
            vn = _head_rmsnorm(v, ones_bd).astype(BF16)
            r = jnp.dot(w, vn, preferred_element_type=F32)
            sv = jnp.where(low, r[:GM_CHUNK], r[GM_CHUNK:]) + b_ref[:, sl]
            o_ref[rows, sl] = (u * sv).astype(BF16)


def _gmlp(proj3, ws, bexp):
    b, seq, _ = proj3.shape
    return pl.pallas_call(
        _gmlp_kernel,
        grid=(b, seq // GM_STEP),
        in_specs=[pl.BlockSpec((None, GM_STEP, GM_W), lambda bi, ti: (bi, ti, COL_GMU * LANES // GM_W)),
                  pl.BlockSpec((None, GM_STEP, GM_W), lambda bi, ti: (bi, ti, COL_GMV * LANES // GM_W)),
                  pl.BlockSpec((N_GM, GM_CHUNK, GM_CHUNK), lambda bi, ti: (0, 0, 0)),
                  pl.BlockSpec((GM_CHUNK, GM_W), lambda bi, ti: (0, 0))],
        out_specs=pl.BlockSpec((None, GM_STEP, GM_W), lambda bi, ti: (bi, ti, 0)),
        out_shape=jax.ShapeDtypeStruct((b, seq, GM_W), BF16),
        compiler_params=_cparams(("arbitrary", "arbitrary")),
        name="gmlp",
    )(proj3, proj3, ws, bexp)


FFN_CHUNK = 256
FFN_SPLIT = 2


def _ffn_kernel(x_ref, r_ref, n_ref, m_ref, wo_ref, g_ref, wgu_ref, wd_ref, o_ref, *, tm):
    sub_rows = tm // FFN_SPLIT
    for r0 in range(0, tm, sub_rows):
        rows = slice(r0, r0 + sub_rows)
        mix = jnp.dot(r_ref[rows, :], wo_ref[0:RET_W, :], preferred_element_type=F32)
        mix = mix + jnp.dot(n_ref[rows, :], wo_ref[RET_W:RET_W + NSA_W, :], preferred_element_type=F32)
        mix = mix + jnp.dot(m_ref[rows, :], wo_ref[RET_W + NSA_W:, :], preferred_element_type=F32)
        x = x_ref[rows, :] + mix
        ms = jnp.mean(x * x, axis=-1, keepdims=True)
        h = (x * lax.rsqrt(ms + EPS) * g_ref[...]).astype(BF16)
        acc = x
        for f in range(0, D_FF, FFN_CHUNK):
            gate = jnp.dot(h, wgu_ref[:, f:f + FFN_CHUNK], preferred_element_type=F32)
            up = jnp.dot(h, wgu_ref[:, D_FF + f:D_FF + f + FFN_CHUNK], preferred_element_type=F32)
            act = (gate * jax.nn.sigmoid(gate) * up).astype(BF16)
            acc = acc + jnp.dot(act, wd_ref[f:f + FFN_CHUNK, :], preferred_element_type=F32)
        o_ref[rows, :] = acc


def _outproj_ffn(x2d, ret_o, nsa_o, gm_o, w_out, gain, wgu, wd, layer):
    m = x2d.shape[0]
    tm = 1024
    row = lambda width: pl.BlockSpec((tm, width), lambda i: (i, 0))
    const = lambda arr: pl.BlockSpec(arr.shape, lambda i: (0, 0), pipeline_mode=pl.Buffered(1))
    of_layer = lambda arr: pl.BlockSpec((None,) + arr.shape[1:], lambda i: (layer, 0, 0),
                                        pipeline_mode=pl.Buffered(1))
    return pl.pallas_call(
        functools.partial(_ffn_kernel, tm=tm),
        grid=(m // tm,),
        in_specs=[row(D_MODEL), row(RET_W), row(NSA_W), row(GM_W), const(w_out),
                  pl.BlockSpec((1, D_MODEL), lambda i: (0, 0)),
                  of_layer(wgu), of_layer(wd)],
        out_specs=row(D_MODEL),
        out_shape=jax.ShapeDtypeStruct((m, D_MODEL), F32),
        compiler_params=_cparams(("arbitrary",)),
        name="outproj_ffn",
    )(x2d, ret_o, nsa_o, gm_o, w_out, gain, wgu, wd)


def _nq_head_order():
    return [c + (N_NSA // N_KV) * half for c in range(NSA_W // LANES) for half in range(2)]


def _w_in_layout(w):
    wt = w.T
    off_nq = 4 * RET_W
    off_kv = off_nq + NSA_W
    off_gate = off_kv + 3 * 2 * N_KV * HEAD_DIM
    off_gm = off_gate + 3 * N_NSA
    heads = [wt[off_nq + h * HEAD_DIM:off_nq + (h + 1) * HEAD_DIM] for h in _nq_head_order()]
    main = jnp.concatenate([wt[:off_nq]] + heads + [wt[off_kv:off_gate + LANES]], axis=0).astype(BF16)
    return main, wt[off_gm:].astype(BF16)


def _w_out_layout(w):
    heads = [w[RET_W + h * HEAD_DIM:RET_W + (h + 1) * HEAD_DIM] for h in _nq_head_order()]
    return jnp.concatenate([w[:RET_W]] + heads + [w[RET_W + NSA_W:]], axis=0).astype(BF16)


def _gate_expand():
    e = np.zeros((LANES, 3 * NSA_W), np.float32)
    ncol = NSA_W // LANES
    for br in range(3):
        for c in range(ncol):
            for half in range(2):
                head = c + (N_NSA // N_KV) * half
                dst = (br * ncol + c) * LANES + half * HEAD_DIM
                e[head * 3 + br, dst:dst + HEAD_DIM] = 1.0
    return jnp.asarray(e, BF16)


def _rope_tables(seq):
    half = HEAD_DIM // 2
    inv = ROPE_THETA ** (-jnp.arange(half, dtype=F32) / half)
    ang = jnp.arange(seq, dtype=jnp.int32).astype(F32)[:, None] * inv[None, :]
    cos = jnp.tile(jnp.cos(ang), (1, LANES // half))
    sin = jnp.tile(jnp.concatenate([-jnp.sin(ang), jnp.sin(ang)], axis=1), (1, LANES // HEAD_DIM))
    return cos, sin


def _retention_tables():
    c = RET_CHUNK
    log_gamma = jnp.log(1.0 - 2.0 ** (-5.0 - jnp.arange(N_RET, dtype=F32)))
    idx = jnp.arange(c, dtype=F32)
    diff = idx[:, None] - idx[None, :]
    decay = jnp.where(diff >= 0, jnp.exp(jnp.maximum(diff, 0.0)[None] * log_gamma[:, None, None]), 0.0)
    zeta = jnp.exp((c - 1 - idx)[None, :] * log_gamma[:, None])
    xi = jnp.exp((idx + 1)[None, :] * log_gamma[:, None])
    chunk_decay = jnp.exp(c * log_gamma)
    ncol = RET_W // LANES
    dec = decay.reshape(ncol, 2 * c, c)
    xi_st = jnp.broadcast_to(xi.reshape(ncol, 2 * c, 1), (ncol, 2 * c, LANES))
    zeta_l = jnp.repeat(zeta.reshape(ncol, 2, c).transpose(0, 2, 1), HEAD_DIM, axis=2)
    cd = jnp.broadcast_to(jnp.repeat(chunk_decay.reshape(ncol, 2), HEAD_DIM, axis=1)[:, :, None],
                          (ncol, LANES, LANES))
    return dec, xi_st, zeta_l, cd


def _overlap_table(seq):
    n_cmp = (seq - CMP_LEN) // CMP_STRIDE + 1
    n_slc = seq // SEL_BLOCK
    cs = np.arange(n_cmp)[:, None] * CMP_STRIDE
    ss = np.arange(n_slc)[None, :] * SEL_BLOCK
    ov = np.clip(np.minimum(cs + CMP_LEN, ss + SEL_BLOCK) - np.maximum(cs, ss), 0, None) // CMP_STRIDE
    full = np.zeros((seq // CMP_STRIDE, LANES), np.float32)
    full[:n_cmp, :n_slc] = ov
    return jnp.asarray(full, BF16)


def _cmp_band(seq):
    nq = seq // QB
    ncmp = seq // CMP_STRIDE
    x = np.arange(2 * ncmp)[:, None]
    m = np.arange(LANES)[None, :]
    base = (QB // CMP_STRIDE) * nq - 16
    band = np.where(m < LANES - 1, x == m + base, x >= m + base)
    return jnp.asarray(band.astype(np.float32), BF16)


def _block_onehot(seq):
    oh = (np.arange(seq)[:, None] // SEL_BLOCK) == np.arange(LANES)[None, :]
    return jnp.asarray(oh.astype(np.float32), BF16)


def _compress_weights(cmp_pe, cmp_w1, cmp_w2):
    def block_diag2(w):
        zero = jnp.zeros_like(w)
        return jnp.concatenate([jnp.concatenate([w, zero], axis=-1), jnp.concatenate([zero, w], axis=-1)],
                               axis=-2).astype(BF16)

    w1bd = block_diag2(cmp_w1.reshape(2, CMP_LEN, HEAD_DIM, HEAD_DIM))
    w2bd = block_diag2(cmp_w2)
    pe2 = jnp.tile(cmp_pe, (1, 1, N_KV))[:, :, None, :]
    return w1bd[:, :CMP_LEN // 2], w1bd[:, CMP_LEN // 2:], pe2, w2bd


def kernel(x, attn_norm, w_in, w_out, nsa_q_gain, nsa_k_gain, cmp_pe, cmp_w1, cmp_w2, gm_ws, gm_b, ffn_norm,
           w_gate_up, w_down, rel_bias):
    b, seq, _ = x.shape
    assert seq % (2 * TK) == 0 and N_SEL <= seq // SEL_BLOCK <= LANES, "selection blocks must fit one lane row"
    depth = w_in.shape[0]
    cos_tab, sin_tab = _rope_tables(seq)
    ret_tabs = _retention_tables()
    overlap = _overlap_table(seq)
    onehot = _block_onehot(seq)
    band = _cmp_band(seq)
    expand = _gate_expand()
    d0, d1, tc = _bias_tiles(rel_bias)
    ngrp = seq // CMP_STRIDE
    wgu_all = w_gate_up.astype(BF16)
    wd_all = w_down.astype(BF16)
    x2d = x.reshape(b * seq, D_MODEL)
    for l in range(depth):
        zero_row = jnp.zeros((5, LANES), F32)
        head_gains = jnp.concatenate([jnp.tile(nsa_q_gain[l] * (HEAD_DIM ** -0.5 * LOG2E), 2)[None],
                                      jnp.tile(nsa_k_gain[l, 1], 2)[None],
                                      jnp.tile(nsa_k_gain[l, 2], 2)[None], zero_row], axis=0)
        proj, cmp_grp = _inproj(x2d, attn_norm[l][None], *_w_in_layout(w_in[l]), cos_tab, sin_tab, head_gains, seq)
        proj3 = proj.reshape(b, seq, PROJ_W)
        ret_o = _retention(proj3, ret_tabs)
        cmp_in = cmp_grp.reshape(b, ngrp, CMP_STRIDE * 2 * LANES)
        wtop, wbot, pe2, w2bd = _compress_weights(cmp_pe[l], cmp_w1[l], cmp_w2[l])
        kgain = jnp.broadcast_to(jnp.tile(nsa_k_gain[l, 0], 2)[None], (8, LANES))
        kcmp, vcmp = _compress(cmp_in, wtop, wbot, pe2, w2bd, kgain)
        nsa_o = _nsa(proj3, kcmp, vcmp, onehot, overlap, band, d0, d1, tc, expand)
        gm_o = _gmlp(proj3, gm_ws[l], jnp.repeat(gm_b[l].T, GM_DIM, axis=1))
        x2d = _outproj_ffn(x2d, ret_o.reshape(b * seq, RET_W), nsa_o.reshape(b * seq, NSA_W),
                           gm_o.reshape(b * seq, GM_W), _w_out_layout(w_out[l]), ffn_norm[l][None],
                           wgu_all, wd_all, l)
    return x2d.reshape(b, seq, D_MODEL)
```

```python
import functools
import math

import numpy as np
import jax
import jax.numpy as jnp
from jax import lax
from jax.experimental import pallas as pl
from jax.experimental.pallas import tpu as pltpu

F32 = jnp.float32
BF16 = jnp.bfloat16

D_MODEL = 1024
HEAD_DIM = 64
N_RET = 6
N_NSA = 6
N_KV = 2
N_GM = 4
GM_DIM = 64
RET_W = N_RET * HEAD_DIM
NSA_W = N_NSA * HEAD_DIM
GM_W = N_GM * GM_DIM
RET_CHUNK = 128
CMP_LEN = 32
CMP_STRIDE = 16
SEL_BLOCK = 64
N_SEL = 16
WINDOW = 512
GM_CHUNK = 128
N_BUCKETS = 32
MAX_DISTANCE = 128
ROPE_THETA = 10000.0
D_FF = 2816
EPS = 1e-6
BIG = 1e9
NEG = -1e30
LOG2E = math.log2(math.e)

LANES = 128
PROJ_W = 3328
COL_RQ, COL_RK, COL_RV, COL_RG = 0, 3, 6, 9
COL_NQ = 12
COL_CMPK, COL_CMPV, COL_SELK, COL_SELV, COL_WINK, COL_WINV = 15, 16, 17, 18, 19, 20
COL_GATE = 21
COL_GMU, COL_GMV = 22, 24
QB = 256
TK = 256
NQ_SUB = 2
NSA_PHASES = 6
FAR_TILES = 8
INPROJ_SPLIT = 2
RET_STEP = 512
GM_STEP = 512
VMEM_LIMIT = 56 * 1024 * 1024


def _bucket_thresholds():
    n = np.arange(0, 4 * MAX_DISTANCE)
    max_exact = N_BUCKETS // 2
    nf = np.maximum(n, 1).astype(np.float64)
    large = max_exact + np.floor(np.log(nf / max_exact) / math.log(MAX_DISTANCE / max_exact)
                                 * (N_BUCKETS - max_exact)).astype(np.int64)
    bucket = np.where(n < max_exact, n, np.minimum(large, N_BUCKETS - 1))
    return [int(np.min(n[bucket >= b])) for b in range(N_BUCKETS)]


BUCKET_THR = _bucket_thresholds()


def _cparams(sem):
    return pltpu.CompilerParams(dimension_semantics=sem, vmem_limit_bytes=VMEM_LIMIT)


def _head_ones():
    r = lax.broadcasted_iota(jnp.int32, (LANES, LANES), 0) // HEAD_DIM
    c = lax.broadcasted_iota(jnp.int32, (LANES, LANES), 1) // HEAD_DIM
    return jnp.where(r == c, 1.0, 0.0).astype(BF16)


def _head_sum(x, ones_bd):
    return jnp.dot(x.astype(BF16), ones_bd, preferred_element_type=F32)


def _head_rmsnorm(x, ones_bd):
    return x * lax.rsqrt(_head_sum(x * x, ones_bd) * (1.0 / HEAD_DIM) + EPS)


def _dot_nt(a, b):
    return lax.dot_general(a, b, (((1,), (1,)), ((), ())), preferred_element_type=F32)


def _dot_tn(a, b):
    return lax.dot_general(a, b, (((0,), (0,)), ((), ())), preferred_element_type=F32)


def _stack_heads(q, low):
    zero = jnp.zeros_like(q)
    return jnp.concatenate([jnp.where(low, q, zero), jnp.where(low, zero, q)], axis=0)


def _inproj_kernel(x_ref, g_ref, wa_ref, wb_ref, cos_ref, sin_ref, hg_ref, o_ref, grp_ref, cmp_scr, *, tm):
    sub_rows = tm // INPROJ_SPLIT
    lane = lax.broadcasted_iota(jnp.int32, (sub_rows, LANES), 1)
    first_half = (lane % HEAD_DIM) < (HEAD_DIM // 2)
    ones_bd = _head_ones()
    norm_gain = {COL_NQ: 0, COL_NQ + 1: 0, COL_NQ + 2: 0, COL_SELK: 1, COL_WINK: 2}
    deferred = []
    for r0 in range(0, tm, sub_rows):
        rows = slice(r0, r0 + sub_rows)
        x = x_ref[rows, :]
        ms = jnp.mean(x * x, axis=-1, keepdims=True)
        h = (x * lax.rsqrt(ms + EPS) * g_ref[...]).astype(BF16)
        cos = cos_ref[rows, :]
        sin = sin_ref[rows, :]
        for s in range(0, PROJ_W // LANES, 2):
            if s < COL_GMU:
                acc = _dot_nt(h, wa_ref[s * LANES:(s + 2) * LANES, :])
            else:
                acc = _dot_nt(h, wb_ref[(s - COL_GMU) * LANES:(s - COL_GMU + 2) * LANES, :])
            for sub in range(2):
                j = s + sub
                a = acc[:, sub * LANES:(sub + 1) * LANES]
                if j < COL_RV:
                    swapped = jnp.where(first_half, pltpu.roll(a, LANES - HEAD_DIM // 2, axis=1),
                                        pltpu.roll(a, HEAD_DIM // 2, axis=1))
                    a = a * cos + swapped * sin
                    if j >= COL_RK:
                        a = a * (HEAD_DIM ** -0.5)
                elif j in norm_gain:
                    deferred.append((rows, j, a))
                    continue
                elif j in (COL_CMPK, COL_CMPV):
                    cmp_scr[j - COL_CMPK, rows, :] = a
                o_ref[rows, j * LANES:(j + 1) * LANES] = a.astype(BF16)
    for rows, j, a in deferred:
        a = _head_rmsnorm(a, ones_bd) * hg_ref[norm_gain[j]:norm_gain[j] + 1, :]
        o_ref[rows, j * LANES:(j + 1) * LANES] = a.astype(BF16)
    for l in range(CMP_STRIDE):
        for kv in range(2):
            grp_ref[:, (2 * l + kv) * LANES:(2 * l + kv + 1) * LANES] = (
                cmp_scr[kv, pl.ds(l, tm // CMP_STRIDE, stride=CMP_STRIDE), :].astype(BF16))


def _inproj(x2d, gain, w_main, w_gm, cos_tab, sin_tab, head_gains, seq):
    m = x2d.shape[0]
    tm = 1024
    nt = seq // tm
    const = lambda arr: pl.BlockSpec(arr.shape, lambda i: (0, 0), pipeline_mode=pl.Buffered(1))
    return pl.pallas_call(
        functools.partial(_inproj_kernel, tm=tm),
        grid=(m // tm,),
        in_specs=[
            pl.BlockSpec((tm, D_MODEL), lambda i: (i, 0)),
            pl.BlockSpec((1, D_MODEL), lambda i: (0, 0)),
            const(w_main), const(w_gm),
            pl.BlockSpec((tm, LANES), lambda i: (i % nt, 0)),
            pl.BlockSpec((tm, LANES), lambda i: (i % nt, 0)),
            pl.BlockSpec((8, LANES), lambda i: (0, 0)),
        ],
        out_specs=[pl.BlockSpec((tm, PROJ_W), lambda i: (i, 0)),
                   pl.BlockSpec((tm // CMP_STRIDE, CMP_STRIDE * 2 * LANES), lambda i: (i, 0))],
        out_shape=[jax.ShapeDtypeStruct((m, PROJ_W), BF16),
                   jax.ShapeDtypeStruct((m // CMP_STRIDE, CMP_STRIDE * 2 * LANES), BF16)],
        scratch_shapes=[pltpu.VMEM((2, tm, LANES), F32)],
        compiler_params=_cparams(("arbitrary",)),
        name="inproj",
    )(x2d, gain, w_main, w_gm, cos_tab, sin_tab, head_gains)


def _ret_kernel(q_ref, k_ref, v_ref, g_ref, dec_ref, xi_ref, zeta_ref, cd_ref, o_ref, r_ref):
    @pl.when(pl.program_id(1) == 0)
    def _():
        r_ref[...] = jnp.zeros_like(r_ref)

    lane = lax.broadcasted_iota(jnp.int32, (RET_CHUNK, LANES), 1)
    low = lane < HEAD_DIM
    ones_bd = _head_ones()
    for c in range(RET_W // LANES):
        sl = slice(c * LANES, (c + 1) * LANES)
        state = r_ref[c]
        for ch in range(RET_STEP // RET_CHUNK):
            rows = slice(ch * RET_CHUNK, (ch + 1) * RET_CHUNK)
            q = q_ref[rows, sl]
            k = k_ref[rows, sl]
            v = v_ref[rows, sl]
            g = g_ref[rows, sl].astype(F32)
            qs = _stack_heads(q, low)
            scores = _dot_nt(qs, k) * dec_ref[c]
            inner = jnp.dot(scores.astype(BF16), v, preferred_element_type=F32)
            cross = jnp.dot(qs, state.astype(BF16), preferred_element_type=F32) * xi_ref[c]
            tot = inner + cross
            o = jnp.where(low, tot[:RET_CHUNK], tot[RET_CHUNK:])
            kz = (k.astype(F32) * zeta_ref[c]).astype(BF16)
            state = cd_ref[c] * state + _dot_tn(kz, v)
            y = _head_rmsnorm(o, ones_bd) * (g * jax.nn.sigmoid(g))
            o_ref[rows, sl] = y.astype(BF16)
        r_ref[c] = state


def _retention(proj3, tabs):
    b, seq, _ = proj3.shape
    dec, xi, zeta, cd = tabs
    ncol = RET_W // LANES
    qkvg = [pl.BlockSpec((None, RET_STEP, RET_W), functools.partial(lambda bi, ti, col: (bi, ti, col), col=col))
            for col in range(4)]
    const3 = lambda shape: pl.BlockSpec(shape, lambda bi, ti: (0, 0, 0))
    return pl.pallas_call(
        _ret_kernel,
        grid=(b, seq // RET_STEP),
        in_specs=qkvg + [const3((ncol, 2 * RET_CHUNK, LANES)), const3((ncol, 2 * RET_CHUNK, LANES)),
                         const3((ncol, RET_CHUNK, LANES)), const3((ncol, LANES, LANES))],
        out_specs=pl.BlockSpec((None, RET_STEP, RET_W), lambda bi, ti: (bi, ti, 0)),
        out_shape=jax.ShapeDtypeStruct((b, seq, RET_W), BF16),
        scratch_shapes=[pltpu.VMEM((ncol, LANES, LANES), F32)],
        compiler_params=_cparams(("arbitrary", "arbitrary")),
        name="retention",
    )(proj3, proj3, proj3, proj3, dec, xi, zeta, cd)


def _compress_kernel(a_ref, wtop_ref, wbot_ref, pe_ref, w2_ref, kg_ref, kc_ref, vc_ref, *, ngrp):
    half = CMP_LEN // 2
    ones_bd = _head_ones()
    for kv in range(2):
        top = jnp.zeros((ngrp, LANES), F32)
        bot = jnp.zeros((ngrp, LANES), F32)
        for l in range(half):
            x = a_ref[:, l * 2 * LANES + kv * LANES:l * 2 * LANES + (kv + 1) * LANES].astype(F32)
            top = top + jnp.dot((x + pe_ref[kv, l]).astype(BF16), wtop_ref[kv, l], preferred_element_type=F32)
            bot = bot + jnp.dot((x + pe_ref[kv, half + l]).astype(BF16), wbot_ref[kv, l],
                                preferred_element_type=F32)
        hid = jax.nn.gelu(top + pltpu.roll(bot, ngrp - 1, axis=0), approximate=True)
        out = jnp.dot(hid.astype(BF16), w2_ref[kv], preferred_element_type=F32)
        if kv == 0:
            kc_ref[...] = (_head_rmsnorm(out, ones_bd) * kg_ref[0:1, :]).astype(BF16)
        else:
            vc_ref[...] = out.astype(BF16)


def _compress(a, wtop, wbot, pe2, w2bd, kgain):
    b, ngrp, width = a.shape
    full = lambda arr: pl.BlockSpec(arr.shape, lambda bi: (0,) * arr.ndim)
    out_spec = pl.BlockSpec((None, ngrp, LANES), lambda bi: (bi, 0, 0))
    return pl.pallas_call(
        functools.partial(_compress_kernel, ngrp=ngrp),
        grid=(b,),
        in_specs=[pl.BlockSpec((None, ngrp, width), lambda bi: (bi, 0, 0)),
                  full(wtop), full(wbot), full(pe2), full(w2bd), full(kgain)],
        out_specs=[out_spec, out_spec],
        out_shape=[jax.ShapeDtypeStruct((b, ngrp, LANES), BF16)] * 2,
        compiler_params=_cparams(("arbitrary",)),
        name="nsa_compress",
    )(a, wtop, wbot, pe2, w2bd, kgain)


def _bias_kernel(rb_ref, d0_ref, d1_ref, tc_ref):
    h = pl.program_id(0)
    far = rb_ref[N_BUCKETS - 1, h]

    def rel(n):
        val = jnp.full(n.shape, (rb_ref[0, h] - far) * LOG2E, F32)
        for bkt in range(1, N_BUCKETS):
            val = jnp.where(n >= BUCKET_THR[bkt], (rb_ref[bkt, h] - far) * LOG2E, val)
        return val

    ql = lax.broadcasted_iota(jnp.int32, (QB, TK), 0)
    kl = lax.broadcasted_iota(jnp.int32, (QB, TK), 1)
    d = ql - kl
    d0_ref[...] = jnp.where(d >= 0, rel(d), NEG)
    d1_ref[...] = rel(d + TK)
    ql2 = lax.broadcasted_iota(jnp.int32, (QB, LANES), 0)
    m = lax.broadcasted_iota(jnp.int32, (QB, LANES), 1)
    dc = ql2 - CMP_STRIDE * (m - 16) - (CMP_LEN - 1)
    tc_ref[...] = jnp.where(dc >= 0, rel(dc), NEG).astype(BF16)


def _bias_tiles(rel_bias):
    return pl.pallas_call(
        _bias_kernel,
        grid=(N_NSA,),
        in_specs=[pl.BlockSpec(memory_space=pltpu.SMEM)],
        out_specs=[pl.BlockSpec((None, QB, TK), lambda h: (h, 0, 0)),
                   pl.BlockSpec((None, QB, TK), lambda h: (h, 0, 0)),
                   pl.BlockSpec((None, QB, LANES), lambda h: (h, 0, 0))],
        out_shape=[jax.ShapeDtypeStruct((N_NSA, QB, TK), F32),
                   jax.ShapeDtypeStruct((N_NSA, QB, TK), F32),
                   jax.ShapeDtypeStruct((N_NSA, QB, LANES), BF16)],
        compiler_params=_cparams(("arbitrary",)),
        name="t5_bias_tiles",
    )(rel_bias)


def _nsa_kernel(*refs, ncmp):
    programs = [_nsa_tile(sub, *refs, ncmp=ncmp) for sub in range(NQ_SUB)]
    for _ in range(NSA_PHASES):
        for prog in programs:
            next(prog)


def _nsa_tile(sub, q_ref, gt_ref, sk_ref, sv_ref, wk_ref, wv_ref, kc_ref, vc_ref, oh_ref, ov_ref, band_ref,
              d0_ref, d1_ref, tc_ref, e_ref, o_ref, acc_ref, m_ref, *, ncmp):
    i = NQ_SUB * pl.program_id(1) + sub
    t0 = i * QB
    ncol = NSA_W // LANES
    rows = slice(sub * QB, (sub + 1) * QB)
    slot = sub * ncol
    sq = 2 * QB
    low = lax.broadcasted_iota(jnp.int32, (QB, LANES), 1) < HEAD_DIM
    qs = [_stack_heads(q_ref[rows, c * LANES:(c + 1) * LANES], low) for c in range(ncol)]

    def stacked(ref, c):
        return jnp.concatenate([ref[c], ref[c + ncol]], axis=0)

    ones_col = jnp.ones((TK, LANES), BF16)

    n_tiles = NQ_SUB * pl.num_programs(1)
    band = band_ref[pl.ds(pl.multiple_of((QB // CMP_STRIDE) * (n_tiles - i), CMP_STRIDE), ncmp), :]
    kc_aug = jnp.concatenate([kc_ref[...], band], axis=1)
    vc_ov = jnp.concatenate([vc_ref[...], ov_ref[...]], axis=1)
    imp = jnp.zeros((sq, LANES), F32)
    o_cmp = []
    s_cmp = [_dot_nt(jnp.concatenate([qs[c], stacked(tc_ref, c)], axis=1), kc_aug) for c in range(ncol)]

    start_d = pl.multiple_of(t0, TK)
    start_p = pl.multiple_of(jnp.maximum(t0 - TK, 0), TK)
    start_e = pl.multiple_of(jnp.maximum(t0 - 2 * TK, 0), TK)
    s_win = [[_dot_nt(qs[c], wk_ref[pl.ds(st, TK), :]) for st in (start_d, start_p, start_e)] for c in range(ncol)]

    for c in range(ncol):
        s = s_cmp[c]
        mx = jnp.maximum(jnp.max(s, axis=-1, keepdims=True), -1e20)
        e = jnp.exp2(s - mx)
        den = jnp.sum(e, axis=-1, keepdims=True)
        inv = jnp.where(den > 0.0, 1.0 / den, 0.0)
        both = jnp.dot(e.astype(BF16), vc_ov, preferred_element_type=F32) * inv
        imp = imp + both[:, LANES:]
        o_cmp.append(jnp.where(low, both[:QB, :LANES], both[QB:, :LANES]))
    yield

    off_prev = jnp.where(i >= 1, 0.0, -NEG)
    off_edge = jnp.where(i >= 2, 0.0, -NEG)
    ql_t = lax.broadcasted_iota(jnp.int32, (sq, TK), 0) % QB
    kl_t = lax.broadcasted_iota(jnp.int32, (sq, TK), 1)
    edge = jnp.where(kl_t > ql_t, 0.0, NEG)
    wv_all = jnp.concatenate([
        jnp.concatenate([wv_ref[pl.ds(st, TK), :], ones_col], axis=1) for st in (start_d, start_p, start_e)], axis=0)
    o_win = []
    for c in range(ncol):
        s0 = s_win[c][0] + stacked(d0_ref, c)
        s1 = s_win[c][1] + stacked(d1_ref, c)
        s2 = s_win[c][2] + edge
        mx = jnp.maximum(jnp.maximum(jnp.max(s0, axis=-1, keepdims=True),
                                     jnp.max(s1, axis=-1, keepdims=True) - off_prev),
                         jnp.max(s2, axis=-1, keepdims=True) - off_edge)
        p = jnp.exp2(jnp.concatenate([s0 - mx, s1 - (mx + off_prev), s2 - (mx + off_edge)], axis=1)).astype(BF16)
        ow = jnp.dot(p, wv_all, preferred_element_type=F32)
        o_win.append(jnp.where(low, ow[:QB, :LANES] / ow[:QB, LANES:], ow[QB:, :LANES] / ow[QB:, LANES:]))
    yield

    imp_t = imp.T
    blk = lax.broadcasted_iota(jnp.int32, (LANES, sq), 0)
    cur = (t0 + lax.broadcasted_iota(jnp.int32, (LANES, sq), 1) % QB) // SEL_BLOCK
    forced = (blk == 0) | (blk == cur) | (blk == cur - 1)
    imp_t = jnp.where(forced, -jnp.inf, jnp.where(blk > cur, -BIG, imp_t))
    blk_f = blk.astype(F32)

    val = imp_t
    for _ in range(N_SEL - 3):
        top = jnp.max(val, axis=0, keepdims=True)
        idx = jnp.min(jnp.where(val == top, blk_f, 1e6), axis=0, keepdims=True)
        val = jnp.where(blk_f == idx, -jnp.inf, val)
    msel = jnp.where(val == -jnp.inf, 0.0, NEG).T.astype(BF16)
    yield

    qaug = [jnp.concatenate([qs[c], msel], axis=1) for c in range(ncol)]

    def sel_scores(c, start, width):
        kaug = jnp.concatenate([sk_ref[pl.ds(start, width), :], oh_ref[pl.ds(start, width), :]], axis=1)
        return _dot_nt(qaug[c], kaug)

    def sel_vaug(start, width):
        return jnp.concatenate([sv_ref[pl.ds(start, width), :], jnp.ones((width, LANES), BF16)], axis=1)

    def lanes(x, width):
        return jnp.concatenate([x] * (width // LANES), axis=1)

    s_near = [(sel_scores(c, start_d, TK), sel_scores(c, start_p, TK)) for c in range(ncol)]
    for c in range(ncol):
        s0 = s_near[c][0] + stacked(d0_ref, c)
        s1 = s_near[c][1] + stacked(d1_ref, c)
        mx = jnp.maximum(jnp.max(s0, axis=-1, keepdims=True), jnp.max(s1, axis=-1, keepdims=True) - off_prev)
        p0 = jnp.exp2(s0 - mx).astype(BF16)
        p1 = jnp.exp2(s1 - (mx + off_prev)).astype(BF16)
        acc_ref[slot + c] = (jnp.dot(p0, sel_vaug(start_d, TK), preferred_element_type=F32)
                             + jnp.dot(p1, sel_vaug(start_p, TK), preferred_element_type=F32))
        m_ref[slot + c] = jnp.broadcast_to(mx, (sq, LANES))
    yield

    def far_tile(kt0, tiles):
        start = pl.multiple_of(kt0 * TK, tiles * TK)
        width = tiles * TK
        vaug = sel_vaug(start, width)
        scores = [sel_scores(c, start, width) for c in range(ncol)]
        for c in range(ncol):
            s = scores[c]
            m_old = m_ref[slot + c]
            m_new = jnp.maximum(m_old, jnp.max(s, axis=-1, keepdims=True))
            alpha = jnp.exp2(m_old - m_new)
            p = jnp.exp2(s - lanes(m_new, width)).astype(BF16)
            acc_ref[slot + c] = (lanes(alpha, 2 * LANES) * acc_ref[slot + c]
                                 + jnp.dot(p, vaug, preferred_element_type=F32))
            m_ref[slot + c] = m_new

    n_far = jnp.maximum(i - 1, 0)

    def far_loop(tiles):
        def body(kt, carry):
            far_tile(kt * tiles, tiles)
            return carry
        return body

    tiles, done = FAR_TILES, 0
    while tiles >= 1:
        count = n_far // tiles
        lax.fori_loop(done, count, far_loop(tiles), 0)
        tiles, done = tiles // 2, count * 2
    yield

    gate = jax.nn.sigmoid(gt_ref[rows, :].astype(F32)).astype(BF16)
    gexp = jnp.dot(gate, e_ref[...], preferred_element_type=F32)
    for c in range(ncol):
        acc = acc_ref[slot + c]
        o_sel = jnp.where(low, acc[:QB, :LANES] / acc[:QB, LANES:], acc[QB:, :LANES] / acc[QB:, LANES:])
        gc = lambda br: gexp[:, (br * ncol + c) * LANES:(br * ncol + c + 1) * LANES]
        o_ref[rows, c * LANES:(c + 1) * LANES] = (gc(0) * o_cmp[c] + gc(1) * o_sel + gc(2) * o_win[c]).astype(BF16)
    yield


def _nsa(proj3, kcmp, vcmp, onehot, overlap, band, d0, d1, tc, expand):
    b, seq, _ = proj3.shape
    ncmp = kcmp.shape[1]
    step = NQ_SUB * QB
    col = lambda width, cidx: pl.BlockSpec((None, step, width), lambda bi, qi: (bi, qi, cidx))
    res = lambda cidx: pl.BlockSpec((None, seq, LANES), lambda bi, qi: (bi, 0, cidx))
    full = lambda arr: pl.BlockSpec(arr.shape, lambda bi, qi: (0,) * arr.ndim)
    cmp_spec = pl.BlockSpec((None, ncmp, LANES), lambda bi, qi: (bi, 0, 0))
    return pl.pallas_call(
        functools.partial(_nsa_kernel, ncmp=ncmp),
        grid=(b, seq // step),
        in_specs=[col(NSA_W, COL_NQ * LANES // NSA_W), col(LANES, COL_GATE),
                  res(COL_SELK), res(COL_SELV), res(COL_WINK), res(COL_WINV),
                  cmp_spec, cmp_spec, full(onehot), full(overlap), full(band), full(d0), full(d1), full(tc),
                  full(expand)],
        out_specs=pl.BlockSpec((None, step, NSA_W), lambda bi, qi: (bi, qi, 0)),
        out_shape=jax.ShapeDtypeStruct((b, seq, NSA_W), BF16),
        scratch_shapes=[pltpu.VMEM((NQ_SUB * NSA_W // LANES, 2 * QB, 2 * LANES), F32),
                        pltpu.VMEM((NQ_SUB * NSA_W // LANES, 2 * QB, LANES), F32)],
        compiler_params=_cparams(("arbitrary", "arbitrary")),
        name="nsa_attention",
    )(proj3, proj3, proj3, proj3, proj3, proj3, kcmp, vcmp, onehot, overlap, band, d0, d1, tc, expand)


def _gmlp_kernel(u_ref, v_ref, ws_ref, b_ref, o_ref):
    lane = lax.broadcasted_iota(jnp.int32, (GM_CHUNK, LANES), 1)
    low = lane < GM_DIM
    row = lax.broadcasted_iota(jnp.int32, (GM_CHUNK, GM_CHUNK), 0)
    colm = lax.broadcasted_iota(jnp.int32, (GM_CHUNK, GM_CHUNK), 1)
    tril = colm <= row
    ones_bd = _head_ones()
    for cc in range(GM_W // LANES):
        sl = slice(cc * LANES, (cc + 1) * LANES)
        w = jnp.concatenate([jnp.where(tril, ws_ref[2 * cc], 0.0), jnp.where(tril, ws_ref[2 * cc + 1], 0.0)],
                            axis=0).astype(BF16)
        for ch in range(GM_STEP // GM_CHUNK):
            rows = slice(ch * GM_CHUNK, (ch + 1) * GM_CHUNK)
            u = jax.nn.gelu(u_ref[rows, sl].astype(F32), approximate=True)
            v = jax.nn.gelu(v_ref[rows, sl].astype(F32), approximate=True)
            vn = _head_rmsnorm(v, ones_bd).astype(BF16)
            r = jnp.dot(w, vn, preferred_element_type=F32)
            sv = jnp.where(low, r[:GM_CHUNK], r[GM_CHUNK:]) + b_ref[:, sl]
            o_ref[rows, sl] = (u * sv).astype(BF16)


def _gmlp(proj3, ws, bexp):
    b, seq, _ = proj3.shape
    return pl.pallas_call(
        _gmlp_kernel,
        grid=(b, seq // GM_STEP),
        in_specs=[pl.BlockSpec((None, GM_STEP, GM_W), lambda bi, ti: (bi, ti, COL_GMU * LANES // GM_W)),
                  pl.BlockSpec((None, GM_STEP, GM_W), lambda bi, ti: (bi, ti, COL_GMV * LANES // GM_W)),
                  pl.BlockSpec((N_GM, GM_CHUNK, GM_CHUNK), lambda bi, ti: (0, 0, 0)),
                  pl.BlockSpec((GM_CHUNK, GM_W), lambda bi, ti: (0, 0))],
        out_specs=pl.BlockSpec((None, GM_STEP, GM_W), lambda bi, ti: (bi, ti, 0)),
        out_shape=jax.ShapeDtypeStruct((b, seq, GM_W), BF16),
        compiler_params=_cparams(("arbitrary", "arbitrary")),
        name="gmlp",
    )(proj3, proj3, ws, bexp)


FFN_CHUNK = 256
FFN_SPLIT = 2


def _ffn_kernel(x_ref, r_ref, n_ref, m_ref, wo_ref, g_ref, wgu_ref, wd_ref, o_ref, *, tm):
    sub_rows = tm // FFN_SPLIT
    for r0 in range(0, tm, sub_rows):
        rows = slice(r0, r0 + sub_rows)
        mix = jnp.dot(r_ref[rows, :], wo_ref[0:RET_W, :], preferred_element_type=F32)
        mix = mix + jnp.dot(n_ref[rows, :], wo_ref[RET_W:RET_W + NSA_W, :], preferred_element_type=F32)
        mix = mix + jnp.dot(m_ref[rows, :], wo_ref[RET_W + NSA_W:, :], preferred_element_type=F32)
        x = x_ref[rows, :] + mix
        ms = jnp.mean(x * x, axis=-1, keepdims=True)
        h = (x * lax.rsqrt(ms + EPS) * g_ref[...]).astype(BF16)
        acc = x
        for f in range(0, D_FF, FFN_CHUNK):
            gate = jnp.dot(h, wgu_ref[:, f:f + FFN_CHUNK], preferred_element_type=F32)
            up = jnp.dot(h, wgu_ref[:, D_FF + f:D_FF + f + FFN_CHUNK], preferred_element_type=F32)
            act = (gate * jax.nn.sigmoid(gate) * up).astype(BF16)
            acc = acc + jnp.dot(act, wd_ref[f:f + FFN_CHUNK, :], preferred_element_type=F32)
        o_ref[rows, :] = acc


def _outproj_ffn(x2d, ret_o, nsa_o, gm_o, w_out, gain, wgu, wd, layer):
    m = x2d.shape[0]
    tm = 1024
    row = lambda width: pl.BlockSpec((tm, width), lambda i: (i, 0))
    const = lambda arr: pl.BlockSpec(arr.shape, lambda i: (0, 0), pipeline_mode=pl.Buffered(1))
    of_layer = lambda arr: pl.BlockSpec((None,) + arr.shape[1:], lambda i: (layer, 0, 0),
                                        pipeline_mode=pl.Buffered(1))
    return pl.pallas_call(
        functools.partial(_ffn_kernel, tm=tm),
        grid=(m // tm,),
        in_specs=[row(D_MODEL), row(RET_W), row(NSA_W), row(GM_W), const(w_out),
                  pl.BlockSpec((1, D_MODEL), lambda i: (0, 0)),
                  of_layer(wgu), of_layer(wd)],
        out_specs=row(D_MODEL),
        out_shape=jax.ShapeDtypeStruct((m, D_MODEL), F32),
        compiler_params=_cparams(("arbitrary",)),
        name="outproj_ffn",
    )(x2d, ret_o, nsa_o, gm_o, w_out, gain, wgu, wd)


def _nq_head_order():
    return [c + (N_NSA // N_KV) * half for c in range(NSA_W // LANES) for half in range(2)]


def _w_in_layout(w):
    wt = w.T
    off_nq = 4 * RET_W
    off_kv = off_nq + NSA_W
    off_gate = off_kv + 3 * 2 * N_KV * HEAD_DIM
    off_gm = off_gate + 3 * N_NSA
    heads = [wt[off_nq + h * HEAD_DIM:off_nq + (h + 1) * HEAD_DIM] for h in _nq_head_order()]
    main = jnp.concatenate([wt[:off_nq]] + heads + [wt[off_kv:off_gate + LANES]], axis=0).astype(BF16)
    return main, wt[off_gm:].astype(BF16)


def _w_out_layout(w):
    heads = [w[RET_W + h * HEAD_DIM:RET_W + (h + 1) * HEAD_DIM] for h in _nq_head_order()]
    return jnp.concatenate([w[:RET_W]] + heads + [w[RET_W + NSA_W:]], axis=0).astype(BF16)


def _gate_expand():
    e = np.zeros((LANES, 3 * NSA_W), np.float32)
    ncol = NSA_W // LANES
    for br in range(3):
        for c in range(ncol):
            for half in range(2):
                head = c + (N_NSA // N_KV) * half
                dst = (br * ncol + c) * LANES + half * HEAD_DIM
                e[head * 3 + br, dst:dst + HEAD_DIM] = 1.0
    return jnp.asarray(e, BF16)


def _rope_tables(seq):
    half = HEAD_DIM // 2
    inv = ROPE_THETA ** (-jnp.arange(half, dtype=F32) / half)
    ang = jnp.arange(seq, dtype=jnp.int32).astype(F32)[:, None] * inv[None, :]
    cos = jnp.tile(jnp.cos(ang), (1, LANES // half))
    sin = jnp.tile(jnp.concatenate([-jnp.sin(ang), jnp.sin(ang)], axis=1), (1, LANES // HEAD_DIM))
    return cos, sin


def _retention_tables():
    c = RET_CHUNK
    log_gamma = jnp.log(1.0 - 2.0 ** (-5.0 - jnp.arange(N_RET, dtype=F32)))
    idx = jnp.arange(c, dtype=F32)
    diff = idx[:, None] - idx[None, :]
    decay = jnp.where(diff >= 0, jnp.exp(jnp.maximum(diff, 0.0)[None] * log_gamma[:, None, None]), 0.0)
    zeta = jnp.exp((c - 1 - idx)[None, :] * log_gamma[:, None])
    xi = jnp.exp((idx + 1)[None, :] * log_gamma[:, None])
    chunk_decay = jnp.exp(c * log_gamma)
    ncol = RET_W // LANES
    dec = decay.reshape(ncol, 2 * c, c)
    xi_st = jnp.broadcast_to(xi.reshape(ncol, 2 * c, 1), (ncol, 2 * c, LANES))
    zeta_l = jnp.repeat(zeta.reshape(ncol, 2, c).transpose(0, 2, 1), HEAD_DIM, axis=2)
    cd = jnp.broadcast_to(jnp.repeat(chunk_decay.reshape(ncol, 2), HEAD_DIM, axis=1)[:, :, None],
                          (ncol, LANES, LANES))
    return dec, xi_st, zeta_l, cd


def _overlap_table(seq):
    n_cmp = (seq - CMP_LEN) // CMP_STRIDE + 1
    n_slc = seq // SEL_BLOCK
    cs = np.arange(n_cmp)[:, None] * CMP_STRIDE
    ss = np.arange(n_slc)[None, :] * SEL_BLOCK
    ov = np.clip(np.minimum(cs + CMP_LEN, ss + SEL_BLOCK) - np.maximum(cs, ss), 0, None) // CMP_STRIDE
    full = np.zeros((seq // CMP_STRIDE, LANES), np.float32)
    full[:n_cmp, :n_slc] = ov
    return jnp.asarray(full, BF16)


def _cmp_band(seq):
    nq = seq // QB
    ncmp = seq // CMP_STRIDE
    x = np.arange(2 * ncmp)[:, None]
    m = np.arange(LANES)[None, :]
    base = (QB // CMP_STRIDE) * nq - 16
    band = np.where(m < LANES - 1, x == m + base, x >= m + base)
    return jnp.asarray(band.astype(np.float32), BF16)


def _block_onehot(seq):
    oh = (np.arange(seq)[:, None] // SEL_BLOCK) == np.arange(LANES)[None, :]
    return jnp.asarray(oh.astype(np.float32), BF16)


def _compress_weights(cmp_pe, cmp_w1, cmp_w2):
    def block_diag2(w):
        zero = jnp.zeros_like(w)
        return jnp.concatenate([jnp.concatenate([w, zero], axis=-1), jnp.concatenate([zero, w], axis=-1)],
                               axis=-2).astype(BF16)

    w1bd = block_diag2(cmp_w1.reshape(2, CMP_LEN, HEAD_DIM, HEAD_DIM))
    w2bd = block_diag2(cmp_w2)
    pe2 = jnp.tile(cmp_pe, (1, 1, N_KV))[:, :, None, :]
    return w1bd[:, :CMP_LEN // 2], w1bd[:, CMP_LEN // 2:], pe2, w2bd


def kernel(x, attn_norm, w_in, w_out, nsa_q_gain, nsa_k_gain, cmp_pe, cmp_w1, cmp_w2, gm_ws, gm_b, ffn_norm,
           w_gate_up, w_down, rel_bias):
    b, seq, _ = x.shape
    assert seq % (2 * TK) == 0 and N_SEL <= seq // SEL_BLOCK <= LANES, "selection blocks must fit one lane row"
    depth = w_in.shape[0]
    cos_tab, sin_tab = _rope_tables(seq)
    ret_tabs = _retention_tables()
    overlap = _overlap_table(seq)
    onehot = _block_onehot(seq)
    band = _cmp_band(seq)
    expand = _gate_expand()
    d0, d1, tc = _bias_tiles(rel_bias)
    ngrp = seq // CMP_STRIDE
    wgu_all = w_gate_up.astype(BF16)
    wd_all = w_down.astype(BF16)
    x2d = x.reshape(b * seq, D_MODEL)
    for l in range(depth):
        zero_row = jnp.zeros((5, LANES), F32)
        head_gains = jnp.concatenate([jnp.tile(nsa_q_gain[l] * (HEAD_DIM ** -0.5 * LOG2E), 2)[None],
                                      jnp.tile(nsa_k_gain[l, 1], 2)[None],
                                      jnp.tile(nsa_k_gain[l, 2], 2)[None], zero_row], axis=0)
        proj, cmp_grp = _inproj(x2d, attn_norm[l][None], *_w_in_layout(w_in[l]), cos_tab, sin_tab, head_gains, seq)
        proj3 = proj.reshape(b, seq, PROJ_W)
        ret_o = _retention(proj3, ret_tabs)
        cmp_in = cmp_grp.reshape(b, ngrp, CMP_STRIDE * 2 * LANES)
        wtop, wbot, pe2, w2bd = _compress_weights(cmp_pe[l], cmp_w1[l], cmp_w2[l])
        kgain = jnp.broadcast_to(jnp.tile(nsa_k_gain[l, 0], 2)[None], (8, LANES))
        kcmp, vcmp = _compress(cmp_in, wtop, wbot, pe2, w2bd, kgain)
        nsa_o = _nsa(proj3, kcmp, vcmp, onehot, overlap, band, d0, d1, tc, expand)
        gm_o = _gmlp(proj3, gm_ws[l], jnp.repeat(gm_b[l].T, GM_DIM, axis=1))
        x2d = _outproj_ffn(x2d, ret_o.reshape(b * seq, RET_W), nsa_o.reshape(b * seq, NSA_W),
                           gm_o.reshape(b * seq, GM_W), _w_out_layout(w_out[l]), ffn_norm[l][None],
                           wgu_all, wd_all, l)
    return x2d.reshape(b, seq, D_MODEL)
```

```python
import functools
import math

import numpy as np
import jax
import jax.numpy as jnp
from jax import lax
from jax.experimental import pallas as pl
from jax.experimental.pallas import tpu as pltpu

F32 = jnp.float32
BF16 = jnp.bfloat16

D_MODEL = 1024
HEAD_DIM = 64
N_RET = 6
N_NSA = 6
N_KV = 2
N_GM = 4
GM_DIM = 64
RET_W = N_RET * HEAD_DIM
NSA_W = N_NSA * HEAD_DIM
GM_W = N_GM * GM_DIM
RET_CHUNK = 128
CMP_LEN = 32
CMP_STRIDE = 16
SEL_BLOCK = 64
N_SEL = 16
WINDOW = 512
GM_CHUNK = 128
N_BUCKETS = 32
MAX_DISTANCE = 128
ROPE_THETA = 10000.0
D_FF = 2816
EPS = 1e-6
BIG = 1e9
NEG = -1e30
LOG2E = math.log2(math.e)

LANES = 128
PROJ_W = 3328
COL_RQ, COL_RK, COL_RV, COL_RG = 0, 3, 6, 9
COL_NQ = 12
COL_CMPK, COL_CMPV, COL_SELK, COL_SELV, COL_WINK, COL_WINV = 15, 16, 17, 18, 19, 20
COL_GATE = 21
COL_GMU, COL_GMV = 22, 24
QB = 256
TK = 256
CMP_LEAD = 16
NQ_SUB = 2
NSA_PHASES = 6
FAR_TILES = 8
INPROJ_SPLIT = 2
RET_STEP = 512
GM_STEP = 512
VMEM_LIMIT = 56 * 1024 * 1024


def _bucket_thresholds():
    n = np.arange(0, 4 * MAX_DISTANCE)
    max_exact = N_BUCKETS // 2
    nf = np.maximum(n, 1).astype(np.float64)
    large = max_exact + np.floor(np.log(nf / max_exact) / math.log(MAX_DISTANCE / max_exact)
                                 * (N_BUCKETS - max_exact)).astype(np.int64)
    bucket = np.where(n < max_exact, n, np.minimum(large, N_BUCKETS - 1))
    return [int(np.min(n[bucket >= b])) for b in range(N_BUCKETS)]


BUCKET_THR = _bucket_thresholds()


def _cparams(sem):
    return pltpu.CompilerParams(dimension_semantics=sem, vmem_limit_bytes=VMEM_LIMIT)


def _head_ones():
    r = lax.broadcasted_iota(jnp.int32, (LANES, LANES), 0) // HEAD_DIM
    c = lax.broadcasted_iota(jnp.int32, (LANES, LANES), 1) // HEAD_DIM
    return jnp.where(r == c, 1.0, 0.0).astype(BF16)


def _head_sum(x, ones_bd):
    return jnp.dot(x.astype(BF16), ones_bd, preferred_element_type=F32)


def _head_rmsnorm(x, ones_bd):
    return x * lax.rsqrt(_head_sum(x * x, ones_bd) * (1.0 / HEAD_DIM) + EPS)


def _dot_nt(a, b):
    return lax.dot_general(a, b, (((1,), (1,)), ((), ())), preferred_element_type=F32)


def _dot_tn(a, b):
    return lax.dot_general(a, b, (((0,), (0,)), ((), ())), preferred_element_type=F32)


def _stack_heads(q, low):
    zero = jnp.zeros_like(q)
    return jnp.concatenate([jnp.where(low, q, zero), jnp.where(low, zero, q)], axis=0)


def _inproj_kernel(x_ref, g_ref, wa_ref, wb_ref, cos_ref, sin_ref, hg_ref, o_ref, grp_ref, cmp_scr, *, tm):
    sub_rows = tm // INPROJ_SPLIT
    lane = lax.broadcasted_iota(jnp.int32, (sub_rows, LANES), 1)
    first_half = (lane % HEAD_DIM) < (HEAD_DIM // 2)
    ones_bd = _head_ones()
    norm_gain = {COL_NQ: 0, COL_NQ + 1: 0, COL_NQ + 2: 0, COL_SELK: 1, COL_WINK: 2}
    deferred = []
    for r0 in range(0, tm, sub_rows):
        rows = slice(r0, r0 + sub_rows)
        x = x_ref[rows, :]
        ms = jnp.mean(x * x, axis=-1, keepdims=True)
        h = (x * lax.rsqrt(ms + EPS) * g_ref[...]).astype(BF16)
        cos = cos_ref[rows, :]
        sin = sin_ref[rows, :]
        for s in range(0, PROJ_W // LANES, 2):
            if s < COL_GMU:
                acc = _dot_nt(h, wa_ref[s * LANES:(s + 2) * LANES, :])
            else:
                acc = _dot_nt(h, wb_ref[(s - COL_GMU) * LANES:(s - COL_GMU + 2) * LANES, :])
            for sub in range(2):
                j = s + sub
                a = acc[:, sub * LANES:(sub + 1) * LANES]
                if j < COL_RV:
                    swapped = jnp.where(first_half, pltpu.roll(a, LANES - HEAD_DIM // 2, axis=1),
                                        pltpu.roll(a, HEAD_DIM // 2, axis=1))
                    a = a * cos + swapped * sin
                    if j >= COL_RK:
                        a = a * (HEAD_DIM ** -0.5)
                elif j in norm_gain:
                    deferred.append((rows, j, a))
                    continue
                elif j in (COL_CMPK, COL_CMPV):
                    cmp_scr[j - COL_CMPK, rows, :] = a
                o_ref[rows, j * LANES:(j + 1) * LANES] = a.astype(BF16)
    for rows, j, a in deferred:
        a = _head_rmsnorm(a, ones_bd) * hg_ref[norm_gain[j]:norm_gain[j] + 1, :]
        o_ref[rows, j * LANES:(j + 1) * LANES] = a.astype(BF16)
    for l in range(CMP_STRIDE):
        for kv in range(2):
            grp_ref[:, (2 * l + kv) * LANES:(2 * l + kv + 1) * LANES] = (
                cmp_scr[kv, pl.ds(l, tm // CMP_STRIDE, stride=CMP_STRIDE), :].astype(BF16))


def _inproj(x2d, gain, w_main, w_gm, cos_tab, sin_tab, head_gains, seq):
    m = x2d.shape[0]
    tm = 1024
    nt = seq // tm
    const = lambda arr: pl.BlockSpec(arr.shape, lambda i: (0, 0), pipeline_mode=pl.Buffered(1))
    return pl.pallas_call(
        functools.partial(_inproj_kernel, tm=tm),
        grid=(m // tm,),
        in_specs=[
            pl.BlockSpec((tm, D_MODEL), lambda i: (i, 0)),
            pl.BlockSpec((1, D_MODEL), lambda i: (0, 0)),
            const(w_main), const(w_gm),
            pl.BlockSpec((tm, LANES), lambda i: (i % nt, 0)),
            pl.BlockSpec((tm, LANES), lambda i: (i % nt, 0)),
            pl.BlockSpec((8, LANES), lambda i: (0, 0)),
        ],
        out_specs=[pl.BlockSpec((tm, PROJ_W), lambda i: (i, 0)),
                   pl.BlockSpec((tm // CMP_STRIDE, CMP_STRIDE * 2 * LANES), lambda i: (i, 0))],
        out_shape=[jax.ShapeDtypeStruct((m, PROJ_W), BF16),
                   jax.ShapeDtypeStruct((m // CMP_STRIDE, CMP_STRIDE * 2 * LANES), BF16)],
        scratch_shapes=[pltpu.VMEM((2, tm, LANES), F32)],
        compiler_params=_cparams(("arbitrary",)),
        name="inproj",
    )(x2d, gain, w_main, w_gm, cos_tab, sin_tab, head_gains)


def _ret_kernel(q_ref, k_ref, v_ref, g_ref, dec_ref, xi_ref, zeta_ref, cd_ref, o_ref, r_ref):
    @pl.when(pl.program_id(1) == 0)
    def _():
        r_ref[...] = jnp.zeros_like(r_ref)

    lane = lax.broadcasted_iota(jnp.int32, (RET_CHUNK, LANES), 1)
    low = lane < HEAD_DIM
    ones_bd = _head_ones()
    cells = [(c, ch) for c in range(RET_W // LANES) for ch in range(RET_STEP // RET_CHUNK)]
    sl = lambda c: slice(c * LANES, (c + 1) * LANES)
    rows = lambda ch: slice(ch * RET_CHUNK, (ch + 1) * RET_CHUNK)
    qs = {(c, ch): _stack_heads(q_ref[rows(ch), sl(c)], low) for c, ch in cells}
    scores = {(c, ch): _dot_nt(qs[c, ch], k_ref[rows(ch), sl(c)]) for c, ch in cells}
    incr = {(c, ch): _dot_tn((k_ref[rows(ch), sl(c)].astype(F32) * zeta_ref[c]).astype(BF16), v_ref[rows(ch), sl(c)])
            for c, ch in cells}
    inner = {(c, ch): jnp.dot((scores[c, ch] * dec_ref[c]).astype(BF16), v_ref[rows(ch), sl(c)],
                              preferred_element_type=F32) for c, ch in cells}
    outs = {}
    for c in range(RET_W // LANES):
        state = r_ref[c]
        for ch in range(RET_STEP // RET_CHUNK):
            cross = jnp.dot(qs[c, ch], state.astype(BF16), preferred_element_type=F32) * xi_ref[c]
            tot = inner[c, ch] + cross
            outs[c, ch] = jnp.where(low, tot[:RET_CHUNK], tot[RET_CHUNK:])
            state = cd_ref[c] * state + incr[c, ch]
        r_ref[c] = state
    for c, ch in cells:
        g = g_ref[rows(ch), sl(c)].astype(F32)
        y = _head_rmsnorm(outs[c, ch], ones_bd) * (g * jax.nn.sigmoid(g))
        o_ref[rows(ch), sl(c)] = y.astype(BF16)


def _retention(proj3, tabs):
    b, seq, _ = proj3.shape
    dec, xi, zeta, cd = tabs
    ncol = RET_W // LANES
    qkvg = [pl.BlockSpec((None, RET_STEP, RET_W), functools.partial(lambda bi, ti, col: (bi, ti, col), col=col))
            for col in range(4)]
    const3 = lambda shape: pl.BlockSpec(shape, lambda bi, ti: (0, 0, 0))
    return pl.pallas_call(
        _ret_kernel,
        grid=(b, seq // RET_STEP),
        in_specs=qkvg + [const3((ncol, 2 * RET_CHUNK, LANES)), const3((ncol, 2 * RET_CHUNK, LANES)),
                         const3((ncol, RET_CHUNK, LANES)), const3((ncol, LANES, LANES))],
        out_specs=pl.BlockSpec((None, RET_STEP, RET_W), lambda bi, ti: (bi, ti, 0)),
        out_shape=jax.ShapeDtypeStruct((b, seq, RET_W), BF16),
        scratch_shapes=[pltpu.VMEM((ncol, LANES, LANES), F32)],
        compiler_params=_cparams(("arbitrary", "arbitrary")),
        name="retention",
    )(proj3, proj3, proj3, proj3, dec, xi, zeta, cd)


def _compress_kernel(a_ref, wtop_ref, wbot_ref, pe_ref, w2_ref, kg_ref, kc_ref, vc_ref, *, ngrp):
    half = CMP_LEN // 2
    ones_bd = _head_ones()
    for kv in range(2):
        top = jnp.zeros((ngrp, LANES), F32)
        bot = jnp.zeros((ngrp, LANES), F32)
        for l in range(half):
            x = a_ref[:, l * 2 * LANES + kv * LANES:l * 2 * LANES + (kv + 1) * LANES].astype(F32)
            top = top + jnp.dot((x + pe_ref[kv, l]).astype(BF16), wtop_ref[kv, l], preferred_element_type=F32)
            bot = bot + jnp.dot((x + pe_ref[kv, half + l]).astype(BF16), wbot_ref[kv, l],
                                preferred_element_type=F32)
        hid = jax.nn.gelu(top + pltpu.roll(bot, ngrp - 1, axis=0), approximate=True)
        out = jnp.dot(hid.astype(BF16), w2_ref[kv], preferred_element_type=F32)
        if kv == 0:
            kc_ref[...] = (_head_rmsnorm(out, ones_bd) * kg_ref[0:1, :]).astype(BF16)
        else:
            vc_ref[...] = out.astype(BF16)


def _compress(a, wtop, wbot, pe2, w2bd, kgain):
    b, ngrp, width = a.shape
    full = lambda arr: pl.BlockSpec(arr.shape, lambda bi: (0,) * arr.ndim)
    out_spec = pl.BlockSpec((None, ngrp, LANES), lambda bi: (bi, 0, 0))
    return pl.pallas_call(
        functools.partial(_compress_kernel, ngrp=ngrp),
        grid=(b,),
        in_specs=[pl.BlockSpec((None, ngrp, width), lambda bi: (bi, 0, 0)),
                  full(wtop), full(wbot), full(pe2), full(w2bd), full(kgain)],
        out_specs=[out_spec, out_spec],
        out_shape=[jax.ShapeDtypeStruct((b, ngrp, LANES), BF16)] * 2,
        compiler_params=_cparams(("arbitrary",)),
        name="nsa_compress",
    )(a, wtop, wbot, pe2, w2bd, kgain)


def _bias_kernel(rb_ref, d0_ref, d1_ref, tc_ref):
    h = pl.program_id(0)
    far = rb_ref[N_BUCKETS - 1, h]

    def rel(n):
        val = jnp.full(n.shape, (rb_ref[0, h] - far) * LOG2E, F32)
        for bkt in range(1, N_BUCKETS):
            val = jnp.where(n >= BUCKET_THR[bkt], (rb_ref[bkt, h] - far) * LOG2E, val)
        return val

    ql = lax.broadcasted_iota(jnp.int32, (QB, TK), 0)
    kl = lax.broadcasted_iota(jnp.int32, (QB, TK), 1)
    d = ql - kl
    d0_ref[...] = jnp.where(d >= 0, rel(d), NEG)
    d1_ref[...] = rel(d + TK)
    ql2 = lax.broadcasted_iota(jnp.int32, (QB, LANES), 0)
    m = lax.broadcasted_iota(jnp.int32, (QB, LANES), 1)
    dc = ql2 - CMP_STRIDE * (m - CMP_LEAD) - (CMP_LEN - 1)
    tc_ref[...] = jnp.where(dc >= 0, rel(dc), NEG).astype(BF16)


def _bias_tiles(rel_bias):
    return pl.pallas_call(
        _bias_kernel,
        grid=(N_NSA,),
        in_specs=[pl.BlockSpec(memory_space=pltpu.SMEM)],
        out_specs=[pl.BlockSpec((None, QB, TK), lambda h: (h, 0, 0)),
                   pl.BlockSpec((None, QB, TK), lambda h: (h, 0, 0)),
                   pl.BlockSpec((None, QB, LANES), lambda h: (h, 0, 0))],
        out_shape=[jax.ShapeDtypeStruct((N_NSA, QB, TK), F32),
                   jax.ShapeDtypeStruct((N_NSA, QB, TK), F32),
                   jax.ShapeDtypeStruct((N_NSA, QB, LANES), BF16)],
        compiler_params=_cparams(("arbitrary",)),
        name="t5_bias_tiles",
    )(rel_bias)


def _nsa_kernel(*refs, ncmp):
    programs = [_nsa_tile(sub, *refs, ncmp=ncmp) for sub in range(NQ_SUB)]
    for _ in range(NSA_PHASES):
        for prog in programs:
            next(prog)


def _nsa_tile(sub, q_ref, gt_ref, sk_ref, sv_ref, wk_ref, wv_ref, kc_ref, vc_ref, oh_ref, ov_ref, band_ref,
              d0_ref, d1_ref, tc_ref, e_ref, o_ref, acc_ref, m_ref, *, ncmp):
    i = NQ_SUB * pl.program_id(1) + sub
    t0 = i * QB
    ncol = NSA_W // LANES
    rows = slice(sub * QB, (sub + 1) * QB)
    slot = sub * ncol
    sq = 2 * QB
    low = lax.broadcasted_iota(jnp.int32, (QB, LANES), 1) < HEAD_DIM
    qs = [_stack_heads(q_ref[rows, c * LANES:(c + 1) * LANES], low) for c in range(ncol)]

    def stacked(ref, c):
        return jnp.concatenate([ref[c], ref[c + ncol]], axis=0)

    ones_col = jnp.ones((TK, LANES), BF16)

    n_tiles = NQ_SUB * pl.num_programs(1)
    band = band_ref[pl.ds(pl.multiple_of((QB // CMP_STRIDE) * (n_tiles - i), CMP_STRIDE), ncmp), :]
    kc_aug = jnp.concatenate([kc_ref[...], band], axis=1)
    vc_ov = jnp.concatenate([vc_ref[...], ov_ref[...]], axis=1)
    imp = jnp.zeros((sq, LANES), F32)
    o_cmp = []
    s_cmp =[_dot_nt(jnp.concatenate([qs[c], stacked(tc_ref, c)], axis=1), kc_aug) for c in range(ncol)]

    start_d = pl.multiple_of(t0, TK)
    start_p = pl.multiple_of(jnp.maximum(t0 - TK, 0), TK)
    start_e = pl.multiple_of(jnp.maximum(t0 - 2 * TK, 0), TK)
    s_win = [[_dot_nt(qs[c], wk_ref[pl.ds(st, TK), :]) for st in (start_d, start_p, start_e)] for c in range(ncol)]

    for c in range(ncol):
        s = s_cmp[c]
        mx = jnp.maximum(jnp.max(s, axis=-1, keepdims=True), -1e20)
        e = jnp.exp2(s - mx)
        den = jnp.sum(e, axis=-1, keepdims=True)
        inv = jnp.where(den > 0.0, 1.0 / den, 0.0)
        both = jnp.dot(e.astype(BF16), vc_ov, preferred_element_type=F32) * inv
        imp = imp + both[:, LANES:]
        o_cmp.append(jnp.where(low, both[:QB, :LANES], both[QB:, :LANES]))
    yield

    off_prev = jnp.where(i >= 1, 0.0, -NEG)
    off_edge = jnp.where(i >= 2, 0.0, -NEG)
    ql_t = lax.broadcasted_iota(jnp.int32, (sq, TK), 0) % QB
    kl_t = lax.broadcasted_iota(jnp.int32, (sq, TK), 1)
    edge = jnp.where(kl_t > ql_t, 0.0, NEG)
    wv_all = jnp.concatenate([
        jnp.concatenate([wv_ref[pl.ds(st, TK), :], ones_col], axis=1) for st in (start_d, start_p, start_e)], axis=0)
    o_win = []
    for c in range(ncol):
        s0 = s_win[c][0] + stacked(d0_ref, c)
        s1 = s_win[c][1] + stacked(d1_ref, c)
        s2 = s_win[c][2] + edge
        mx = jnp.maximum(jnp.maximum(jnp.max(s0, axis=-1, keepdims=True),
                                     jnp.max(s1, axis=-1, keepdims=True) - off_prev),
                         jnp.max(s2, axis=-1, keepdims=True) - off_edge)
        p = jnp.exp2(jnp.concatenate([s0 - mx, s1 - (mx + off_prev), s2 - (mx + off_edge)], axis=1)).astype(BF16)
        ow = jnp.dot(p, wv_all, preferred_element_type=F32)
        o_win.append(jnp.where(low, ow[:QB, :LANES] / ow[:QB, LANES:], ow[QB:, :LANES] / ow[QB:, LANES:]))
    yield

    imp_t = imp.T
    blk = lax.broadcasted_iota(jnp.int32, (LANES, sq), 0)
    cur = (t0 + lax.broadcasted_iota(jnp.int32, (LANES, sq), 1) % QB) // SEL_BLOCK
    forced = (blk == 0) | (blk == cur) | (blk == cur - 1)
    imp_t = jnp.where(forced, -jnp.inf, jnp.where(blk > cur, -BIG, imp_t))
    blk_f = blk.astype(F32)

    val = imp_t
    for _ in range(N_SEL - 3):
        top = jnp.max(val, axis=0, keepdims=True)
        idx = jnp.min(jnp.where(val == top, blk_f, 1e6), axis=0, keepdims=True)
        val = jnp.where(blk_f == idx, -jnp.inf, val)
    msel = jnp.where(val == -jnp.inf, 0.0, NEG).T.astype(BF16)
    yield

    qaug = [jnp.concatenate([qs[c], msel], axis=1) for c in range(ncol)]

    def sel_scores(c, start, width):
        kaug = jnp.concatenate([sk_ref[pl.ds(start, width), :], oh_ref[pl.ds(start, width), :]], axis=1)
        return _dot_nt(qaug[c], kaug)

    def sel_vaug(start, width):
        return jnp.concatenate([sv_ref[pl.ds(start, width), :], jnp.ones((width, LANES), BF16)], axis=1)

    def lanes(x, width):
        return jnp.concatenate([x] * (width // LANES), axis=1)

    s_near = [(sel_scores(c, start_d, TK), sel_scores(c, start_p, TK)) for c in range(ncol)]
    for c in range(ncol):
        s0 = s_near[c][0] + stacked(d0_ref, c)
        s1 = s_near[c][1] + stacked(d1_ref, c)
        mx = jnp.maximum(jnp.max(s0, axis=-1, keepdims=True), jnp.max(s1, axis=-1, keepdims=True) - off_prev)
        p0 = jnp.exp2(s0 - mx).astype(BF16)
        p1 = jnp.exp2(s1 - (mx + off_prev)).astype(BF16)
        acc_ref[slot + c] = (jnp.dot(p0, sel_vaug(start_d, TK), preferred_element_type=F32)
                             + jnp.dot(p1, sel_vaug(start_p, TK), preferred_element_type=F32))
        m_ref[slot + c] = jnp.broadcast_to(mx, (sq, LANES))
    yield

    def far_tile(kt0, tiles):
        start = pl.multiple_of(kt0 * TK, tiles * TK)
        width = tiles * TK
        vaug = sel_vaug(start, width)
        scores = [sel_scores(c, start, width) for c in range(ncol)]
        for c in range(ncol):
            s = scores[c]
            m_old = m_ref[slot + c]
            m_new = jnp.maximum(m_old, jnp.max(s, axis=-1, keepdims=True))
            alpha = jnp.exp2(m_old - m_new)
            p = jnp.exp2(s - lanes(m_new, width)).astype(BF16)
            acc_ref[slot + c] = (lanes(alpha, 2 * LANES) * acc_ref[slot + c]
                                 + jnp.dot(p, vaug, preferred_element_type=F32))
            m_ref[slot + c] = m_new

    n_far = jnp.maximum(i - 1, 0)

    def far_loop(tiles):
        def body(kt, carry):
            far_tile(kt * tiles, tiles)
            return carry
        return body

    tiles, done = FAR_TILES, 0
    while tiles >= 1:
        count = n_far // tiles
        lax.fori_loop(done, count, far_loop(tiles), 0)
        tiles, done = tiles // 2, count * 2
    yield

    gate = jax.nn.sigmoid(gt_ref[rows, :].astype(F32)).astype(BF16)
    gexp = jnp.dot(gate, e_ref[...], preferred_element_type=F32)
    for c in range(ncol):
        acc = acc_ref[slot + c]
        o_sel = jnp.where(low, acc[:QB, :LANES] / acc[:QB, LANES:], acc[QB:, :LANES] / acc[QB:, LANES:])
        gc = lambda br: gexp[:, (br * ncol + c) * LANES:(br * ncol + c + 1) * LANES]
        o_ref[rows, c * LANES:(c + 1) * LANES] = (gc(0) * o_cmp[c] + gc(1) * o_sel + gc(2) * o_win[c]).astype(BF16)
    yield


def _nsa(proj3, kcmp, vcmp, onehot, overlap, band, d0, d1, tc, expand):
    b, seq, _ = proj3.shape
    ncmp = kcmp.shape[1]
    step = NQ_SUB * QB
    col = lambda width, cidx: pl.BlockSpec((None, step, width), lambda bi, qi: (bi, qi, cidx))
    res = lambda cidx: pl.BlockSpec((None, seq, LANES), lambda bi, qi: (bi, 0, cidx))
    full = lambda arr: pl.BlockSpec(arr.shape, lambda bi, qi: (0,) * arr.ndim)
    cmp_spec = pl.BlockSpec((None, ncmp, LANES), lambda bi, qi: (bi, 0, 0))
    return pl.pallas_call(
        functools.partial(_nsa_kernel, ncmp=ncmp),
        grid=(b, seq // step),
        in_specs=[col(NSA_W, COL_NQ * LANES // NSA_W), col(LANES, COL_GATE),
                  res(COL_SELK), res(COL_SELV), res(COL_WINK), res(COL_WINV),
                  cmp_spec, cmp_spec, full(onehot), full(overlap), full(band), full(d0), full(d1), full(tc),
                  full(expand)],
        out_specs=pl.BlockSpec((None, step, NSA_W), lambda bi, qi: (bi, qi, 0)),
        out_shape=jax.ShapeDtypeStruct((b, seq, NSA_W), BF16),
        scratch_shapes=[pltpu.VMEM((NQ_SUB * NSA_W // LANES, 2 * QB, 2 * LANES), F32),
                        pltpu.VMEM((NQ_SUB * NSA_W // LANES, 2 * QB, LANES), F32)],
        compiler_params=_cparams(("arbitrary", "arbitrary")),
        name="nsa_attention",
    )(proj3, proj3, proj3, proj3, proj3, proj3, kcmp, vcmp, onehot, overlap, band, d0, d1, tc, expand)


def _gmlp_kernel(u_ref, v_ref, ws_ref, b_ref, o_ref):
    lane = lax.broadcasted_iota(jnp.int32, (GM_CHUNK, LANES), 1)
    low = lane < GM_DIM
    row = lax.broadcasted_iota(jnp.int32, (GM_CHUNK, GM_CHUNK), 0)
    colm = lax.broadcasted_iota(jnp.int32, (GM_CHUNK, GM_CHUNK), 1)
    tril = colm <= row
    ones_bd = _head_ones()
    cells = [(cc, ch) for cc in range(GM_W // LANES) for ch in range(GM_STEP // GM_CHUNK)]
    sl = lambda cc: slice(cc * LANES, (cc + 1) * LANES)
    rows = lambda ch: slice(ch * GM_CHUNK, (ch + 1) * GM_CHUNK)
    w = [jnp.concatenate([jnp.where(tril, ws_ref[2 * cc], 0.0), jnp.where(tril, ws_ref[2 * cc + 1], 0.0)],
                         axis=0).astype(BF16) for cc in range(GM_W // LANES)]
    v = {(cc, ch): jax.nn.gelu(v_ref[rows(ch), sl(cc)].astype(F32), approximate=True) for cc, ch in cells}
    sums = {cell: _head_sum(v[cell] * v[cell], ones_bd) for cell in cells}
    vn = {cell: (v[cell] * lax.rsqrt(sums[cell] * (1.0 / GM_DIM) + EPS)).astype(BF16) for cell in cells}
    mixed = {(cc, ch): jnp.dot(w[cc], vn[cc, ch], preferred_element_type=F32) for cc, ch in cells}
    for cc, ch in cells:
        r = mixed[cc, ch]
        sv = jnp.where(low, r[:GM_CHUNK], r[GM_CHUNK:]) + b_ref[:, sl(cc)]
        u = jax.nn.gelu(u_ref[rows(ch), sl(cc)].astype(F32), approximate=True)
        o_ref[rows(ch), sl(cc)] = (u * sv).astype(BF16)


def _gmlp(proj3, ws, bexp):
    b, seq, _ = proj3.shape
    return pl.pallas_call(
        _gmlp_kernel,
        grid=(b, seq // GM_STEP),
        in_specs=[pl.BlockSpec((None, GM_STEP, GM_W), lambda bi, ti: (bi, ti, COL_GMU * LANES // GM_W)),
                  pl.BlockSpec((None, GM_STEP, GM_W), lambda bi, ti: (bi, ti, COL_GMV * LANES // GM_W)),
                  pl.BlockSpec((N_GM, GM_CHUNK, GM_CHUNK), lambda bi, ti: (0, 0, 0)),
                  pl.BlockSpec((GM_CHUNK, GM_W), lambda bi, ti: (0, 0))],
        out_specs=pl.BlockSpec((None, GM_STEP, GM_W), lambda bi, ti: (bi, ti, 0)),
        out_shape=jax.ShapeDtypeStruct((b, seq, GM_W), BF16),
        compiler_params=_cparams(("arbitrary", "arbitrary")),
        name="gmlp",
    )(proj3, proj3, ws, bexp)


FFN_CHUNK = 256
FFN_SPLIT = 2


def _ffn_kernel(x_ref, r_ref, n_ref, m_ref, wo_ref, g_ref, wgu_ref, wd_ref, o_ref, *, tm):
    sub_rows = tm // FFN_SPLIT
    for r0 in range(0, tm, sub_rows):
        rows = slice(r0, r0 + sub_rows)
        mix = jnp.dot(r_ref[rows, :], wo_ref[0:RET_W, :], preferred_element_type=F32)
        mix = mix + jnp.dot(n_ref[rows, :], wo_ref[RET_W:RET_W + NSA_W, :], preferred_element_type=F32)
        mix = mix + jnp.dot(m_ref[rows, :], wo_ref[RET_W + NSA_W:, :], preferred_element_type=F32)
        x = x_ref[rows, :] + mix
        ms = jnp.mean(x * x, axis=-1, keepdims=True)
        h = (x * lax.rsqrt(ms + EPS) * g_ref[...]).astype(BF16)
        acc = x
        for f in range(0, D_FF, FFN_CHUNK):
            gate = jnp.dot(h, wgu_ref[:, f:f + FFN_CHUNK], preferred_element_type=F32)
            up = jnp.dot(h, wgu_ref[:, D_FF + f:D_FF + f + FFN_CHUNK], preferred_element_type=F32)
            act = (gate * jax.nn.sigmoid(gate) * up).astype(BF16)
            acc = acc + jnp.dot(act, wd_ref[f:f + FFN_CHUNK, :], preferred_element_type=F32)
        o_ref[rows, :] = acc


def _outproj_ffn(x2d, ret_o, nsa_o, gm_o, w_out, gain, wgu, wd, layer):
    m = x2d.shape[0]
    tm = 1024
    row = lambda width: pl.BlockSpec((tm, width), lambda i: (i, 0))
    const = lambda arr: pl.BlockSpec(arr.shape, lambda i: (0, 0), pipeline_mode=pl.Buffered(1))
    of_layer = lambda arr: pl.BlockSpec((None,) + arr.shape[1:], lambda i: (layer, 0, 0),
                                        pipeline_mode=pl.Buffered(1))
    return pl.pallas_call(
        functools.partial(_ffn_kernel, tm=tm),
        grid=(m // tm,),
        in_specs=[row(D_MODEL), row(RET_W), row(NSA_W), row(GM_W), const(w_out),
                  pl.BlockSpec((1, D_MODEL), lambda i: (0, 0)),
                  of_layer(wgu), of_layer(wd)],
        out_specs=row(D_MODEL),
        out_shape=jax.ShapeDtypeStruct((m, D_MODEL), F32),
        compiler_params=_cparams(("arbitrary",)),
        name="outproj_ffn",
    )(x2d, ret_o, nsa_o, gm_o, w_out, gain, wgu, wd)


def _nq_head_order():
    return [c + (N_NSA // N_KV) * half for c in range(NSA_W // LANES) for half in range(2)]


def _w_in_layout(w):
    wt = w.T
    off_nq = 4 * RET_W
    off_kv = off_nq + NSA_W
    off_gate = off_kv + 3 * 2 * N_KV * HEAD_DIM
    off_gm = off_gate + 3 * N_NSA
    heads = [wt[off_nq + h * HEAD_DIM:off_nq + (h + 1) * HEAD_DIM] for h in _nq_head_order()]
    main = jnp.concatenate([wt[:off_nq]] + heads + [wt[off_kv:off_gate + LANES]], axis=0).astype(BF16)
    return main, wt[off_gm:].astype(BF16)


def _w_out_layout(w):
    heads = [w[RET_W + h * HEAD_DIM:RET_W + (h + 1) * HEAD_DIM] for h in _nq_head_order()]
    return jnp.concatenate([w[:RET_W]] + heads + [w[RET_W + NSA_W:]], axis=0).astype(BF16)


def _gate_expand():
    e = np.zeros((LANES, 3 * NSA_W), np.float32)
    ncol = NSA_W // LANES
    for br in range(3):
        for c in range(ncol):
            for half in range(2):
                head = c + (N_NSA // N_KV) * half
                dst = (br * ncol + c) * LANES + half * HEAD_DIM
                e[head * 3 + br, dst:dst + HEAD_DIM] = 1.0
    return jnp.asarray(e, BF16)


def _rope_tables(seq):
    half = HEAD_DIM // 2
    inv = ROPE_THETA ** (-jnp.arange(half, dtype=F32) / half)
    ang = jnp.arange(seq, dtype=jnp.int32).astype(F32)[:, None] * inv[None, :]
    cos = jnp.tile(jnp.cos(ang), (1, LANES // half))
    sin = jnp.tile(jnp.concatenate([-jnp.sin(ang), jnp.sin(ang)], axis=1), (1, LANES // HEAD_DIM))
    return cos, sin


def _retention_tables():
    c = RET_CHUNK
    log_gamma = jnp.log(1.0 - 2.0 ** (-5.0 - jnp.arange(N_RET, dtype=F32)))
    idx = jnp.arange(c, dtype=F32)
    diff = idx[:, None] - idx[None, :]
    decay = jnp.where(diff >= 0, jnp.exp(jnp.maximum(diff, 0.0)[None] * log_gamma[:, None, None]), 0.0)
    zeta = jnp.exp((c - 1 - idx)[None, :] * log_gamma[:, None])
    xi = jnp.exp((idx + 1)[None, :] * log_gamma[:, None])
    chunk_decay = jnp.exp(c * log_gamma)
    ncol = RET_W // LANES
    dec = decay.reshape(ncol, 2 * c, c)
    xi_st = jnp.broadcast_to(xi.reshape(ncol, 2 * c, 1), (ncol, 2 * c, LANES))
    zeta_l = jnp.repeat(zeta.reshape(ncol, 2, c).transpose(0, 2, 1), HEAD_DIM, axis=2)
    cd = jnp.broadcast_to(jnp.repeat(chunk_decay.reshape(ncol, 2), HEAD_DIM, axis=1)[:, :, None],
                          (ncol, LANES, LANES))
    return dec, xi_st, zeta_l, cd


def _overlap_table(seq):
    n_cmp = (seq - CMP_LEN) // CMP_STRIDE + 1
    n_slc = seq // SEL_BLOCK
    cs = np.arange(n_cmp)[:, None] * CMP_STRIDE
    ss = np.arange(n_slc)[None, :] * SEL_BLOCK
    ov = np.clip(np.minimum(cs + CMP_LEN, ss + SEL_BLOCK) - np.maximum(cs, ss), 0, None) // CMP_STRIDE
    full = np.zeros((seq // CMP_STRIDE, LANES), np.float32)
    full[:n_cmp, :n_slc] = ov
    return jnp.asarray(full, BF16)


def _cmp_band(seq):
    nq = seq // QB
    ncmp = seq // CMP_STRIDE
    x = np.arange(2 * ncmp)[:, None]
    m = np.arange(LANES)[None, :]
    base = (QB // CMP_STRIDE) * nq - CMP_LEAD
    band = np.where(m < LANES - 1, x == m + base, x >= m + base)
    return jnp.asarray(band.astype(np.float32), BF16)


def _block_onehot(seq):
    oh = (np.arange(seq)[:, None] // SEL_BLOCK) == np.arange(LANES)[None, :]
    return jnp.asarray(oh.astype(np.float32), BF16)


def _compress_weights(cmp_pe, cmp_w1, cmp_w2):
    def block_diag2(w):
        zero = jnp.zeros_like(w)
        return jnp.concatenate([jnp.concatenate([w, zero], axis=-1), jnp.concatenate([zero, w], axis=-1)],
                               axis=-2).astype(BF16)

    w1bd = block_diag2(cmp_w1.reshape(2, CMP_LEN, HEAD_DIM, HEAD_DIM))
    w2bd = block_diag2(cmp_w2)
    pe2 = jnp.tile(cmp_pe, (1, 1, N_KV))[:, :, None, :]
    return w1bd[:, :CMP_LEN // 2], w1bd[:, CMP_LEN // 2:], pe2, w2bd


def kernel(x, attn_norm, w_in, w_out, nsa_q_gain, nsa_k_gain, cmp_pe, cmp_w1, cmp_w2, gm_ws, gm_b, ffn_norm,
           w_gate_up, w_down, rel_bias):
    b, seq, _ = x.shape
    assert seq % (2 * TK) == 0 and N_SEL <= seq // SEL_BLOCK <= LANES, "selection blocks must fit one lane row"
    depth = w_in.shape[0]
    cos_tab, sin_tab = _rope_tables(seq)
    ret_tabs = _retention_tables()
    overlap = _overlap_table(seq)
    onehot = _block_onehot(seq)
    band = _cmp_band(seq)
    expand = _gate_expand()
    d0, d1, tc = _bias_tiles(rel_bias)
    ngrp = seq // CMP_STRIDE
    wgu_all = w_gate_up.astype(BF16)
    wd_all = w_down.astype(BF16)
    x2d = x.reshape(b * seq, D_MODEL)
    for l in range(depth):
        zero_row = jnp.zeros((5, LANES), F32)
        head_gains = jnp.concatenate([jnp.tile(nsa_q_gain[l] * (HEAD_DIM ** -0.5 * LOG2E), 2)[None],
                                      jnp.tile(nsa_k_gain[l, 1], 2)[None],
                                      jnp.tile(nsa_k_gain[l, 2], 2)[None], zero_row], axis=0)
        proj, cmp_grp = _inproj(x2d, attn_norm[l][None], *_w_in_layout(w_in[l]), cos_tab, sin_tab, head_gains, seq)
        proj3 = proj.reshape(b, seq, PROJ_W)
        ret_o = _retention(proj3, ret_tabs)
        cmp_in = cmp_grp.reshape(b, ngrp, CMP_STRIDE * 2 * LANES)
        wtop, wbot, pe2, w2bd = _compress_weights(cmp_pe[l], cmp_w1[l], cmp_w2[l])
        kgain = jnp.broadcast_to(jnp.tile(nsa_k_gain[l, 0], 2)[None], (8, LANES))
        kcmp, vcmp = _compress(cmp_in, wtop, wbot, pe2, w2bd, kgain)
        nsa_o = _nsa(proj3, kcmp, vcmp, onehot, overlap, band, d0, d1, tc, expand)
        gm_o = _gmlp(proj3, gm_ws[l], jnp.repeat(gm_b[l].T, GM_DIM, axis=1))
        x2d = _outproj_ffn(x2d, ret_o.reshape(b * seq, RET_W), nsa_o.reshape(b * seq, NSA_W),
                           gm_o.reshape(b * seq, GM_W), _w_out_layout(w_out[l]), ffn_norm[l][None],
                           wgu_all, wd_all, l)
    return x2d.reshape(b, seq, D_MODEL)
```

```python
import functools
import math

import numpy as np
import jax
import jax.numpy as jnp
from jax import lax
from jax.experimental import pallas as pl
from jax.experimental.pallas import tpu as pltpu

F32 = jnp.float32
BF16 = jnp.bfloat16

D_MODEL = 1024
HEAD_DIM = 64
N_RET = 6
N_NSA = 6
N_KV = 2
N_GM = 4
GM_DIM = 64
RET_W = N_RET * HEAD_DIM
NSA_W = N_NSA * HEAD_DIM
GM_W = N_GM * GM_DIM
RET_CHUNK = 128
CMP_LEN = 32
CMP_STRIDE = 16
SEL_BLOCK = 64
N_SEL = 16
WINDOW = 512
GM_CHUNK = 128
N_BUCKETS = 32
MAX_DISTANCE = 128
ROPE_THETA = 10000.0
D_FF = 2816
EPS = 1e-6
BIG = 1e9
NEG = -1e30
LOG2E = math.log2(math.e)

LANES = 128
PROJ_W = 3328
COL_RQ, COL_RK, COL_RV, COL_RG = 0, 3, 6, 9
COL_NQ = 12
COL_CMPK, COL_CMPV, COL_SELK, COL_SELV, COL_WINK, COL_WINV = 15, 16, 17, 18, 19, 20
COL_GATE = 21
COL_GMU, COL_GMV = 22, 24
QB = 256
TK = 256
CMP_LEAD = 16
NQ_SUB = 2
NSA_PHASES = 5
CMP_PARTS = 4
FAR_TILES = 8
INPROJ_SPLIT = 2
RET_STEP = 512
GM_STEP = 512
VMEM_LIMIT = 56 * 1024 * 1024


def _bucket_thresholds():
    n = np.arange(0, 4 * MAX_DISTANCE)
    max_exact = N_BUCKETS // 2
    nf = np.maximum(n, 1).astype(np.float64)
    large = max_exact + np.floor(np.log(nf / max_exact) / math.log(MAX_DISTANCE / max_exact)
                                 * (N_BUCKETS - max_exact)).astype(np.int64)
    bucket = np.where(n < max_exact, n, np.minimum(large, N_BUCKETS - 1))
    return [int(np.min(n[bucket >= b])) for b in range(N_BUCKETS)]


BUCKET_THR = _bucket_thresholds()


def _cparams(sem):
    return pltpu.CompilerParams(dimension_semantics=sem, vmem_limit_bytes=VMEM_LIMIT)


def _head_ones():
    r = lax.broadcasted_iota(jnp.int32, (LANES, LANES), 0) // HEAD_DIM
    c = lax.broadcasted_iota(jnp.int32, (LANES, LANES), 1) // HEAD_DIM
    return jnp.where(r == c, 1.0, 0.0).astype(BF16)


def _head_sum(x, ones_bd):
    return jnp.dot(x.astype(BF16), ones_bd, preferred_element_type=F32)


def _head_rmsnorm(x, ones_bd):
    return x * lax.rsqrt(_head_sum(x * x, ones_bd) * (1.0 / HEAD_DIM) + EPS)


def _dot_nt(a, b):
    return lax.dot_general(a, b, (((1,), (1,)), ((), ())), preferred_element_type=F32)


def _dot_tn(a, b):
    return lax.dot_general(a, b, (((0,), (0,)), ((), ())), preferred_element_type=F32)


def _stack_heads(q, low):
    zero = jnp.zeros_like(q)
    return jnp.concatenate([jnp.where(low, q, zero), jnp.where(low, zero, q)], axis=0)


def _inproj_kernel(x_ref, g_ref, wa_ref, wb_ref, cos_ref, sin_ref, hg_ref, o_ref, grp_ref, cmp_scr, *, tm):
    sub_rows = tm // INPROJ_SPLIT
    lane = lax.broadcasted_iota(jnp.int32, (sub_rows, LANES), 1)
    first_half = (lane % HEAD_DIM) < (HEAD_DIM // 2)
    ones_bd = _head_ones()
    norm_gain = {COL_NQ: 0, COL_NQ + 1: 0, COL_NQ + 2: 0, COL_SELK: 1, COL_WINK: 2}
    deferred = []
    for r0 in range(0, tm, sub_rows):
        rows = slice(r0, r0 + sub_rows)
        x = x_ref[rows, :]
        ms = jnp.mean(x * x, axis=-1, keepdims=True)
        h = (x * lax.rsqrt(ms + EPS) * g_ref[...]).astype(BF16)
        cos = cos_ref[rows, :]
        sin = sin_ref[rows, :]
        for s in range(0, PROJ_W // LANES, 2):
            if s < COL_GMU:
                acc = _dot_nt(h, wa_ref[s * LANES:(s + 2) * LANES, :])
            else:
                acc = _dot_nt(h, wb_ref[(s - COL_GMU) * LANES:(s - COL_GMU + 2) * LANES, :])
            for sub in range(2):
                j = s + sub
                a = acc[:, sub * LANES:(sub + 1) * LANES]
                if j < COL_RV:
                    swapped = jnp.where(first_half, pltpu.roll(a, LANES - HEAD_DIM // 2, axis=1),
                                        pltpu.roll(a, HEAD_DIM // 2, axis=1))
                    a = a * cos + swapped * sin
                    if j >= COL_RK:
                        a = a * (HEAD_DIM ** -0.5)
                elif j in norm_gain:
                    deferred.append((rows, j, a))
                    continue
                elif j in (COL_CMPK, COL_CMPV):
                    cmp_scr[j - COL_CMPK, rows, :] = a
                o_ref[rows, j * LANES:(j + 1) * LANES] = a.astype(BF16)
    for rows, j, a in deferred:
        a = _head_rmsnorm(a, ones_bd) * hg_ref[norm_gain[j]:norm_gain[j] + 1, :]
        o_ref[rows, j * LANES:(j + 1) * LANES] = a.astype(BF16)
    for l in range(CMP_STRIDE):
        for kv in range(2):
            grp_ref[:, (2 * l + kv) * LANES:(2 * l + kv + 1) * LANES] = (
                cmp_scr[kv, pl.ds(l, tm // CMP_STRIDE, stride=CMP_STRIDE), :].astype(BF16))


def _inproj(x2d, gain, w_main, w_gm, cos_tab, sin_tab, head_gains, seq):
    m = x2d.shape[0]
    tm = 1024
    nt = seq // tm
    const = lambda arr: pl.BlockSpec(arr.shape, lambda i: (0, 0), pipeline_mode=pl.Buffered(1))
    return pl.pallas_call(
        functools.partial(_inproj_kernel, tm=tm),
        grid=(m // tm,),
        in_specs=[
            pl.BlockSpec((tm, D_MODEL), lambda i: (i, 0)),
            pl.BlockSpec((1, D_MODEL), lambda i: (0, 0)),
            const(w_main), const(w_gm),
            pl.BlockSpec((tm, LANES), lambda i: (i % nt, 0)),
            pl.BlockSpec((tm, LANES), lambda i: (i % nt, 0)),
            pl.BlockSpec((8, LANES), lambda i: (0, 0)),
        ],
        out_specs=[pl.BlockSpec((tm, PROJ_W), lambda i: (i, 0)),
                   pl.BlockSpec((tm // CMP_STRIDE, CMP_STRIDE * 2 * LANES), lambda i: (i, 0))],
        out_shape=[jax.ShapeDtypeStruct((m, PROJ_W), BF16),
                   jax.ShapeDtypeStruct((m // CMP_STRIDE, CMP_STRIDE * 2 * LANES), BF16)],
        scratch_shapes=[pltpu.VMEM((2, tm, LANES), F32)],
        compiler_params=_cparams(("arbitrary",)),
        name="inproj",
    )(x2d, gain, w_main, w_gm, cos_tab, sin_tab, head_gains)


def _ret_kernel(q_ref, k_ref, v_ref, g_ref, dec_ref, xi_ref, zeta_ref, cd_ref, o_ref, r_ref):
    @pl.when(pl.program_id(1) == 0)
    def _():
        r_ref[...] = jnp.zeros_like(r_ref)

    lane = lax.broadcasted_iota(jnp.int32, (RET_CHUNK, LANES), 1)
    low = lane < HEAD_DIM
    ones_bd = _head_ones()
    cells = [(c, ch) for c in range(RET_W // LANES) for ch in range(RET_STEP // RET_CHUNK)]
    sl = lambda c: slice(c * LANES, (c + 1) * LANES)
    rows = lambda ch: slice(ch * RET_CHUNK, (ch + 1) * RET_CHUNK)
    qs = {(c, ch): _stack_heads(q_ref[rows(ch), sl(c)], low) for c, ch in cells}
    scores = {(c, ch): _dot_nt(qs[c, ch], k_ref[rows(ch), sl(c)]) for c, ch in cells}
    incr = {(c, ch): _dot_tn((k_ref[rows(ch), sl(c)].astype(F32) * zeta_ref[c]).astype(BF16), v_ref[rows(ch), sl(c)])
            for c, ch in cells}
    inner = {(c, ch): jnp.dot((scores[c, ch] * dec_ref[c]).astype(BF16), v_ref[rows(ch), sl(c)],
                              preferred_element_type=F32) for c, ch in cells}
    outs = {}
    for c in range(RET_W // LANES):
        state = r_ref[c]
        for ch in range(RET_STEP // RET_CHUNK):
            cross = jnp.dot(qs[c, ch], state.astype(BF16), preferred_element_type=F32) * xi_ref[c]
            tot = inner[c, ch] + cross
            outs[c, ch] = jnp.where(low, tot[:RET_CHUNK], tot[RET_CHUNK:])
            state = cd_ref[c] * state + incr[c, ch]
        r_ref[c] = state
    for c, ch in cells:
        g = g_ref[rows(ch), sl(c)].astype(F32)
        y = _head_rmsnorm(outs[c, ch], ones_bd) * (g * jax.nn.sigmoid(g))
        o_ref[rows(ch), sl(c)] = y.astype(BF16)


def _retention(proj3, tabs):
    b, seq, _ = proj3.shape
    dec, xi, zeta, cd = tabs
    ncol = RET_W // LANES
    qkvg = [pl.BlockSpec((None, RET_STEP, RET_W), functools.partial(lambda bi, ti, col: (bi, ti, col), col=col))
            for col in range(4)]
    const3 = lambda shape: pl.BlockSpec(shape, lambda bi, ti: (0, 0, 0))
    return pl.pallas_call(
        _ret_kernel,
        grid=(b, seq // RET_STEP),
        in_specs=qkvg + [const3((ncol, 2 * RET_CHUNK, LANES)), const3((ncol, 2 * RET_CHUNK, LANES)),
                         const3((ncol, RET_CHUNK, LANES)), const3((ncol, LANES, LANES))],
        out_specs=pl.BlockSpec((None, RET_STEP, RET_W), lambda bi, ti: (bi, ti, 0)),
        out_shape=jax.ShapeDtypeStruct((b, seq, RET_W), BF16),
        scratch_shapes=[pltpu.VMEM((ncol, LANES, LANES), F32)],
        compiler_params=_cparams(("arbitrary", "arbitrary")),
        name="retention",
    )(proj3, proj3, proj3, proj3, dec, xi, zeta, cd)


def _compress_kernel(a_ref, wtop_ref, wbot_ref, pe_ref, w2_ref, kg_ref, kc_ref, vc_ref, *, ngrp):
    half = CMP_LEN // 2
    ones_bd = _head_ones()
    for kv in range(2):
        top = jnp.zeros((ngrp, LANES), F32)
        bot = jnp.zeros((ngrp, LANES), F32)
        for l in range(half):
            x = a_ref[:, l * 2 * LANES + kv * LANES:l * 2 * LANES + (kv + 1) * LANES].astype(F32)
            top = top + jnp.dot((x + pe_ref[kv, l]).astype(BF16), wtop_ref[kv, l], preferred_element_type=F32)
            bot = bot + jnp.dot((x + pe_ref[kv, half + l]).astype(BF16), wbot_ref[kv, l],
                                preferred_element_type=F32)
        hid = jax.nn.gelu(top + pltpu.roll(bot, ngrp - 1, axis=0), approximate=True)
        out = jnp.dot(hid.astype(BF16), w2_ref[kv], preferred_element_type=F32)
        if kv == 0:
            kc_ref[...] = (_head_rmsnorm(out, ones_bd) * kg_ref[0:1, :]).astype(BF16)
        else:
            vc_ref[...] = out.astype(BF16)


def _compress(a, wtop, wbot, pe2, w2bd, kgain):
    b, ngrp, width = a.shape
    full = lambda arr: pl.BlockSpec(arr.shape, lambda bi: (0,) * arr.ndim)
    out_spec = pl.BlockSpec((None, ngrp, LANES), lambda bi: (bi, 0, 0))
    return pl.pallas_call(
        functools.partial(_compress_kernel, ngrp=ngrp),
        grid=(b,),
        in_specs=[pl.BlockSpec((None, ngrp, width), lambda bi: (bi, 0, 0)),
                  full(wtop), full(wbot), full(pe2), full(w2bd), full(kgain)],
        out_specs=[out_spec, out_spec],
        out_shape=[jax.ShapeDtypeStruct((b, ngrp, LANES), BF16)] * 2,
        compiler_params=_cparams(("arbitrary",)),
        name="nsa_compress",
    )(a, wtop, wbot, pe2, w2bd, kgain)


def _bias_kernel(rb_ref, d0_ref, d1_ref, tc_ref):
    h = pl.program_id(0)
    far = rb_ref[N_BUCKETS - 1, h]

    def rel(n):
        val = jnp.full(n.shape, (rb_ref[0, h] - far) * LOG2E, F32)
        for bkt in range(1, N_BUCKETS):
            val = jnp.where(n >= BUCKET_THR[bkt], (rb_ref[bkt, h] - far) * LOG2E, val)
        return val

    ql = lax.broadcasted_iota(jnp.int32, (QB, TK), 0)
    kl = lax.broadcasted_iota(jnp.int32, (QB, TK), 1)
    d = ql - kl
    d0_ref[...] = jnp.where(d >= 0, rel(d), NEG)
    d1_ref[...] = rel(d + TK)
    ql2 = lax.broadcasted_iota(jnp.int32, (QB, LANES), 0)
    m = lax.broadcasted_iota(jnp.int32, (QB, LANES), 1)
    dc = ql2 - CMP_STRIDE * (m - CMP_LEAD) - (CMP_LEN - 1)
    tc_ref[...] = jnp.where(dc >= 0, rel(dc), NEG).astype(BF16)


def _bias_tiles(rel_bias):
    return pl.pallas_call(
        _bias_kernel,
        grid=(N_NSA,),
        in_specs=[pl.BlockSpec(memory_space=pltpu.SMEM)],
        out_specs=[pl.BlockSpec((None, QB, TK), lambda h: (h, 0, 0)),
                   pl.BlockSpec((None, QB, TK), lambda h: (h, 0, 0)),
                   pl.BlockSpec((None, QB, LANES), lambda h: (h, 0, 0))],
        out_shape=[jax.ShapeDtypeStruct((N_NSA, QB, TK), F32),
                   jax.ShapeDtypeStruct((N_NSA, QB, TK), F32),
                   jax.ShapeDtypeStruct((N_NSA, QB, LANES), BF16)],
        compiler_params=_cparams(("arbitrary",)),
        name="t5_bias_tiles",
    )(rel_bias)


def _nsa_kernel(*refs, ncmp):
    programs = [_nsa_tile(sub, *refs, ncmp=ncmp) for sub in range(NQ_SUB)]
    for _ in range(NSA_PHASES):
        for prog in programs:
            next(prog)


def _nsa_tile(sub, q_ref, gt_ref, sk_ref, sv_ref, wk_ref, wv_ref, kc_ref, vc_ref, oh_ref, ov_ref, band_ref,
              d0_ref, d1_ref, tc_ref, e_ref, o_ref, acc_ref, m_ref, *, ncmp):
    i = NQ_SUB * pl.program_id(1) + sub
    t0 = i * QB
    ncol = NSA_W // LANES
    rows = slice(sub * QB, (sub + 1) * QB)
    slot = sub * ncol
    sq = 2 * QB
    low = lax.broadcasted_iota(jnp.int32, (QB, LANES), 1) < HEAD_DIM
    qs = [_stack_heads(q_ref[rows, c * LANES:(c + 1) * LANES], low) for c in range(ncol)]

    def stacked(ref, c):
        return jnp.concatenate([ref[c], ref[c + ncol]], axis=0)

    ones_col = jnp.ones((TK, LANES), BF16)

    n_tiles = ncmp // (QB // CMP_STRIDE)
    n_parts = max(1, min(CMP_PARTS, ncmp // LANES))
    band_start = pl.multiple_of((QB // CMP_STRIDE) * (n_tiles - i), CMP_STRIDE)

    def compress_and_select(part):
        nkeys = ncmp * part // n_parts
        nblk = LANES * part // n_parts
        kc_aug = jnp.concatenate([kc_ref[0:nkeys, :], band_ref[pl.ds(band_start, nkeys), :]], axis=1)
        vc_ov = jnp.concatenate([vc_ref[0:nkeys, :], ov_ref[0:nkeys, :]], axis=1)
        s_cmp = [_dot_nt(jnp.concatenate([qs[c], stacked(tc_ref, c)], axis=1), kc_aug) for c in range(ncol)]
        imp = jnp.zeros((sq, LANES), F32)
        outs = []
        for c in range(ncol):
            s = s_cmp[c]
            mx = jnp.maximum(jnp.max(s, axis=-1, keepdims=True), -1e20)
            e = jnp.exp2(s - mx)
            den = jnp.sum(e, axis=-1, keepdims=True)
            inv = jnp.where(den > 0.0, 1.0 / den, 0.0)
            both = jnp.dot(e.astype(BF16), vc_ov, preferred_element_type=F32) * inv
            imp = imp + both[:, LANES:]
            outs.append(jnp.where(low, both[:QB, :LANES], both[QB:, :LANES]))

        imp_t = imp.T[0:nblk]
        blk = lax.broadcasted_iota(jnp.int32, (nblk, sq), 0)
        cur = (t0 + lax.broadcasted_iota(jnp.int32, (nblk, sq), 1) % QB) // SEL_BLOCK
        forced = (blk == 0) | (blk == cur) | (blk == cur - 1)
        val = jnp.where(forced, -jnp.inf, jnp.where(blk > cur, -BIG, imp_t))
        blk_f = blk.astype(F32)
        for _ in range(N_SEL - 3):
            top = jnp.max(val, axis=0, keepdims=True)
            idx = jnp.min(jnp.where(val == top, blk_f, 1e6), axis=0, keepdims=True)
            val = jnp.where(blk_f == idx, -jnp.inf, val)
        mask_t = jnp.where(val == -jnp.inf, 0.0, NEG)
        if nblk < LANES:
            mask_t = jnp.concatenate([mask_t, jnp.full((LANES - nblk, sq), NEG, F32)], axis=0)
        return tuple(outs) + (mask_t.T.astype(BF16),)

    start_d = pl.multiple_of(t0, TK)
    start_p = pl.multiple_of(jnp.maximum(t0 - TK, 0), TK)
    start_e = pl.multiple_of(jnp.maximum(t0 - 2 * TK, 0), TK)
    s_win = [[_dot_nt(qs[c], wk_ref[pl.ds(st, TK), :]) for st in (start_d, start_p, start_e)] for c in range(ncol)]

    picked = lax.switch(i * n_parts // n_tiles,
                        [functools.partial(compress_and_select, part) for part in range(1, n_parts + 1)])
    o_cmp, msel = picked[:ncol], picked[ncol]
    yield

    off_prev = jnp.where(i >= 1, 0.0, -NEG)
    off_edge = jnp.where(i >= 2, 0.0, -NEG)
    ql_t = lax.broadcasted_iota(jnp.int32, (sq, TK), 0) % QB
    kl_t = lax.broadcasted_iota(jnp.int32, (sq, TK), 1)
    edge = jnp.where(kl_t > ql_t, 0.0, NEG)
    wv_all = jnp.concatenate([
        jnp.concatenate([wv_ref[pl.ds(st, TK), :], ones_col], axis=1) for st in (start_d, start_p, start_e)], axis=0)
    o_win = []
    for c in range(ncol):
        s0 = s_win[c][0] + stacked(d0_ref, c)
        s1 = s_win[c][1] + stacked(d1_ref, c)
        s2 = s_win[c][2] + edge
        mx = jnp.maximum(jnp.maximum(jnp.max(s0, axis=-1, keepdims=True),
                                     jnp.max(s1, axis=-1, keepdims=True) - off_prev),
                         jnp.max(s2, axis=-1, keepdims=True) - off_edge)
        p = jnp.exp2(jnp.concatenate([s0 - mx, s1 - (mx + off_prev), s2 - (mx + off_edge)], axis=1)).astype(BF16)
        ow = jnp.dot(p, wv_all, preferred_element_type=F32)
        o_win.append(jnp.where(low, ow[:QB, :LANES] / ow[:QB, LANES:], ow[QB:, :LANES] / ow[QB:, LANES:]))
    yield

    qaug = [jnp.concatenate([qs[c], msel], axis=1) for c in range(ncol)]

    def sel_scores(c, start, width):
        kaug = jnp.concatenate([sk_ref[pl.ds(start, width), :], oh_ref[pl.ds(start, width), :]], axis=1)
        return _dot_nt(qaug[c], kaug)

    def sel_vaug(start, width):
        return jnp.concatenate([sv_ref[pl.ds(start, width), :], jnp.ones((width, LANES), BF16)], axis=1)

    def lanes(x, width):
        return jnp.concatenate([x] * (width // LANES), axis=1)

    s_near = [(sel_scores(c, start_d, TK), sel_scores(c, start_p, TK)) for c in range(ncol)]
    for c in range(ncol):
        s0 = s_near[c][0] + stacked(d0_ref, c)
        s1 = s_near[c][1] + stacked(d1_ref, c)
        mx = jnp.maximum(jnp.max(s0, axis=-1, keepdims=True), jnp.max(s1, axis=-1, keepdims=True) - off_prev)
        p0 = jnp.exp2(s0 - mx).astype(BF16)
        p1 = jnp.exp2(s1 - (mx + off_prev)).astype(BF16)
        acc_ref[slot + c] = (jnp.dot(p0, sel_vaug(start_d, TK), preferred_element_type=F32)
                             + jnp.dot(p1, sel_vaug(start_p, TK), preferred_element_type=F32))
        m_ref[slot + c] = jnp.broadcast_to(mx, (sq, LANES))
    yield

    def far_tile(kt0, tiles):
        start = pl.multiple_of(kt0 * TK, tiles * TK)
        width = tiles * TK
        vaug = sel_vaug(start, width)
        scores = [sel_scores(c, start, width) for c in range(ncol)]
        for c in range(ncol):
            s = scores[c]
            m_old = m_ref[slot + c]
            m_new = jnp.maximum(m_old, jnp.max(s, axis=-1, keepdims=True))
            alpha = jnp.exp2(m_old - m_new)
            p = jnp.exp2(s - lanes(m_new, width)).astype(BF16)
            acc_ref[slot + c] = (lanes(alpha, 2 * LANES) * acc_ref[slot + c]
                                 + jnp.dot(p, vaug, preferred_element_type=F32))
            m_ref[slot + c] = m_new

    n_far = jnp.maximum(i - 1, 0)

    def far_loop(tiles):
        def body(kt, carry):
            far_tile(kt * tiles, tiles)
            return carry
        return body

    tiles, done = FAR_TILES, 0
    while tiles >= 1:
        count = n_far // tiles
        lax.fori_loop(done, count, far_loop(tiles), 0)
        tiles, done = tiles // 2, count * 2
    yield

    gate = jax.nn.sigmoid(gt_ref[rows, :].astype(F32)).astype(BF16)
    gexp = jnp.dot(gate, e_ref[...], preferred_element_type=F32)
    for c in range(ncol):
        acc = acc_ref[slot + c]
        o_sel = jnp.where(low, acc[:QB, :LANES] / acc[:QB, LANES:], acc[QB:, :LANES] / acc[QB:, LANES:])
        gc = lambda br: gexp[:, (br * ncol + c) * LANES:(br * ncol + c + 1) * LANES]
        o_ref[rows, c * LANES:(c + 1) * LANES] = (gc(0) * o_cmp[c] + gc(1) * o_sel + gc(2) * o_win[c]).astype(BF16)
    yield


def _nsa(proj3, kcmp, vcmp, onehot, overlap, band, d0, d1, tc, expand):
    b, seq, _ = proj3.shape
    ncmp = kcmp.shape[1]
    step = NQ_SUB * QB
    col = lambda width, cidx: pl.BlockSpec((None, step, width), lambda bi, qi: (bi, qi, cidx))
    res = lambda cidx: pl.BlockSpec((None, seq, LANES), lambda bi, qi: (bi, 0, cidx))
    full = lambda arr: pl.BlockSpec(arr.shape, lambda bi, qi: (0,) * arr.ndim)
    cmp_spec = pl.BlockSpec((None, ncmp, LANES), lambda bi, qi: (bi, 0, 0))
    return pl.pallas_call(
        functools.partial(_nsa_kernel, ncmp=ncmp),
        grid=(b, seq // step),
        in_specs=[col(NSA_W, COL_NQ * LANES // NSA_W), col(LANES, COL_GATE),
                  res(COL_SELK), res(COL_SELV), res(COL_WINK), res(COL_WINV),
                  cmp_spec, cmp_spec, full(onehot), full(overlap), full(band), full(d0), full(d1), full(tc),
                  full(expand)],
        out_specs=pl.BlockSpec((None, step, NSA_W), lambda bi, qi: (bi, qi, 0)),
        out_shape=jax.ShapeDtypeStruct((b, seq, NSA_W), BF16),
        scratch_shapes=[pltpu.VMEM((NQ_SUB * NSA_W // LANES, 2 * QB, 2 * LANES), F32),
                        pltpu.VMEM((NQ_SUB * NSA_W // LANES, 2 * QB, LANES), F32)],
        compiler_params=_cparams(("arbitrary", "arbitrary")),
        name="nsa_attention",
    )(proj3, proj3, proj3, proj3, proj3, proj3, kcmp, vcmp, onehot, overlap, band, d0, d1, tc, expand)


def _gmlp_kernel(u_ref, v_ref, ws_ref, b_ref, o_ref):
    lane = lax.broadcasted_iota(jnp.int32, (GM_CHUNK, LANES), 1)
    low = lane < GM_DIM
    row = lax.broadcasted_iota(jnp.int32, (GM_CHUNK, GM_CHUNK), 0)
    colm = lax.broadcasted_iota(jnp.int32, (GM_CHUNK, GM_CHUNK), 1)
    tril = colm <= row
    ones_bd = _head_ones()
    cells = [(cc, ch) for cc in range(GM_W // LANES) for ch in range(GM_STEP // GM_CHUNK)]
    sl = lambda cc: slice(cc * LANES, (cc + 1) * LANES)
    rows = lambda ch: slice(ch * GM_CHUNK, (ch + 1) * GM_CHUNK)
    w = [jnp.concatenate([jnp.where(tril, ws_ref[2 * cc], 0.0), jnp.where(tril, ws_ref[2 * cc + 1], 0.0)],
                         axis=0).astype(BF16) for cc in range(GM_W // LANES)]
    v = {(cc, ch): jax.nn.gelu(v_ref[rows(ch), sl(cc)].astype(F32), approximate=True) for cc, ch in cells}
    sums = {cell: _head_sum(v[cell] * v[cell], ones_bd) for cell in cells}
    vn = {cell: (v[cell] * lax.rsqrt(sums[cell] * (1.0 / GM_DIM) + EPS)).astype(BF16) for cell in cells}
    mixed = {(cc, ch): jnp.dot(w[cc], vn[cc, ch], preferred_element_type=F32) for cc, ch in cells}
    for cc, ch in cells:
        r = mixed[cc, ch]
        sv = jnp.where(low, r[:GM_CHUNK], r[GM_CHUNK:]) + b_ref[:, sl(cc)]
        u = jax.nn.gelu(u_ref[rows(ch), sl(cc)].astype(F32), approximate=True)
        o_ref[rows(ch), sl(cc)] = (u * sv).astype(BF16)


def _gmlp(proj3, ws, bexp):
    b, seq, _ = proj3.shape
    return pl.pallas_call(
        _gmlp_kernel,
        grid=(b, seq // GM_STEP),
        in_specs=[pl.BlockSpec((None, GM_STEP, GM_W), lambda bi, ti: (bi, ti, COL_GMU * LANES // GM_W)),
                  pl.BlockSpec((None, GM_STEP, GM_W), lambda bi, ti: (bi, ti, COL_GMV * LANES // GM_W)),
                  pl.BlockSpec((N_GM, GM_CHUNK, GM_CHUNK), lambda bi, ti: (0, 0, 0)),
                  pl.BlockSpec((GM_CHUNK, GM_W), lambda bi, ti: (0, 0))],
        out_specs=pl.BlockSpec((None, GM_STEP, GM_W), lambda bi, ti: (bi, ti, 0)),
        out_shape=jax.ShapeDtypeStruct((b, seq, GM_W), BF16),
        compiler_params=_cparams(("arbitrary", "arbitrary")),
        name="gmlp",
    )(proj3, proj3, ws, bexp)


FFN_CHUNK = 256
FFN_SPLIT = 2


def _ffn_kernel(x_ref, r_ref, n_ref, m_ref, wo_ref, g_ref, wgu_ref, wd_ref, o_ref, *, tm):
    sub_rows = tm // FFN_SPLIT
    for r0 in range(0, tm, sub_rows):
        rows = slice(r0, r0 + sub_rows)
        mixed = jnp.concatenate([r_ref[rows, :], n_ref[rows, :], m_ref[rows, :]], axis=1)
        x = x_ref[rows, :] + jnp.dot(mixed, wo_ref[...], preferred_element_type=F32)
        ms = jnp.mean(x * x, axis=-1, keepdims=True)
        h = (x * lax.rsqrt(ms + EPS) * g_ref[...]).astype(BF16)
        acc = x
        for f in range(0, D_FF, FFN_CHUNK):
            gate = jnp.dot(h, wgu_ref[:, f:f + FFN_CHUNK], preferred_element_type=F32)
            up = jnp.dot(h, wgu_ref[:, D_FF + f:D_FF + f + FFN_CHUNK], preferred_element_type=F32)
            act = (gate * jax.nn.sigmoid(gate) * up).astype(BF16)
            acc = acc + jnp.dot(act, wd_ref[f:f + FFN_CHUNK, :], preferred_element_type=F32)
        o_ref[rows, :] = acc


def _outproj_ffn(x2d, ret_o, nsa_o, gm_o, w_out, gain, wgu, wd, layer):
    m = x2d.shape[0]
    tm = 1024
    row = lambda width: pl.BlockSpec((tm, width), lambda i: (i, 0))
    const = lambda arr: pl.BlockSpec(arr.shape, lambda i: (0, 0), pipeline_mode=pl.Buffered(1))
    of_layer = lambda arr: pl.BlockSpec((None,) + arr.shape[1:], lambda i: (layer, 0, 0),
                                        pipeline_mode=pl.Buffered(1))
    return pl.pallas_call(
        functools.partial(_ffn_kernel, tm=tm),
        grid=(m // tm,),
        in_specs=[row(D_MODEL), row(RET_W), row(NSA_W), row(GM_W), const(w_out),
                  pl.BlockSpec((1, D_MODEL), lambda i: (0, 0)),
                  of_layer(wgu), of_layer(wd)],
        out_specs=row(D_MODEL),
        out_shape=jax.ShapeDtypeStruct((m, D_MODEL), F32),
        compiler_params=_cparams(("arbitrary",)),
        name="outproj_ffn",
    )(x2d, ret_o, nsa_o, gm_o, w_out, gain, wgu, wd)


def _nq_head_order():
    return [c + (N_NSA // N_KV) * half for c in range(NSA_W // LANES) for half in range(2)]


def _w_in_layout(w):
    wt = w.T
    off_nq = 4 * RET_W
    off_kv = off_nq + NSA_W
    off_gate = off_kv + 3 * 2 * N_KV * HEAD_DIM
    off_gm = off_gate + 3 * N_NSA
    heads = [wt[off_nq + h * HEAD_DIM:off_nq + (h + 1) * HEAD_DIM] for h in _nq_head_order()]
    main = jnp.concatenate([wt[:off_nq]] + heads + [wt[off_kv:off_gate + LANES]], axis=0).astype(BF16)
    return main, wt[off_gm:].astype(BF16)


def _w_out_layout(w):
    heads = [w[RET_W + h * HEAD_DIM:RET_W + (h + 1) * HEAD_DIM] for h in _nq_head_order()]
    return jnp.concatenate([w[:RET_W]] + heads + [w[RET_W + NSA_W:]], axis=0).astype(BF16)


def _gate_expand():
    e = np.zeros((LANES, 3 * NSA_W), np.float32)
    ncol = NSA_W // LANES
    for br in range(3):
        for c in range(ncol):
            for half in range(2):
                head = c + (N_NSA // N_KV) * half
                dst = (br * ncol + c) * LANES + half * HEAD_DIM
                e[head * 3 + br, dst:dst + HEAD_DIM] = 1.0
    return jnp.asarray(e, BF16)


def _rope_tables(seq):
    half = HEAD_DIM // 2
    inv = ROPE_THETA ** (-np.arange(half, dtype=np.float64) / half)
    ang = np.arange(seq, dtype=np.float64)[:, None] * inv[None, :]
    cos = np.tile(np.cos(ang), (1, LANES // half))
    sin = np.tile(np.concatenate([-np.sin(ang), np.sin(ang)], axis=1), (1, LANES // HEAD_DIM))
    return jnp.asarray(cos, F32), jnp.asarray(sin, F32)


def _retention_tables():
    c = RET_CHUNK
    log_gamma = np.log(1.0 - 2.0 ** (-5.0 - np.arange(N_RET, dtype=np.float64)))
    idx = np.arange(c, dtype=np.float64)
    diff = idx[:, None] - idx[None, :]
    decay = np.where(diff >= 0, np.exp(np.maximum(diff, 0.0)[None] * log_gamma[:, None, None]), 0.0)
    zeta = np.exp((c - 1 - idx)[None, :] * log_gamma[:, None])
    xi = np.exp((idx + 1)[None, :] * log_gamma[:, None])
    chunk_decay = np.exp(c * log_gamma)
    ncol = RET_W // LANES
    dec = decay.reshape(ncol, 2 * c, c)
    xi_st = np.broadcast_to(xi.reshape(ncol, 2 * c, 1), (ncol, 2 * c, LANES))
    zeta_l = np.repeat(zeta.reshape(ncol, 2, c).transpose(0, 2, 1), HEAD_DIM, axis=2)
    cd = np.broadcast_to(np.repeat(chunk_decay.reshape(ncol, 2), HEAD_DIM, axis=1)[:, :, None],
                         (ncol, LANES, LANES))
    return tuple(jnp.asarray(t, F32) for t in (dec, xi_st, zeta_l, cd))


def _overlap_table(seq):
    n_cmp = (seq - CMP_LEN) // CMP_STRIDE + 1
    n_slc = seq // SEL_BLOCK
    cs = np.arange(n_cmp)[:, None] * CMP_STRIDE
    ss = np.arange(n_slc)[None, :] * SEL_BLOCK
    ov = np.clip(np.minimum(cs + CMP_LEN, ss + SEL_BLOCK) - np.maximum(cs, ss), 0, None) // CMP_STRIDE
    full = np.zeros((seq // CMP_STRIDE, LANES), np.float32)
    full[:n_cmp, :n_slc] = ov
    return jnp.asarray(full, BF16)


def _cmp_band(seq):
    nq = seq // QB
    ncmp = seq // CMP_STRIDE
    x = np.arange(2 * ncmp)[:, None]
    m = np.arange(LANES)[None, :]
    base = (QB // CMP_STRIDE) * nq - CMP_LEAD
    band = np.where(m < LANES - 1, x == m + base, x >= m + base)
    return jnp.asarray(band.astype(np.float32), BF16)


def _block_onehot(seq):
    oh = (np.arange(seq)[:, None] // SEL_BLOCK) == np.arange(LANES)[None, :]
    return jnp.asarray(oh.astype(np.float32), BF16)


def _compress_weights(cmp_pe, cmp_w1, cmp_w2):
    def block_diag2(w):
        zero = jnp.zeros_like(w)
        return jnp.concatenate([jnp.concatenate([w, zero], axis=-1), jnp.concatenate([zero, w], axis=-1)],
                               axis=-2).astype(BF16)

    w1bd = block_diag2(cmp_w1.reshape(2, CMP_LEN, HEAD_DIM, HEAD_DIM))
    w2bd = block_diag2(cmp_w2)
    pe2 = jnp.tile(cmp_pe, (1, 1, N_KV))[:, :, None, :]
    return w1bd[:, :CMP_LEN // 2], w1bd[:, CMP_LEN // 2:], pe2, w2bd


def kernel(x, attn_norm, w_in, w_out, nsa_q_gain, nsa_k_gain, cmp_pe, cmp_w1, cmp_w2, gm_ws, gm_b, ffn_norm,
           w_gate_up, w_down, rel_bias):
    b, seq, _ = x.shape
    assert seq % (2 * TK) == 0 and N_SEL <= seq // SEL_BLOCK <= LANES, "selection blocks must fit one lane row"
    depth = w_in.shape[0]
    cos_tab, sin_tab = _rope_tables(seq)
    ret_tabs = _retention_tables()
    overlap = _overlap_table(seq)
    onehot = _block_onehot(seq)
    band = _cmp_band(seq)
    expand = _gate_expand()
    d0, d1, tc = _bias_tiles(rel_bias)
    ngrp = seq // CMP_STRIDE
    wgu_all = w_gate_up.astype(BF16)
    wd_all = w_down.astype(BF16)
    x2d = x.reshape(b * seq, D_MODEL)
    for l in range(depth):
        zero_row = jnp.zeros((5, LANES), F32)
        head_gains = jnp.concatenate([jnp.tile(nsa_q_gain[l] * (HEAD_DIM ** -0.5 * LOG2E), 2)[None],
                                      jnp.tile(nsa_k_gain[l, 1], 2)[None],
                                      jnp.tile(nsa_k_gain[l, 2], 2)[None], zero_row], axis=0)
        proj, cmp_grp = _inproj(x2d, attn_norm[l][None], *_w_in_layout(w_in[l]), cos_tab, sin_tab, head_gains, seq)
        proj3 = proj.reshape(b, seq, PROJ_W)
        ret_o = _retention(proj3, ret_tabs)
        cmp_in = cmp_grp.reshape(b, ngrp, CMP_STRIDE * 2 * LANES)
        wtop, wbot, pe2, w2bd = _compress_weights(cmp_pe[l], cmp_w1[l], cmp_w2[l])
        kgain = jnp.broadcast_to(jnp.tile(nsa_k_gain[l, 0], 2)[None], (8, LANES))
        kcmp, vcmp = _compress(cmp_in, wtop, wbot, pe2, w2bd, kgain)
        nsa_o = _nsa(proj3, kcmp, vcmp, onehot, overlap, band, d0, d1, tc, expand)
        gm_o = _gmlp(proj3, gm_ws[l], jnp.repeat(gm_b[l].T, GM_DIM, axis=1))
        x2d = _outproj_ffn(x2d, ret_o.reshape(b * seq, RET_W), nsa_o.reshape(b * seq, NSA_W),
                           gm_o.reshape(b * seq, GM_W), _w_out_layout(w_out[l]), ffn_norm[l][None],
                           wgu_all, wd_all, l)
    return x2d.reshape(b, seq, D_MODEL)
```

```python
import functools
import math

import numpy as np
import jax
import jax.numpy as jnp
from jax import lax
from jax.experimental import pallas as pl
from jax.experimental.pallas import tpu as pltpu

F32 = jnp.float32
BF16 = jnp.bfloat16

D_MODEL = 1024
HEAD_DIM = 64
N_RET = 6
N_NSA = 6
N_KV = 2
N_GM = 4
GM_DIM = 64
RET_W = N_RET * HEAD_DIM
NSA_W = N_NSA * HEAD_DIM
GM_W = N_GM * GM_DIM
RET_CHUNK = 128
CMP_LEN = 32
CMP_STRIDE = 16
SEL_BLOCK = 64
N_SEL = 16
WINDOW = 512
GM_CHUNK = 128
N_BUCKETS = 32
MAX_DISTANCE = 128
ROPE_THETA = 10000.0
D_FF = 2816
EPS = 1e-6
BIG = 1e9
NEG = -1e30
LOG2E = math.log2(math.e)

LANES = 128
PROJ_W = 3328
COL_RQ, COL_RK, COL_RV, COL_RG = 0, 3, 6, 9
COL_NQ = 12
COL_CMPK, COL_CMPV, COL_SELK, COL_SELV, COL_WINK, COL_WINV = 15, 16, 17, 18, 19, 20
COL_GATE = 21
COL_GMU, COL_GMV = 22, 24
QB = 256
TK = 256
CMP_LEAD = 16
NQ_SUB = 2
NSA_PHASES = 6
FAR_TILES = 8
INPROJ_SPLIT = 2
RET_STEP = 512
GM_STEP = 512
VMEM_LIMIT = 56 * 1024 * 1024


def _bucket_thresholds():
    n = np.arange(0, 4 * MAX_DISTANCE)
    max_exact = N_BUCKETS // 2
    nf = np.maximum(n, 1).astype(np.float64)
    large = max_exact + np.floor(np.log(nf / max_exact) / math.log(MAX_DISTANCE / max_exact)
                                 * (N_BUCKETS - max_exact)).astype(np.int64)
    bucket = np.where(n < max_exact, n, np.minimum(large, N_BUCKETS - 1))
    return [int(np.min(n[bucket >= b])) for b in range(N_BUCKETS)]


BUCKET_THR = _bucket_thresholds()


def _cparams(sem):
    return pltpu.CompilerParams(dimension_semantics=sem, vmem_limit_bytes=VMEM_LIMIT)


def _head_ones():
    r = lax.broadcasted_iota(jnp.int32, (LANES, LANES), 0) // HEAD_DIM
    c = lax.broadcasted_iota(jnp.int32, (LANES, LANES), 1) // HEAD_DIM
    return jnp.where(r == c, 1.0, 0.0).astype(BF16)


def _head_sum(x, ones_bd):
    return jnp.dot(x.astype(BF16), ones_bd, preferred_element_type=F32)


def _head_rmsnorm(x, ones_bd):
    return x * lax.rsqrt(_head_sum(x * x, ones_bd) * (1.0 / HEAD_DIM) + EPS)


def _dot_nt(a, b):
    return lax.dot_general(a, b, (((1,), (1,)), ((), ())), preferred_element_type=F32)


def _dot_tn(a, b):
    return lax.dot_general(a, b, (((0,), (0,)), ((), ())), preferred_element_type=F32)


def _stack_heads(q, low):
    zero = jnp.zeros_like(q)
    return jnp.concatenate([jnp.where(low, q, zero), jnp.where(low, zero, q)], axis=0)


def _inproj_kernel(x_ref, g_ref, wa_ref, wb_ref, cos_ref, sin_ref, hg_ref, o_ref, grp_ref, cmp_scr, *, tm):
    sub_rows = tm // INPROJ_SPLIT
    lane = lax.broadcasted_iota(jnp.int32, (sub_rows, LANES), 1)
    first_half = (lane % HEAD_DIM) < (HEAD_DIM // 2)
    ones_bd = _head_ones()
    norm_gain = {COL_NQ: 0, COL_NQ + 1: 0, COL_NQ + 2: 0, COL_SELK: 1, COL_WINK: 2}
    deferred = []
    for r0 in range(0, tm, sub_rows):
        rows = slice(r0, r0 + sub_rows)
        x = x_ref[rows, :]
        ms = jnp.mean(x * x, axis=-1, keepdims=True)
        h = (x * lax.rsqrt(ms + EPS) * g_ref[...]).astype(BF16)
        cos = cos_ref[rows, :]
        sin = sin_ref[rows, :]
        for s in range(0, PROJ_W // LANES, 2):
            if s < COL_GMU:
                acc = _dot_nt(h, wa_ref[s * LANES:(s + 2) * LANES, :])
            else:
                acc = _dot_nt(h, wb_ref[(s - COL_GMU) * LANES:(s - COL_GMU + 2) * LANES, :])
            for sub in range(2):
                j = s + sub
                a = acc[:, sub * LANES:(sub + 1) * LANES]
                if j < COL_RV:
                    swapped = jnp.where(first_half, pltpu.roll(a, LANES - HEAD_DIM // 2, axis=1),
                                        pltpu.roll(a, HEAD_DIM // 2, axis=1))
                    a = a * cos + swapped * sin
                    if j >= COL_RK:
                        a = a * (HEAD_DIM ** -0.5)
                elif j in norm_gain:
                    deferred.append((rows, j, a))
                    continue
                elif j in (COL_CMPK, COL_CMPV):
                    cmp_scr[j - COL_CMPK, rows, :] = a
                o_ref[rows, j * LANES:(j + 1) * LANES] = a.astype(BF16)
    for rows, j, a in deferred:
        a = _head_rmsnorm(a, ones_bd) * hg_ref[norm_gain[j]:norm_gain[j] + 1, :]
        o_ref[rows, j * LANES:(j + 1) * LANES] = a.astype(BF16)
    for l in range(CMP_STRIDE):
        for kv in range(2):
            grp_ref[:, (2 * l + kv) * LANES:(2 * l + kv + 1) * LANES] = (
                cmp_scr[kv, pl.ds(l, tm // CMP_STRIDE, stride=CMP_STRIDE), :].astype(BF16))


def _inproj(x2d, gain, w_main, w_gm, cos_tab, sin_tab, head_gains, seq):
    m = x2d.shape[0]
    tm = 1024
    nt = seq // tm
    const = lambda arr: pl.BlockSpec(arr.shape, lambda i: (0, 0), pipeline_mode=pl.Buffered(1))
    return pl.pallas_call(
        functools.partial(_inproj_kernel, tm=tm),
        grid=(m // tm,),
        in_specs=[
            pl.BlockSpec((tm, D_MODEL), lambda i: (i, 0)),
            pl.BlockSpec((1, D_MODEL), lambda i: (0, 0)),
            const(w_main), const(w_gm),
            pl.BlockSpec((tm, LANES), lambda i: (i % nt, 0)),
            pl.BlockSpec((tm, LANES), lambda i: (i % nt, 0)),
            pl.BlockSpec((8, LANES), lambda i: (0, 0)),
        ],
        out_specs=[pl.BlockSpec((tm, PROJ_W), lambda i: (i, 0)),
                   pl.BlockSpec((tm // CMP_STRIDE, CMP_STRIDE * 2 * LANES), lambda i: (i, 0))],
        out_shape=[jax.ShapeDtypeStruct((m, PROJ_W), BF16),
                   jax.ShapeDtypeStruct((m // CMP_STRIDE, CMP_STRIDE * 2 * LANES), BF16)],
        scratch_shapes=[pltpu.VMEM((2, tm, LANES), F32)],
        compiler_params=_cparams(("arbitrary",)),
        name="inproj",
    )(x2d, gain, w_main, w_gm, cos_tab, sin_tab, head_gains)


def _ret_kernel(q_ref, k_ref, v_ref, g_ref, dec_ref, xi_ref, zeta_ref, cd_ref, o_ref, r_ref):
    @pl.when(pl.program_id(1) == 0)
    def _():
        r_ref[...] = jnp.zeros_like(r_ref)

    lane = lax.broadcasted_iota(jnp.int32, (RET_CHUNK, LANES), 1)
    low = lane < HEAD_DIM
    ones_bd = _head_ones()
    cells = [(c, ch) for c in range(RET_W // LANES) for ch in range(RET_STEP // RET_CHUNK)]
    sl = lambda c: slice(c * LANES, (c + 1) * LANES)
    rows = lambda ch: slice(ch * RET_CHUNK, (ch + 1) * RET_CHUNK)
    qs = {(c, ch): _stack_heads(q_ref[rows(ch), sl(c)], low) for c, ch in cells}
    scores = {(c, ch): _dot_nt(qs[c, ch], k_ref[rows(ch), sl(c)]) for c, ch in cells}
    incr = {(c, ch): _dot_tn((k_ref[rows(ch), sl(c)].astype(F32) * zeta_ref[c]).astype(BF16), v_ref[rows(ch), sl(c)])
            for c, ch in cells}
    inner = {(c, ch): jnp.dot((scores[c, ch] * dec_ref[c]).astype(BF16), v_ref[rows(ch), sl(c)],
                              preferred_element_type=F32) for c, ch in cells}
    outs = {}
    for c in range(RET_W // LANES):
        state = r_ref[c]
        for ch in range(RET_STEP // RET_CHUNK):
            cross = jnp.dot(qs[c, ch], state.astype(BF16), preferred_element_type=F32) * xi_ref[c]
            tot = inner[c, ch] + cross
            outs[c, ch] = jnp.where(low, tot[:RET_CHUNK], tot[RET_CHUNK:])
            state = cd_ref[c] * state + incr[c, ch]
        r_ref[c] = state
    for c, ch in cells:
        g = g_ref[rows(ch), sl(c)].astype(F32)
        y = _head_rmsnorm(outs[c, ch], ones_bd) * (g * jax.nn.sigmoid(g))
        o_ref[rows(ch), sl(c)] = y.astype(BF16)


def _retention(proj3, tabs):
    b, seq, _ = proj3.shape
    dec, xi, zeta, cd = tabs
    ncol = RET_W // LANES
    qkvg = [pl.BlockSpec((None, RET_STEP, RET_W), functools.partial(lambda bi, ti, col: (bi, ti, col), col=col))
            for col in range(4)]
    const3 = lambda shape: pl.BlockSpec(shape, lambda bi, ti: (0, 0, 0))
    return pl.pallas_call(
        _ret_kernel,
        grid=(b, seq // RET_STEP),
        in_specs=qkvg + [const3((ncol, 2 * RET_CHUNK, LANES)), const3((ncol, 2 * RET_CHUNK, LANES)),
                         const3((ncol, RET_CHUNK, LANES)), const3((ncol, LANES, LANES))],
        out_specs=pl.BlockSpec((None, RET_STEP, RET_W), lambda bi, ti: (bi, ti, 0)),
        out_shape=jax.ShapeDtypeStruct((b, seq, RET_W), BF16),
        scratch_shapes=[pltpu.VMEM((ncol, LANES, LANES), F32)],
        compiler_params=_cparams(("arbitrary", "arbitrary")),
        name="retention",
    )(proj3, proj3, proj3, proj3, dec, xi, zeta, cd)


def _compress_kernel(a_ref, wtop_ref, wbot_ref, pe_ref, w2_ref, kg_ref, kc_ref, vc_ref, *, ngrp):
    half = CMP_LEN // 2
    ones_bd = _head_ones()
    for kv in range(2):
        top = jnp.zeros((ngrp, LANES), F32)
        bot = jnp.zeros((ngrp, LANES), F32)
        for l in range(half):
            x = a_ref[:, l * 2 * LANES + kv * LANES:l * 2 * LANES + (kv + 1) * LANES].astype(F32)
            top = top + jnp.dot((x + pe_ref[kv, l]).astype(BF16), wtop_ref[kv, l], preferred_element_type=F32)
            bot = bot + jnp.dot((x + pe_ref[kv, half + l]).astype(BF16), wbot_ref[kv, l],
                                preferred_element_type=F32)
        hid = jax.nn.gelu(top + pltpu.roll(bot, ngrp - 1, axis=0), approximate=True)
        out = jnp.dot(hid.astype(BF16), w2_ref[kv], preferred_element_type=F32)
        if kv == 0:
            kc_ref[...] = (_head_rmsnorm(out, ones_bd) * kg_ref[0:1, :]).astype(BF16)
        else:
            vc_ref[...] = out.astype(BF16)


def _compress(a, wtop, wbot, pe2, w2bd, kgain):
    b, ngrp, width = a.shape
    full = lambda arr: pl.BlockSpec(arr.shape, lambda bi: (0,) * arr.ndim)
    out_spec = pl.BlockSpec((None, ngrp, LANES), lambda bi: (bi, 0, 0))
    return pl.pallas_call(
        functools.partial(_compress_kernel, ngrp=ngrp),
        grid=(b,),
        in_specs=[pl.BlockSpec((None, ngrp, width), lambda bi: (bi, 0, 0)),
                  full(wtop), full(wbot), full(pe2), full(w2bd), full(kgain)],
        out_specs=[out_spec, out_spec],
        out_shape=[jax.ShapeDtypeStruct((b, ngrp, LANES), BF16)] * 2,
        compiler_params=_cparams(("arbitrary",)),
        name="nsa_compress",
    )(a, wtop, wbot, pe2, w2bd, kgain)


def _bias_kernel(rb_ref, d0_ref, d1_ref, tc_ref):
    h = pl.program_id(0)
    far = rb_ref[N_BUCKETS - 1, h]

    def rel(n):
        val = jnp.full(n.shape, (rb_ref[0, h] - far) * LOG2E, F32)
        for bkt in range(1, N_BUCKETS):
            val = jnp.where(n >= BUCKET_THR[bkt], (rb_ref[bkt, h] - far) * LOG2E, val)
        return val

    ql = lax.broadcasted_iota(jnp.int32, (QB, TK), 0)
    kl = lax.broadcasted_iota(jnp.int32, (QB, TK), 1)
    d = ql - kl
    d0_ref[...] = jnp.where(d >= 0, rel(d), NEG)
    d1_ref[...] = rel(d + TK)
    ql2 = lax.broadcasted_iota(jnp.int32, (QB, LANES), 0)
    m = lax.broadcasted_iota(jnp.int32, (QB, LANES), 1)
    dc = ql2 - CMP_STRIDE * (m - CMP_LEAD) - (CMP_LEN - 1)
    tc_ref[...] = jnp.where(dc >= 0, rel(dc), NEG).astype(BF16)


def _bias_tiles(rel_bias):
    return pl.pallas_call(
        _bias_kernel,
        grid=(N_NSA,),
        in_specs=[pl.BlockSpec(memory_space=pltpu.SMEM)],
        out_specs=[pl.BlockSpec((None, QB, TK), lambda h: (h, 0, 0)),
                   pl.BlockSpec((None, QB, TK), lambda h: (h, 0, 0)),
                   pl.BlockSpec((None, QB, LANES), lambda h: (h, 0, 0))],
        out_shape=[jax.ShapeDtypeStruct((N_NSA, QB, TK), F32),
                   jax.ShapeDtypeStruct((N_NSA, QB, TK), F32),
                   jax.ShapeDtypeStruct((N_NSA, QB, LANES), BF16)],
        compiler_params=_cparams(("arbitrary",)),
        name="t5_bias_tiles",
    )(rel_bias)


def _nsa_kernel(*refs, ncmp):
    programs = [_nsa_tile(sub, *refs, ncmp=ncmp) for sub in range(NQ_SUB)]
    for _ in range(NSA_PHASES):
        for prog in programs:
            next(prog)


def _nsa_tile(sub, q_ref, gt_ref, sk_ref, sv_ref, wk_ref, wv_ref, kc_ref, vc_ref, oh_ref, ov_ref, band_ref,
              d0_ref, d1_ref, tc_ref, e_ref, o_ref, acc_ref, m_ref, *, ncmp):
    i = NQ_SUB * pl.program_id(1) + sub
    t0 = i * QB
    ncol = NSA_W // LANES
    rows = slice(sub * QB, (sub + 1) * QB)
    slot = sub * ncol
    sq = 2 * QB
    low = lax.broadcasted_iota(jnp.int32, (QB, LANES), 1) < HEAD_DIM
    qs = [_stack_heads(q_ref[rows, c * LANES:(c + 1) * LANES], low) for c in range(ncol)]

    def stacked(ref, c):
        return jnp.concatenate([ref[c], ref[c + ncol]], axis=0)

    ones_col = jnp.ones((TK, LANES), BF16)

    n_tiles = NQ_SUB * pl.num_programs(1)
    band = band_ref[pl.ds(pl.multiple_of((QB // CMP_STRIDE) * (n_tiles - i), CMP_STRIDE), ncmp), :]
    kc_aug = jnp.concatenate([kc_ref[...], band], axis=1)
    vc_ov = jnp.concatenate([vc_ref[...], ov_ref[...]], axis=1)
    imp = jnp.zeros((sq, LANES), F32)
    o_cmp = []
    s_cmp = [_dot_nt(jnp.concatenate([qs[c], stacked(tc_ref, c)], axis=1), kc_aug) for c in range(ncol)]

    start_d = pl.multiple_of(t0, TK)
    start_p = pl.multiple_of(jnp.maximum(t0 - TK, 0), TK)
    start_e = pl.multiple_of(jnp.maximum(t0 - 2 * TK, 0), TK)
    s_win = [[_dot_nt(qs[c], wk_ref[pl.ds(st, TK), :]) for st in (start_d, start_p, start_e)] for c in range(ncol)]

    for c in range(ncol):
        s = s_cmp[c]
        mx = jnp.maximum(jnp.max(s, axis=-1, keepdims=True), -1e20)
        e = jnp.exp2(s - mx)
        den = jnp.sum(e, axis=-1, keepdims=True)
        inv = jnp.where(den > 0.0, 1.0 / den, 0.0)
        both = jnp.dot(e.astype(BF16), vc_ov, preferred_element_type=F32) * inv
        imp = imp + both[:, LANES:]
        o_cmp.append(jnp.where(low, both[:QB, :LANES], both[QB:, :LANES]))
    yield

    off_prev = jnp.where(i >= 1, 0.0, -NEG)
    off_edge = jnp.where(i >= 2, 0.0, -NEG)
    ql_t = lax.broadcasted_iota(jnp.int32, (sq, TK), 0) % QB
    kl_t = lax.broadcasted_iota(jnp.int32, (sq, TK), 1)
    edge = jnp.where(kl_t > ql_t, 0.0, NEG)
    wv_all = jnp.concatenate([
        jnp.concatenate([wv_ref[pl.ds(st, TK), :], ones_col], axis=1) for st in (start_d, start_p, start_e)], axis=0)
    o_win = []
    for c in range(ncol):
        s0 = s_win[c][0] + stacked(d0_ref, c)
        s1 = s_win[c][1] + stacked(d1_ref, c)
        s2 = s_win[c][2] + edge
        mx = jnp.maximum(jnp.maximum(jnp.max(s0, axis=-1, keepdims=True),
                                     jnp.max(s1, axis=-1, keepdims=True) - off_prev),
                         jnp.max(s2, axis=-1, keepdims=True) - off_edge)
        p = jnp.exp2(jnp.concatenate([s0 - mx, s1 - (mx + off_prev), s2 - (mx + off_edge)], axis=1)).astype(BF16)
        ow = jnp.dot(p, wv_all, preferred_element_type=F32)
        o_win.append(jnp.where(low, ow[:QB, :LANES] / ow[:QB, LANES:], ow[QB:, :LANES] / ow[QB:, LANES:]))
    yield

    imp_t = imp.T
    blk = lax.broadcasted_iota(jnp.int32, (LANES, sq), 0)
    cur = (t0 + lax.broadcasted_iota(jnp.int32, (LANES, sq), 1) % QB) // SEL_BLOCK
    forced = (blk == 0) | (blk == cur) | (blk == cur - 1)
    imp_t = jnp.where(forced, -jnp.inf, jnp.where(blk > cur, -BIG, imp_t))
    blk_f = blk.astype(F32)

    val = imp_t
    for _ in range(N_SEL - 3):
        top = jnp.max(val, axis=0, keepdims=True)
        idx = jnp.min(jnp.where(val == top, blk_f, 1e6), axis=0, keepdims=True)
        val = jnp.where(blk_f == idx, -jnp.inf, val)
    msel = jnp.where(val == -jnp.inf, 0.0, NEG).T.astype(BF16)
    yield

    qaug = [jnp.concatenate([qs[c], msel], axis=1) for c in range(ncol)]

    def sel_scores(c, start, width):
        kaug = jnp.concatenate([sk_ref[pl.ds(start, width), :], oh_ref[pl.ds(start, width), :]], axis=1)
        return _dot_nt(qaug[c], kaug)

    def sel_vaug(start, width):
        return jnp.concatenate([sv_ref[pl.ds(start, width), :], jnp.ones((width, LANES), BF16)], axis=1)

    def lanes(x, width):
        return jnp.concatenate([x] * (width // LANES), axis=1)

    s_near = [(sel_scores(c, start_d, TK), sel_scores(c, start_p, TK)) for c in range(ncol)]
    for c in range(ncol):
        s0 = s_near[c][0] + stacked(d0_ref, c)
        s1 = s_near[c][1] + stacked(d1_ref, c)
        mx = jnp.maximum(jnp.max(s0, axis=-1, keepdims=True), jnp.max(s1, axis=-1, keepdims=True) - off_prev)
        p0 = jnp.exp2(s0 - mx).astype(BF16)
        p1 = jnp.exp2(s1 - (mx + off_prev)).astype(BF16)
        acc_ref[slot + c] = (jnp.dot(p0, sel_vaug(start_d, TK), preferred_element_type=F32)
                             + jnp.dot(p1, sel_vaug(start_p, TK), preferred_element_type=F32))
        m_ref[slot + c] = jnp.broadcast_to(mx, (sq, LANES))
    yield

    def far_tile(kt0, tiles):
        start = pl.multiple_of(kt0 * TK, tiles * TK)
        width = tiles * TK
        vaug = sel_vaug(start, width)
        scores = [sel_scores(c, start, width) for c in range(ncol)]
        for c in range(ncol):
            s = scores[c]
            m_old = m_ref[slot + c]
            m_new = jnp.maximum(m_old, jnp.max(s, axis=-1, keepdims=True))
            alpha = jnp.exp2(m_old - m_new)
            p = jnp.exp2(s - lanes(m_new, width)).astype(BF16)
            acc_ref[slot + c] = (lanes(alpha, 2 * LANES) * acc_ref[slot + c]
                                 + jnp.dot(p, vaug, preferred_element_type=F32))
            m_ref[slot + c] = m_new

    n_far = jnp.maximum(i - 1, 0)

    def far_loop(tiles):
        def body(kt, carry):
            far_tile(kt * tiles, tiles)
            return carry
        return body

    tiles, done = FAR_TILES, 0
    while tiles >= 1:
        count = n_far // tiles
        lax.fori_loop(done, count, far_loop(tiles), 0)
        tiles, done = tiles // 2, count * 2
    yield

    gate = jax.nn.sigmoid(gt_ref[rows, :].astype(F32)).astype(BF16)
    gexp = jnp.dot(gate, e_ref[...], preferred_element_type=F32)
    for c in range(ncol):
        acc = acc_ref[slot + c]
        o_sel = jnp.where(low, acc[:QB, :LANES] / acc[:QB, LANES:], acc[QB:, :LANES] / acc[QB:, LANES:])
        gc = lambda br: gexp[:, (br * ncol + c) * LANES:(br * ncol + c + 1) * LANES]
        o_ref[rows, c * LANES:(c + 1) * LANES] = (gc(0) * o_cmp[c] + gc(1) * o_sel + gc(2) * o_win[c]).astype(BF16)
    yield


def _nsa(proj3, kcmp, vcmp, onehot, overlap, band, d0, d1, tc, expand):
    b, seq, _ = proj3.shape
    ncmp = kcmp.shape[1]
    step = NQ_SUB * QB
    col = lambda width, cidx: pl.BlockSpec((None, step, width), lambda bi, qi: (bi, qi, cidx))
    res = lambda cidx: pl.BlockSpec((None, seq, LANES), lambda bi, qi: (bi, 0, cidx))
    full = lambda arr: pl.BlockSpec(arr.shape, lambda bi, qi: (0,) * arr.ndim)
    cmp_spec = pl.BlockSpec((None, ncmp, LANES), lambda bi, qi: (bi, 0, 0))
    return pl.pallas_call(
        functools.partial(_nsa_kernel, ncmp=ncmp),
        grid=(b, seq // step),
        in_specs=[col(NSA_W, COL_NQ * LANES // NSA_W), col(LANES, COL_GATE),
                  res(COL_SELK), res(COL_SELV), res(COL_WINK), res(COL_WINV),
                  cmp_spec, cmp_spec, full(onehot), full(overlap), full(band), full(d0), full(d1), full(tc),
                  full(expand)],
        out_specs=pl.BlockSpec((None, step, NSA_W), lambda bi, qi: (bi, qi, 0)),
        out_shape=jax.ShapeDtypeStruct((b, seq, NSA_W), BF16),
        scratch_shapes=[pltpu.VMEM((NQ_SUB * NSA_W // LANES, 2 * QB, 2 * LANES), F32),
                        pltpu.VMEM((NQ_SUB * NSA_W // LANES, 2 * QB, LANES), F32)],
        compiler_params=_cparams(("arbitrary", "arbitrary")),
        name="nsa_attention",
    )(proj3, proj3, proj3, proj3, proj3, proj3, kcmp, vcmp, onehot, overlap, band, d0, d1, tc, expand)


def _gmlp_kernel(u_ref, v_ref, ws_ref, b_ref, o_ref):
    lane = lax.broadcasted_iota(jnp.int32, (GM_CHUNK, LANES), 1)
    low = lane < GM_DIM
    row = lax.broadcasted_iota(jnp.int32, (GM_CHUNK, GM_CHUNK), 0)
    colm = lax.broadcasted_iota(jnp.int32, (GM_CHUNK, GM_CHUNK), 1)
    tril = colm <= row
    ones_bd = _head_ones()
    cells = [(cc, ch) for cc in range(GM_W // LANES) for ch in range(GM_STEP // GM_CHUNK)]
    sl = lambda cc: slice(cc * LANES, (cc + 1) * LANES)
    rows = lambda ch: slice(ch * GM_CHUNK, (ch + 1) * GM_CHUNK)
    w = [jnp.concatenate([jnp.where(tril, ws_ref[2 * cc], 0.0), jnp.where(tril, ws_ref[2 * cc + 1], 0.0)],
                         axis=0).astype(BF16) for cc in range(GM_W // LANES)]
    v = {(cc, ch): jax.nn.gelu(v_ref[rows(ch), sl(cc)].astype(F32), approximate=True) for cc, ch in cells}
    sums = {cell: _head_sum(v[cell] * v[cell], ones_bd) for cell in cells}
    vn = {cell: (v[cell] * lax.rsqrt(sums[cell] * (1.0 / GM_DIM) + EPS)).astype(BF16) for cell in cells}
    mixed = {(cc, ch): jnp.dot(w[cc], vn[cc, ch], preferred_element_type=F32) for cc, ch in cells}
    for cc, ch in cells:
        r = mixed[cc, ch]
        sv = jnp.where(low, r[:GM_CHUNK], r[GM_CHUNK:]) + b_ref[:, sl(cc)]
        u = jax.nn.gelu(u_ref[rows(ch), sl(cc)].astype(F32), approximate=True)
        o_ref[rows(ch), sl(cc)] = (u * sv).astype(BF16)


def _gmlp(proj3, ws, bexp):
    b, seq, _ = proj3.shape
    return pl.pallas_call(
        _gmlp_kernel,
        grid=(b, seq // GM_STEP),
        in_specs=[pl.BlockSpec((None, GM_STEP, GM_W), lambda bi, ti: (bi, ti, COL_GMU * LANES // GM_W)),
                  pl.BlockSpec((None, GM_STEP, GM_W), lambda bi, ti: (bi, ti, COL_GMV * LANES // GM_W)),
                  pl.BlockSpec((N_GM, GM_CHUNK, GM_CHUNK), lambda bi, ti: (0, 0, 0)),
                  pl.BlockSpec((GM_CHUNK, GM_W), lambda bi, ti: (0, 0))],
        out_specs=pl.BlockSpec((None, GM_STEP, GM_W), lambda bi, ti: (bi, ti, 0)),
        out_shape=jax.ShapeDtypeStruct((b, seq, GM_W), BF16),
        compiler_params=_cparams(("arbitrary", "arbitrary")),
        name="gmlp",
    )(proj3, proj3, ws, bexp)


FFN_CHUNK = 256
FFN_SPLIT = 2


def _ffn_kernel(x_ref, r_ref, n_ref, m_ref, wo_ref, g_ref, wgu_ref, wd_ref, o_ref, *, tm):
    sub_rows = tm // FFN_SPLIT
    for r0 in range(0, tm, sub_rows):
        rows = slice(r0, r0 + sub_rows)
        mixed = jnp.concatenate([r_ref[rows, :], n_ref[rows, :], m_ref[rows, :]], axis=1)
        x = x_ref[rows, :] + jnp.dot(mixed, wo_ref[...], preferred_element_type=F32)
        ms = jnp.mean(x * x, axis=-1, keepdims=True)
        h = (x * lax.rsqrt(ms + EPS) * g_ref[...]).astype(BF16)
        acc = x
        for f in range(0, D_FF, FFN_CHUNK):
            gate = jnp.dot(h, wgu_ref[:, f:f + FFN_CHUNK], preferred_element_type=F32)
            up = jnp.dot(h, wgu_ref[:, D_FF + f:D_FF + f + FFN_CHUNK], preferred_element_type=F32)
            act = (gate * jax.nn.sigmoid(gate) * up).astype(BF16)
            acc = acc + jnp.dot(act, wd_ref[f:f + FFN_CHUNK, :], preferred_element_type=F32)
        o_ref[rows, :] = acc


def _outproj_ffn(x2d, ret_o, nsa_o, gm_o, w_out, gain, wgu, wd, layer):
    m = x2d.shape[0]
    tm = 1024
    row = lambda width: pl.BlockSpec((tm, width), lambda i: (i, 0))
    const = lambda arr: pl.BlockSpec(arr.shape, lambda i: (0, 0), pipeline_mode=pl.Buffered(1))
    of_layer = lambda arr: pl.BlockSpec((None,) + arr.shape[1:], lambda i: (layer, 0, 0),
                                        pipeline_mode=pl.Buffered(1))
    return pl.pallas_call(
        functools.partial(_ffn_kernel, tm=tm),
        grid=(m // tm,),
        in_specs=[row(D_MODEL), row(RET_W), row(NSA_W), row(GM_W), const(w_out),
                  pl.BlockSpec((1, D_MODEL), lambda i: (0, 0)),
                  of_layer(wgu), of_layer(wd)],
        out_specs=row(D_MODEL),
        out_shape=jax.ShapeDtypeStruct((m, D_MODEL), F32),
        compiler_params=_cparams(("arbitrary",)),
        name="outproj_ffn",
    )(x2d, ret_o, nsa_o, gm_o, w_out, gain, wgu, wd)


def _nq_head_order():
    return [c + (N_NSA // N_KV) * half for c in range(NSA_W // LANES) for half in range(2)]


def _w_in_layout(w):
    wt = w.T
    off_nq = 4 * RET_W
    off_kv = off_nq + NSA_W
    off_gate = off_kv + 3 * 2 * N_KV * HEAD_DIM
    off_gm = off_gate + 3 * N_NSA
    heads = [wt[off_nq + h * HEAD_DIM:off_nq + (h + 1) * HEAD_DIM] for h in _nq_head_order()]
    main = jnp.concatenate([wt[:off_nq]] + heads + [wt[off_kv:off_gate + LANES]], axis=0).astype(BF16)
    return main, wt[off_gm:].astype(BF16)


def _w_out_layout(w):
    heads = [w[RET_W + h * HEAD_DIM:RET_W + (h + 1) * HEAD_DIM] for h in _nq_head_order()]
    return jnp.concatenate([w[:RET_W]] + heads + [w[RET_W + NSA_W:]], axis=0).astype(BF16)


def _gate_expand():
    e = np.zeros((LANES, 3 * NSA_W), np.float32)
    ncol = NSA_W // LANES
    for br in range(3):
        for c in range(ncol):
            for half in range(2):
                head = c + (N_NSA // N_KV) * half
                dst = (br * ncol + c) * LANES + half * HEAD_DIM
                e[head * 3 + br, dst:dst + HEAD_DIM] = 1.0
    return jnp.asarray(e, BF16)


def _rope_tables(seq):
    half = HEAD_DIM // 2
    inv = ROPE_THETA ** (-np.arange(half, dtype=np.float64) / half)
    ang = np.arange(seq, dtype=np.float64)[:, None] * inv[None, :]
    cos = np.tile(np.cos(ang), (1, LANES // half))
    sin = np.tile(np.concatenate([-np.sin(ang), np.sin(ang)], axis=1), (1, LANES // HEAD_DIM))
    return jnp.asarray(cos, F32), jnp.asarray(sin, F32)


def _retention_tables():
    c = RET_CHUNK
    log_gamma = np.log(1.0 - 2.0 ** (-5.0 - np.arange(N_RET, dtype=np.float64)))
    idx = np.arange(c, dtype=np.float64)
    diff = idx[:, None] - idx[None, :]
    decay = np.where(diff >= 0, np.exp(np.maximum(diff, 0.0)[None] * log_gamma[:, None, None]), 0.0)
    zeta = np.exp((c - 1 - idx)[None, :] * log_gamma[:, None])
    xi = np.exp((idx + 1)[None, :] * log_gamma[:, None])
    chunk_decay = np.exp(c * log_gamma)
    ncol = RET_W // LANES
    dec = decay.reshape(ncol, 2 * c, c)
    xi_st = np.broadcast_to(xi.reshape(ncol, 2 * c, 1), (ncol, 2 * c, LANES))
    zeta_l = np.repeat(zeta.reshape(ncol, 2, c).transpose(0, 2, 1), HEAD_DIM, axis=2)
    cd = np.broadcast_to(np.repeat(chunk_decay.reshape(ncol, 2), HEAD_DIM, axis=1)[:, :, None],
                         (ncol, LANES, LANES))
    return tuple(jnp.asarray(t, F32) for t in (dec, xi_st, zeta_l, cd))


def _overlap_table(seq):
    n_cmp = (seq - CMP_LEN) // CMP_STRIDE + 1
    n_slc = seq // SEL_BLOCK
    cs = np.arange(n_cmp)[:, None] * CMP_STRIDE
    ss = np.arange(n_slc)[None, :] * SEL_BLOCK
    ov = np.clip(np.minimum(cs + CMP_LEN, ss + SEL_BLOCK) - np.maximum(cs, ss), 0, None) // CMP_STRIDE
    full = np.zeros((seq // CMP_STRIDE, LANES), np.float32)
    full[:n_cmp, :n_slc] = ov
    return jnp.asarray(full, BF16)


def _cmp_band(seq):
    nq = seq // QB
    ncmp = seq // CMP_STRIDE
    x = np.arange(2 * ncmp)[:, None]
    m = np.arange(LANES)[None, :]
    base = (QB // CMP_STRIDE) * nq - CMP_LEAD
    band = np.where(m < LANES - 1, x == m + base, x >= m + base)
    return jnp.asarray(band.astype(np.float32), BF16)


def _block_onehot(seq):
    oh = (np.arange(seq)[:, None] // SEL_BLOCK) == np.arange(LANES)[None, :]
    return jnp.asarray(oh.astype(np.float32), BF16)


def _compress_weights(cmp_pe, cmp_w1, cmp_w2):
    def block_diag2(w):
        zero = jnp.zeros_like(w)
        return jnp.concatenate([jnp.concatenate([w, zero], axis=-1), jnp.concatenate([zero, w], axis=-1)],
                               axis=-2).astype(BF16)

    w1bd = block_diag2(cmp_w1.reshape(2, CMP_LEN, HEAD_DIM, HEAD_DIM))
    w2bd = block_diag2(cmp_w2)
    pe2 = jnp.tile(cmp_pe, (1, 1, N_KV))[:, :, None, :]
    return w1bd[:, :CMP_LEN // 2], w1bd[:, CMP_LEN // 2:], pe2, w2bd


def kernel(x, attn_norm, w_in, w_out, nsa_q_gain, nsa_k_gain, cmp_pe, cmp_w1, cmp_w2, gm_ws, gm_b, ffn_norm,
           w_gate_up, w_down, rel_bias):
    b, seq, _ = x.shape
    assert seq % (2 * TK) == 0 and N_SEL <= seq // SEL_BLOCK <= LANES, "selection blocks must fit one lane row"
    depth = w_in.shape[0]
    cos_tab, sin_tab = _rope_tables(seq)
    ret_tabs = _retention_tables()
    overlap = _overlap_table(seq)
    onehot = _block_onehot(seq)
    band = _cmp_band(seq)
    expand = _gate_expand()
    d0, d1, tc = _bias_tiles(rel_bias)
    ngrp = seq // CMP_STRIDE
    wgu_all = w_gate_up.astype(BF16)
    wd_all = w_down.astype(BF16)
    x2d = x.reshape(b * seq, D_MODEL)
    for l in range(depth):
        zero_row = jnp.zeros((5, LANES), F32)
        head_gains = jnp.concatenate([jnp.tile(nsa_q_gain[l] * (HEAD_DIM ** -0.5 * LOG2E), 2)[None],
                                      jnp.tile(nsa_k_gain[l, 1], 2)[None],
                                      jnp.tile(nsa_k_gain[l, 2], 2)[None], zero_row], axis=0)
        proj, cmp_grp = _inproj(x2d, attn_norm[l][None], *_w_in_layout(w_in[l]), cos_tab, sin_tab, head_gains, seq)
        proj3 = proj.reshape(b, seq, PROJ_W)
        ret_o = _retention(proj3, ret_tabs)
        cmp_in = cmp_grp.reshape(b, ngrp, CMP_STRIDE * 2 * LANES)
        wtop, wbot, pe2, w2bd = _compress_weights(cmp_pe[l], cmp_w1[l], cmp_w2[l])
        kgain = jnp.broadcast_to(jnp.tile(nsa_k_gain[l, 0], 2)[None], (8, LANES))
        kcmp, vcmp = _compress(cmp_in, wtop, wbot, pe2, w2bd, kgain)
        nsa_o = _nsa(proj3, kcmp, vcmp, onehot, overlap, band, d0, d1, tc, expand)
        gm_o = _gmlp(proj3, gm_ws[l], jnp.repeat(gm_b[l].T, GM_DIM, axis=1))
        x2d = _outproj_ffn(x2d, ret_o.reshape(b * seq, RET_W), nsa_o.reshape(b * seq, NSA_W),
                           gm_o.reshape(b * seq, GM_W), _w_out_layout(w_out[l]), ffn_norm[l][None],
                           wgu_all, wd_all, l)
    return x2d.reshape(b, seq, D_MODEL)
```

```python
import functools
import math

import numpy as np
import jax
import jax.numpy as jnp
from jax import lax
from jax.experimental import pallas as pl
from jax.experimental.pallas import tpu as pltpu

F32 = jnp.float32
BF16 = jnp.bfloat16

D_MODEL = 1024
HEAD_DIM = 64
N_RET = 6
N_NSA = 6
N_KV = 2
N_GM = 4
GM_DIM = 64
RET_W = N_RET * HEAD_DIM
NSA_W = N_NSA * HEAD_DIM
GM_W = N_GM * GM_DIM
RET_CHUNK = 128
CMP_LEN = 32
CMP_STRIDE = 16
SEL_BLOCK = 64
N_SEL = 16
WINDOW = 512
GM_CHUNK = 128
N_BUCKETS = 32
MAX_DISTANCE = 128
ROPE_THETA = 10000.0
D_FF = 2816
EPS = 1e-6
BIG = 1e9
NEG = -1e30
LOG2E = math.log2(math.e)

LANES = 128
PROJ_W = 3328
COL_RQ, COL_RK, COL_RV, COL_RG = 0, 3, 6, 9
COL_NQ = 12
COL_CMPK, COL_CMPV, COL_SELK, COL_SELV, COL_WINK, COL_WINV = 15, 16, 17, 18, 19, 20
COL_GATE = 21
COL_GMU, COL_GMV = 22, 24
QB = 256
TK = 256
CMP_LEAD = 16
NQ_SUB = 2
NSA_PHASES = 6
FAR_TILES = 8
INPROJ_SPLIT = 2
RET_STEP = 1024
GM_STEP = 1024
VMEM_LIMIT = 56 * 1024 * 1024


def _bucket_thresholds():
    n = np.arange(0, 4 * MAX_DISTANCE)
    max_exact = N_BUCKETS // 2
    nf = np.maximum(n, 1).astype(np.float64)
    large = max_exact + np.floor(np.log(nf / max_exact) / math.log(MAX_DISTANCE / max_exact)
                                 * (N_BUCKETS - max_exact)).astype(np.int64)
    bucket = np.where(n < max_exact, n, np.minimum(large, N_BUCKETS - 1))
    return [int(np.min(n[bucket >= b])) for b in range(N_BUCKETS)]


BUCKET_THR = _bucket_thresholds()


def _cparams(sem):
    return pltpu.CompilerParams(dimension_semantics=sem, vmem_limit_bytes=VMEM_LIMIT)


def _head_ones():
    r = lax.broadcasted_iota(jnp.int32, (LANES, LANES), 0) // HEAD_DIM
    c = lax.broadcasted_iota(jnp.int32, (LANES, LANES), 1) // HEAD_DIM
    return jnp.where(r == c, 1.0, 0.0).astype(BF16)


def _head_sum(x, ones_bd):
    return jnp.dot(x.astype(BF16), ones_bd, preferred_element_type=F32)


def _head_rmsnorm(x, ones_bd):
    return x * lax.rsqrt(_head_sum(x * x, ones_bd) * (1.0 / HEAD_DIM) + EPS)


def _dot_nt(a, b):
    return lax.dot_general(a, b, (((1,), (1,)), ((), ())), preferred_element_type=F32)


def _dot_tn(a, b):
    return lax.dot_general(a, b, (((0,), (0,)), ((), ())), preferred_element_type=F32)


def _stack_heads(q, low):
    zero = jnp.zeros_like(q)
    return jnp.concatenate([jnp.where(low, q, zero), jnp.where(low, zero, q)], axis=0)


def _inproj_kernel(x_ref, g_ref, wa_ref, wb_ref, cos_ref, sin_ref, hg_ref, o_ref, grp_ref, cmp_scr, *, tm):
    sub_rows = tm // INPROJ_SPLIT
    lane = lax.broadcasted_iota(jnp.int32, (sub_rows, LANES), 1)
    first_half = (lane % HEAD_DIM) < (HEAD_DIM // 2)
    ones_bd = _head_ones()
    norm_gain = {COL_NQ: 0, COL_NQ + 1: 0, COL_NQ + 2: 0, COL_SELK: 1, COL_WINK: 2}
    deferred = []
    for r0 in range(0, tm, sub_rows):
        rows = slice(r0, r0 + sub_rows)
        x = x_ref[rows, :]
        ms = jnp.mean(x * x, axis=-1, keepdims=True)
        h = (x * lax.rsqrt(ms + EPS) * g_ref[...]).astype(BF16)
        cos = cos_ref[rows, :]
        sin = sin_ref[rows, :]
        for s in range(0, PROJ_W // LANES, 2):
            if s < COL_GMU:
                acc = _dot_nt(h, wa_ref[s * LANES:(s + 2) * LANES, :])
            else:
                acc = _dot_nt(h, wb_ref[(s - COL_GMU) * LANES:(s - COL_GMU + 2) * LANES, :])
            for sub in range(2):
                j = s + sub
                a = acc[:, sub * LANES:(sub + 1) * LANES]
                if j < COL_RV:
                    swapped = jnp.where(first_half, pltpu.roll(a, LANES - HEAD_DIM // 2, axis=1),
                                        pltpu.roll(a, HEAD_DIM // 2, axis=1))
                    a = a * cos + swapped * sin
                    if j >= COL_RK:
                        a = a * (HEAD_DIM ** -0.5)
                elif j in norm_gain:
                    deferred.append((rows, j, a))
                    continue
                elif j in (COL_CMPK, COL_CMPV):
                    cmp_scr[j - COL_CMPK, rows, :] = a
                o_ref[rows, j * LANES:(j + 1) * LANES] = a.astype(BF16)
    for rows, j, a in deferred:
        a = _head_rmsnorm(a, ones_bd) * hg_ref[norm_gain[j]:norm_gain[j] + 1, :]
        o_ref[rows, j * LANES:(j + 1) * LANES] = a.astype(BF16)
    for l in range(CMP_STRIDE):
        for kv in range(2):
            grp_ref[:, (2 * l + kv) * LANES:(2 * l + kv + 1) * LANES] = (
                cmp_scr[kv, pl.ds(l, tm // CMP_STRIDE, stride=CMP_STRIDE), :].astype(BF16))


def _inproj(x2d, gain, w_main, w_gm, cos_tab, sin_tab, head_gains, seq):
    m = x2d.shape[0]
    tm = 1024
    nt = seq // tm
    const = lambda arr: pl.BlockSpec(arr.shape, lambda i: (0, 0), pipeline_mode=pl.Buffered(1))
    return pl.pallas_call(
        functools.partial(_inproj_kernel, tm=tm),
        grid=(m // tm,),
        in_specs=[
            pl.BlockSpec((tm, D_MODEL), lambda i: (i, 0)),
            pl.BlockSpec((1, D_MODEL), lambda i: (0, 0)),
            const(w_main), const(w_gm),
            pl.BlockSpec((tm, LANES), lambda i: (i % nt, 0)),
            pl.BlockSpec((tm, LANES), lambda i: (i % nt, 0)),
            pl.BlockSpec((8, LANES), lambda i: (0, 0)),
        ],
        out_specs=[pl.BlockSpec((tm, PROJ_W), lambda i: (i, 0)),
                   pl.BlockSpec((tm // CMP_STRIDE, CMP_STRIDE * 2 * LANES), lambda i: (i, 0))],
        out_shape=[jax.ShapeDtypeStruct((m, PROJ_W), BF16),
                   jax.ShapeDtypeStruct((m // CMP_STRIDE, CMP_STRIDE * 2 * LANES), BF16)],
        scratch_shapes=[pltpu.VMEM((2, tm, LANES), F32)],
        compiler_params=_cparams(("arbitrary",)),
        name="inproj",
    )(x2d, gain, w_main, w_gm, cos_tab, sin_tab, head_gains)


def _ret_kernel(q_ref, k_ref, v_ref, g_ref, dec_ref, xi_ref, zeta_ref, cd_ref, o_ref, r_ref):
    @pl.when(pl.program_id(1) == 0)
    def _():
        r_ref[...] = jnp.zeros_like(r_ref)

    lane = lax.broadcasted_iota(jnp.int32, (RET_CHUNK, LANES), 1)
    low = lane < HEAD_DIM
    ones_bd = _head_ones()
    cells = [(c, ch) for c in range(RET_W // LANES) for ch in range(RET_STEP // RET_CHUNK)]
    sl = lambda c: slice(c * LANES, (c + 1) * LANES)
    rows = lambda ch: slice(ch * RET_CHUNK, (ch + 1) * RET_CHUNK)
    qs = {(c, ch): _stack_heads(q_ref[rows(ch), sl(c)], low) for c, ch in cells}
    scores = {(c, ch): _dot_nt(qs[c, ch], k_ref[rows(ch), sl(c)]) for c, ch in cells}
    incr = {(c, ch): _dot_tn((k_ref[rows(ch), sl(c)].astype(F32) * zeta_ref[c]).astype(BF16), v_ref[rows(ch), sl(c)])
            for c, ch in cells}
    inner = {(c, ch): jnp.dot((scores[c, ch] * dec_ref[c]).astype(BF16), v_ref[rows(ch), sl(c)],
                              preferred_element_type=F32) for c, ch in cells}
    outs = {}
    for c in range(RET_W // LANES):
        state = r_ref[c]
        for ch in range(RET_STEP // RET_CHUNK):
            cross = jnp.dot(qs[c, ch], state.astype(BF16), preferred_element_type=F32) * xi_ref[c]
            tot = inner[c, ch] + cross
            outs[c, ch] = jnp.where(low, tot[:RET_CHUNK], tot[RET_CHUNK:])
            state = cd_ref[c] * state + incr[c, ch]
        r_ref[c] = state
    for c, ch in cells:
        g = g_ref[rows(ch), sl(c)].astype(F32)
        y = _head_rmsnorm(outs[c, ch], ones_bd) * (g * jax.nn.sigmoid(g))
        o_ref[rows(ch), sl(c)] = y.astype(BF16)


def _retention(proj3, tabs):
    b, seq, _ = proj3.shape
    dec, xi, zeta, cd = tabs
    ncol = RET_W // LANES
    qkvg = [pl.BlockSpec((None, RET_STEP, RET_W), functools.partial(lambda bi, ti, col: (bi, ti, col), col=col))
            for col in range(4)]
    const3 = lambda shape: pl.BlockSpec(shape, lambda bi, ti: (0, 0, 0))
    return pl.pallas_call(
        _ret_kernel,
        grid=(b, seq // RET_STEP),
        in_specs=qkvg + [const3((ncol, 2 * RET_CHUNK, LANES)), const3((ncol, 2 * RET_CHUNK, LANES)),
                         const3((ncol, RET_CHUNK, LANES)), const3((ncol, LANES, LANES))],
        out_specs=pl.BlockSpec((None, RET_STEP, RET_W), lambda bi, ti: (bi, ti, 0)),
        out_shape=jax.ShapeDtypeStruct((b, seq, RET_W), BF16),
        scratch_shapes=[pltpu.VMEM((ncol, LANES, LANES), F32)],
        compiler_params=_cparams(("arbitrary", "arbitrary")),
        name="retention",
    )(proj3, proj3, proj3, proj3, dec, xi, zeta, cd)


def _compress_kernel(a_ref, wtop_ref, wbot_ref, pe_ref, w2_ref, kg_ref, kc_ref, vc_ref, *, ngrp):
    half = CMP_LEN // 2
    ones_bd = _head_ones()
    for kv in range(2):
        top = jnp.zeros((ngrp, LANES), F32)
        bot = jnp.zeros((ngrp, LANES), F32)
        for l in range(half):
            x = a_ref[:, l * 2 * LANES + kv * LANES:l * 2 * LANES + (kv + 1) * LANES].astype(F32)
            top = top + jnp.dot((x + pe_ref[kv, l]).astype(BF16), wtop_ref[kv, l], preferred_element_type=F32)
            bot = bot + jnp.dot((x + pe_ref[kv, half + l]).astype(BF16), wbot_ref[kv, l],
                                preferred_element_type=F32)
        hid = jax.nn.gelu(top + pltpu.roll(bot, ngrp - 1, axis=0), approximate=True)
        out = jnp.dot(hid.astype(BF16), w2_ref[kv], preferred_element_type=F32)
        if kv == 0:
            kc_ref[...] = (_head_rmsnorm(out, ones_bd) * kg_ref[0:1, :]).astype(BF16)
        else:
            vc_ref[...] = out.astype(BF16)


def _compress(a, wtop, wbot, pe2, w2bd, kgain):
    b, ngrp, width = a.shape
    full = lambda arr: pl.BlockSpec(arr.shape, lambda bi: (0,) * arr.ndim)
    out_spec = pl.BlockSpec((None, ngrp, LANES), lambda bi: (bi, 0, 0))
    return pl.pallas_call(
        functools.partial(_compress_kernel, ngrp=ngrp),
        grid=(b,),
        in_specs=[pl.BlockSpec((None, ngrp, width), lambda bi: (bi, 0, 0)),
                  full(wtop), full(wbot), full(pe2), full(w2bd), full(kgain)],
        out_specs=[out_spec, out_spec],
        out_shape=[jax.ShapeDtypeStruct((b, ngrp, LANES), BF16)] * 2,
        compiler_params=_cparams(("arbitrary",)),
        name="nsa_compress",
    )(a, wtop, wbot, pe2, w2bd, kgain)


def _bias_kernel(rb_ref, d0_ref, d1_ref, tc_ref):
    h = pl.program_id(0)
    far = rb_ref[N_BUCKETS - 1, h]

    def rel(n):
        val = jnp.full(n.shape, (rb_ref[0, h] - far) * LOG2E, F32)
        for bkt in range(1, N_BUCKETS):
            val = jnp.where(n >= BUCKET_THR[bkt], (rb_ref[bkt, h] - far) * LOG2E, val)
        return val

    ql = lax.broadcasted_iota(jnp.int32, (QB, TK), 0)
    kl = lax.broadcasted_iota(jnp.int32, (QB, TK), 1)
    d = ql - kl
    d0_ref[...] = jnp.where(d >= 0, rel(d), NEG)
    d1_ref[...] = rel(d + TK)
    ql2 = lax.broadcasted_iota(jnp.int32, (QB, LANES), 0)
    m = lax.broadcasted_iota(jnp.int32, (QB, LANES), 1)
    dc = ql2 - CMP_STRIDE * (m - CMP_LEAD) - (CMP_LEN - 1)
    tc_ref[...] = jnp.where(dc >= 0, rel(dc), NEG).astype(BF16)


def _bias_tiles(rel_bias):
    return pl.pallas_call(
        _bias_kernel,
        grid=(N_NSA,),
        in_specs=[pl.BlockSpec(memory_space=pltpu.SMEM)],
        out_specs=[pl.BlockSpec((None, QB, TK), lambda h: (h, 0, 0)),
                   pl.BlockSpec((None, QB, TK), lambda h: (h, 0, 0)),
                   pl.BlockSpec((None, QB, LANES), lambda h: (h, 0, 0))],
        out_shape=[jax.ShapeDtypeStruct((N_NSA, QB, TK), F32),
                   jax.ShapeDtypeStruct((N_NSA, QB, TK), F32),
                   jax.ShapeDtypeStruct((N_NSA, QB, LANES), BF16)],
        compiler_params=_cparams(("arbitrary",)),
        name="t5_bias_tiles",
    )(rel_bias)


def _nsa_kernel(*refs, ncmp):
    programs = [_nsa_tile(sub, *refs, ncmp=ncmp) for sub in range(NQ_SUB)]
    for _ in range(NSA_PHASES):
        for prog in programs:
            next(prog)


def _nsa_tile(sub, q_ref, gt_ref, sk_ref, sv_ref, wk_ref, wv_ref, kc_ref, vc_ref, oh_ref, ov_ref, band_ref,
              d0_ref, d1_ref, tc_ref, e_ref, o_ref, acc_ref, m_ref, *, ncmp):
    i = NQ_SUB * pl.program_id(1) + sub
    t0 = i * QB
    ncol = NSA_W // LANES
    rows = slice(sub * QB, (sub + 1) * QB)
    slot = sub * ncol
    sq = 2 * QB
    low = lax.broadcasted_iota(jnp.int32, (QB, LANES), 1) < HEAD_DIM
    qs = [_stack_heads(q_ref[rows, c * LANES:(c + 1) * LANES], low) for c in range(ncol)]

    def stacked(ref, c):
        return jnp.concatenate([ref[c], ref[c + ncol]], axis=0)

    ones_col = jnp.ones((TK, LANES), BF16)

    n_tiles = NQ_SUB * pl.num_programs(1)
    band = band_ref[pl.ds(pl.multiple_of((QB // CMP_STRIDE) * (n_tiles - i), CMP_STRIDE), ncmp), :]
    kc_aug = jnp.concatenate([kc_ref[...], band], axis=1)
    vc_ov = jnp.concatenate([vc_ref[...], ov_ref[...]], axis=1)
    imp = jnp.zeros((sq, LANES), F32)
    o_cmp = []
    s_cmp = [_dot_nt(jnp.concatenate([qs[c], stacked(tc_ref, c)], axis=1), kc_aug) for c in range(ncol)]

    start_d = pl.multiple_of(t0, TK)
    start_p = pl.multiple_of(jnp.maximum(t0 - TK, 0), TK)
    start_e = pl.multiple_of(jnp.maximum(t0 - 2 * TK, 0), TK)
    s_win = [[_dot_nt(qs[c], wk_ref[pl.ds(st, TK), :]) for st in (start_d, start_p, start_e)] for c in range(ncol)]

    for c in range(ncol):
        s = s_cmp[c]
        mx = jnp.maximum(jnp.max(s, axis=-1, keepdims=True), -1e20)
        e = jnp.exp2(s - mx)
        den = jnp.sum(e, axis=-1, keepdims=True)
        inv = jnp.where(den > 0.0, 1.0 / den, 0.0)
        both = jnp.dot(e.astype(BF16), vc_ov, preferred_element_type=F32) * inv
        imp = imp + both[:, LANES:]
        o_cmp.append(jnp.where(low, both[:QB, :LANES], both[QB:, :LANES]))
    yield

    off_prev = jnp.where(i >= 1, 0.0, -NEG)
    off_edge = jnp.where(i >= 2, 0.0, -NEG)
    ql_t = lax.broadcasted_iota(jnp.int32, (sq, TK), 0) % QB
    kl_t = lax.broadcasted_iota(jnp.int32, (sq, TK), 1)
    edge = jnp.where(kl_t > ql_t, 0.0, NEG)
    wv_all = jnp.concatenate([
        jnp.concatenate([wv_ref[pl.ds(st, TK), :], ones_col], axis=1) for st in (start_d, start_p, start_e)], axis=0)
    o_win = []
    for c in range(ncol):
        s0 = s_win[c][0] + stacked(d0_ref, c)
        s1 = s_win[c][1] + stacked(d1_ref, c)
        s2 = s_win[c][2] + edge
        mx = jnp.maximum(jnp.maximum(jnp.max(s0, axis=-1, keepdims=True),
                                     jnp.max(s1, axis=-1, keepdims=True) - off_prev),
                         jnp.max(s2, axis=-1, keepdims=True) - off_edge)
        p = jnp.exp2(jnp.concatenate([s0 - mx, s1 - (mx + off_prev), s2 - (mx + off_edge)], axis=1)).astype(BF16)
        ow = jnp.dot(p, wv_all, preferred_element_type=F32)
        o_win.append(jnp.where(low, ow[:QB, :LANES] / ow[:QB, LANES:], ow[QB:, :LANES] / ow[QB:, LANES:]))
    yield

    imp_t = imp.T
    blk = lax.broadcasted_iota(jnp.int32, (LANES, sq), 0)
    cur = (t0 + lax.broadcasted_iota(jnp.int32, (LANES, sq), 1) % QB) // SEL_BLOCK
    forced = (blk == 0) | (blk == cur) | (blk == cur - 1)
    imp_t = jnp.where(forced, -jnp.inf, jnp.where(blk > cur, -BIG, imp_t))
    blk_f = blk.astype(F32)

    val = imp_t
    for _ in range(N_SEL - 3):
        top = jnp.max(val, axis=0, keepdims=True)
        idx = jnp.min(jnp.where(val == top, blk_f, 1e6), axis=0, keepdims=True)
        val = jnp.where(blk_f == idx, -jnp.inf, val)
    msel = jnp.where(val == -jnp.inf, 0.0, NEG).T.astype(BF16)
    yield

    qaug = [jnp.concatenate([qs[c], msel], axis=1) for c in range(ncol)]

    def sel_scores(c, start, width):
        kaug = jnp.concatenate([sk_ref[pl.ds(start, width), :], oh_ref[pl.ds(start, width), :]], axis=1)
        return _dot_nt(qaug[c], kaug)

    def sel_vaug(start, width):
        return jnp.concatenate([sv_ref[pl.ds(start, width), :], jnp.ones((width, LANES), BF16)], axis=1)

    def lanes(x, width):
        return jnp.concatenate([x] * (width // LANES), axis=1)

    s_near = [(sel_scores(c, start_d, TK), sel_scores(c, start_p, TK)) for c in range(ncol)]
    for c in range(ncol):
        s0 = s_near[c][0] + stacked(d0_ref, c)
        s1 = s_near[c][1] + stacked(d1_ref, c)
        mx = jnp.maximum(jnp.max(s0, axis=-1, keepdims=True), jnp.max(s1, axis=-1, keepdims=True) - off_prev)
        p0 = jnp.exp2(s0 - mx).astype(BF16)
        p1 = jnp.exp2(s1 - (mx + off_prev)).astype(BF16)
        acc_ref[slot + c] = (jnp.dot(p0, sel_vaug(start_d, TK), preferred_element_type=F32)
                             + jnp.dot(p1, sel_vaug(start_p, TK), preferred_element_type=F32))
        m_ref[slot + c] = jnp.broadcast_to(mx, (sq, LANES))
    yield

    def far_tile(kt0, tiles):
        start = pl.multiple_of(kt0 * TK, tiles * TK)
        width = tiles * TK
        vaug = sel_vaug(start, width)
        scores = [sel_scores(c, start, width) for c in range(ncol)]
        for c in range(ncol):
            s = scores[c]
            m_old = m_ref[slot + c]
            m_new = jnp.maximum(m_old, jnp.max(s, axis=-1, keepdims=True))
            alpha = jnp.exp2(m_old - m_new)
            p = jnp.exp2(s - lanes(m_new, width)).astype(BF16)
            acc_ref[slot + c] = (lanes(alpha, 2 * LANES) * acc_ref[slot + c]
                                 + jnp.dot(p, vaug, preferred_element_type=F32))
            m_ref[slot + c] = m_new

    n_far = jnp.maximum(i - 1, 0)

    def far_loop(tiles):
        def body(kt, carry):
            far_tile(kt * tiles, tiles)
            return carry
        return body

    tiles, done = FAR_TILES, 0
    while tiles >= 1:
        count = n_far // tiles
        lax.fori_loop(done, count, far_loop(tiles), 0)
        tiles, done = tiles // 2, count * 2
    yield

    gate = jax.nn.sigmoid(gt_ref[rows, :].astype(F32)).astype(BF16)
    gexp = jnp.dot(gate, e_ref[...], preferred_element_type=F32)
    for c in range(ncol):
        acc = acc_ref[slot + c]
        o_sel = jnp.where(low, acc[:QB, :LANES] / acc[:QB, LANES:], acc[QB:, :LANES] / acc[QB:, LANES:])
        gc = lambda br: gexp[:, (br * ncol + c) * LANES:(br * ncol + c + 1) * LANES]
        o_ref[rows, c * LANES:(c + 1) * LANES] = (gc(0) * o_cmp[c] + gc(1) * o_sel + gc(2) * o_win[c]).astype(BF16)
    yield


def _nsa(proj3, kcmp, vcmp, onehot, overlap, band, d0, d1, tc, expand):
    b, seq, _ = proj3.shape
    ncmp = kcmp.shape[1]
    step = NQ_SUB * QB
    col = lambda width, cidx: pl.BlockSpec((None, step, width), lambda bi, qi: (bi, qi, cidx))
    res = lambda cidx: pl.BlockSpec((None, seq, LANES), lambda bi, qi: (bi, 0, cidx))
    full = lambda arr: pl.BlockSpec(arr.shape, lambda bi, qi: (0,) * arr.ndim)
    cmp_spec = pl.BlockSpec((None, ncmp, LANES), lambda bi, qi: (bi, 0, 0))
    return pl.pallas_call(
        functools.partial(_nsa_kernel, ncmp=ncmp),
        grid=(b, seq // step),
        in_specs=[col(NSA_W, COL_NQ * LANES // NSA_W), col(LANES, COL_GATE),
                  res(COL_SELK), res(COL_SELV), res(COL_WINK), res(COL_WINV),
                  cmp_spec, cmp_spec, full(onehot), full(overlap), full(band), full(d0), full(d1), full(tc),
                  full(expand)],
        out_specs=pl.BlockSpec((None, step, NSA_W), lambda bi, qi: (bi, qi, 0)),
        out_shape=jax.ShapeDtypeStruct((b, seq, NSA_W), BF16),
        scratch_shapes=[pltpu.VMEM((NQ_SUB * NSA_W // LANES, 2 * QB, 2 * LANES), F32),
                        pltpu.VMEM((NQ_SUB * NSA_W // LANES, 2 * QB, LANES), F32)],
        compiler_params=_cparams(("arbitrary", "arbitrary")),
        name="nsa_attention",
    )(proj3, proj3, proj3, proj3, proj3, proj3, kcmp, vcmp, onehot, overlap, band, d0, d1, tc, expand)


def _gmlp_kernel(u_ref, v_ref, ws_ref, b_ref, o_ref):
    lane = lax.broadcasted_iota(jnp.int32, (GM_CHUNK, LANES), 1)
    low = lane < GM_DIM
    row = lax.broadcasted_iota(jnp.int32, (GM_CHUNK, GM_CHUNK), 0)
    colm = lax.broadcasted_iota(jnp.int32, (GM_CHUNK, GM_CHUNK), 1)
    tril = colm <= row
    ones_bd = _head_ones()
    cells = [(cc, ch) for cc in range(GM_W // LANES) for ch in range(GM_STEP // GM_CHUNK)]
    sl = lambda cc: slice(cc * LANES, (cc + 1) * LANES)
    rows = lambda ch: slice(ch * GM_CHUNK, (ch + 1) * GM_CHUNK)
    w = [jnp.concatenate([jnp.where(tril, ws_ref[2 * cc], 0.0), jnp.where(tril, ws_ref[2 * cc + 1], 0.0)],
                         axis=0).astype(BF16) for cc in range(GM_W // LANES)]
    v = {(cc, ch): jax.nn.gelu(v_ref[rows(ch), sl(cc)].astype(F32), approximate=True) for cc, ch in cells}
    sums = {cell: _head_sum(v[cell] * v[cell], ones_bd) for cell in cells}
    vn = {cell: (v[cell] * lax.rsqrt(sums[cell] * (1.0 / GM_DIM) + EPS)).astype(BF16) for cell in cells}
    mixed = {(cc, ch): jnp.dot(w[cc], vn[cc, ch], preferred_element_type=F32) for cc, ch in cells}
    for cc, ch in cells:
        r = mixed[cc, ch]
        sv = jnp.where(low, r[:GM_CHUNK], r[GM_CHUNK:]) + b_ref[:, sl(cc)]
        u = jax.nn.gelu(u_ref[rows(ch), sl(cc)].astype(F32), approximate=True)
        o_ref[rows(ch), sl(cc)] = (u * sv).astype(BF16)


def _gmlp(proj3, ws, bexp):
    b, seq, _ = proj3.shape
    return pl.pallas_call(
        _gmlp_kernel,
        grid=(b, seq // GM_STEP),
        in_specs=[pl.BlockSpec((None, GM_STEP, GM_W), lambda bi, ti: (bi, ti, COL_GMU * LANES // GM_W)),
                  pl.BlockSpec((None, GM_STEP, GM_W), lambda bi, ti: (bi, ti, COL_GMV * LANES // GM_W)),
                  pl.BlockSpec((N_GM, GM_CHUNK, GM_CHUNK), lambda bi, ti: (0, 0, 0)),
                  pl.BlockSpec((GM_CHUNK, GM_W), lambda bi, ti: (0, 0))],
        out_specs=pl.BlockSpec((None, GM_STEP, GM_W), lambda bi, ti: (bi, ti, 0)),
        out_shape=jax.ShapeDtypeStruct((b, seq, GM_W), BF16),
        compiler_params=_cparams(("arbitrary", "arbitrary")),
        name="gmlp",
    )(proj3, proj3, ws, bexp)


FFN_CHUNK = 256
FFN_SPLIT = 2


def _ffn_kernel(x_ref, r_ref, n_ref, m_ref, wo_ref, g_ref, wgu_ref, wd_ref, o_ref, *, tm):
    sub_rows = tm // FFN_SPLIT
    for r0 in range(0, tm, sub_rows):
        rows = slice(r0, r0 + sub_rows)
        mixed = jnp.concatenate([r_ref[rows, :], n_ref[rows, :], m_ref[rows, :]], axis=1)
        x = x_ref[rows, :] + jnp.dot(mixed, wo_ref[...], preferred_element_type=F32)
        ms = jnp.mean(x * x, axis=-1, keepdims=True)
        h = (x * lax.rsqrt(ms + EPS) * g_ref[...]).astype(BF16)
        acc = x
        for f in range(0, D_FF, FFN_CHUNK):
            gate = jnp.dot(h, wgu_ref[:, f:f + FFN_CHUNK], preferred_element_type=F32)
            up = jnp.dot(h, wgu_ref[:, D_FF + f:D_FF + f + FFN_CHUNK], preferred_element_type=F32)
            act = (gate * jax.nn.sigmoid(gate) * up).astype(BF16)
            acc = acc + jnp.dot(act, wd_ref[f:f + FFN_CHUNK, :], preferred_element_type=F32)
        o_ref[rows, :] = acc


def _outproj_ffn(x2d, ret_o, nsa_o, gm_o, w_out, gain, wgu, wd, layer):
    m = x2d.shape[0]
    tm = 1024
    row = lambda width: pl.BlockSpec((tm, width), lambda i: (i, 0))
    const = lambda arr: pl.BlockSpec(arr.shape, lambda i: (0, 0), pipeline_mode=pl.Buffered(1))
    of_layer = lambda arr: pl.BlockSpec((None,) + arr.shape[1:], lambda i: (layer, 0, 0),
                                        pipeline_mode=pl.Buffered(1))
    return pl.pallas_call(
        functools.partial(_ffn_kernel, tm=tm),
        grid=(m // tm,),
        in_specs=[row(D_MODEL), row(RET_W), row(NSA_W), row(GM_W), const(w_out),
                  pl.BlockSpec((1, D_MODEL), lambda i: (0, 0)),
                  of_layer(wgu), of_layer(wd)],
        out_specs=row(D_MODEL),
        out_shape=jax.ShapeDtypeStruct((m, D_MODEL), F32),
        compiler_params=_cparams(("arbitrary",)),
        name="outproj_ffn",
    )(x2d, ret_o, nsa_o, gm_o, w_out, gain, wgu, wd)


def _nq_head_order():
    return [c + (N_NSA // N_KV) * half for c in range(NSA_W // LANES) for half in range(2)]


def _w_in_layout(w):
    wt = w.T
    off_nq = 4 * RET_W
    off_kv = off_nq + NSA_W
    off_gate = off_kv + 3 * 2 * N_KV * HEAD_DIM
    off_gm = off_gate + 3 * N_NSA
    heads = [wt[off_nq + h * HEAD_DIM:off_nq + (h + 1) * HEAD_DIM] for h in _nq_head_order()]
    main = jnp.concatenate([wt[:off_nq]] + heads + [wt[off_kv:off_gate + LANES]], axis=0).astype(BF16)
    return main, wt[off_gm:].astype(BF16)


def _w_out_layout(w):
    heads = [w[RET_W + h * HEAD_DIM:RET_W + (h + 1) * HEAD_DIM] for h in _nq_head_order()]
    return jnp.concatenate([w[:RET_W]] + heads + [w[RET_W + NSA_W:]], axis=0).astype(BF16)


def _gate_expand():
    e = np.zeros((LANES, 3 * NSA_W), np.float32)
    ncol = NSA_W // LANES
    for br in range(3):
        for c in range(ncol):
            for half in range(2):
                head = c + (N_NSA // N_KV) * half
                dst = (br * ncol + c) * LANES + half * HEAD_DIM
                e[head * 3 + br, dst:dst + HEAD_DIM] = 1.0
    return jnp.asarray(e, BF16)


def _rope_tables(seq):
    half = HEAD_DIM // 2
    inv = ROPE_THETA ** (-np.arange(half, dtype=np.float64) / half)
    ang = np.arange(seq, dtype=np.float64)[:, None] * inv[None, :]
    cos = np.tile(np.cos(ang), (1, LANES // half))
    sin = np.tile(np.concatenate([-np.sin(ang), np.sin(ang)], axis=1), (1, LANES // HEAD_DIM))
    return jnp.asarray(cos, F32), jnp.asarray(sin, F32)


def _retention_tables():
    c = RET_CHUNK
    log_gamma = np.log(1.0 - 2.0 ** (-5.0 - np.arange(N_RET, dtype=np.float64)))
    idx = np.arange(c, dtype=np.float64)
    diff = idx[:, None] - idx[None, :]
    decay = np.where(diff >= 0, np.exp(np.maximum(diff, 0.0)[None] * log_gamma[:, None, None]), 0.0)
    zeta = np.exp((c - 1 - idx)[None, :] * log_gamma[:, None])
    xi = np.exp((idx + 1)[None, :] * log_gamma[:, None])
    chunk_decay = np.exp(c * log_gamma)
    ncol = RET_W // LANES
    dec = decay.reshape(ncol, 2 * c, c)
    xi_st = np.broadcast_to(xi.reshape(ncol, 2 * c, 1), (ncol, 2 * c, LANES))
    zeta_l = np.repeat(zeta.reshape(ncol, 2, c).transpose(0, 2, 1), HEAD_DIM, axis=2)
    cd = np.broadcast_to(np.repeat(chunk_decay.reshape(ncol, 2), HEAD_DIM, axis=1)[:, :, None],
                         (ncol, LANES, LANES))
    return tuple(jnp.asarray(t, F32) for t in (dec, xi_st, zeta_l, cd))


def _overlap_table(seq):
    n_cmp = (seq - CMP_LEN) // CMP_STRIDE + 1
    n_slc = seq // SEL_BLOCK
    cs = np.arange(n_cmp)[:, None] * CMP_STRIDE
    ss = np.arange(n_slc)[None, :] * SEL_BLOCK
    ov = np.clip(np.minimum(cs + CMP_LEN, ss + SEL_BLOCK) - np.maximum(cs, ss), 0, None) // CMP_STRIDE
    full = np.zeros((seq // CMP_STRIDE, LANES), np.float32)
    full[:n_cmp, :n_slc] = ov
    return jnp.asarray(full, BF16)


def _cmp_band(seq):
    nq = seq // QB
    ncmp = seq // CMP_STRIDE
    x = np.arange(2 * ncmp)[:, None]
    m = np.arange(LANES)[None, :]
    base = (QB // CMP_STRIDE) * nq - CMP_LEAD
    band = np.where(m < LANES - 1, x == m + base, x >= m + base)
    return jnp.asarray(band.astype(np.float32), BF16)


def _block_onehot(seq):
    oh = (np.arange(seq)[:, None] // SEL_BLOCK) == np.arange(LANES)[None, :]
    return jnp.asarray(oh.astype(np.float32), BF16)


def _compress_weights(cmp_pe, cmp_w1, cmp_w2):
    def block_diag2(w):
        zero = jnp.zeros_like(w)
        return jnp.concatenate([jnp.concatenate([w, zero], axis=-1), jnp.concatenate([zero, w], axis=-1)],
                               axis=-2).astype(BF16)

    w1bd = block_diag2(cmp_w1.reshape(2, CMP_LEN, HEAD_DIM, HEAD_DIM))
    w2bd = block_diag2(cmp_w2)
    pe2 = jnp.tile(cmp_pe, (1, 1, N_KV))[:, :, None, :]
    return w1bd[:, :CMP_LEN // 2], w1bd[:, CMP_LEN // 2:], pe2, w2bd


def kernel(x, attn_norm, w_in, w_out, nsa_q_gain, nsa_k_gain, cmp_pe, cmp_w1, cmp_w2, gm_ws, gm_b, ffn_norm,
           w_gate_up, w_down, rel_bias):
    b, seq, _ = x.shape
    assert seq % (2 * TK) == 0 and N_SEL <= seq // SEL_BLOCK <= LANES, "selection blocks must fit one lane row"
    depth = w_in.shape[0]
    cos_tab, sin_tab = _rope_tables(seq)
    ret_tabs = _retention_tables()
    overlap = _overlap_table(seq)
    onehot = _block_onehot(seq)
    band = _cmp_band(seq)
    expand = _gate_expand()
    d0, d1, tc = _bias_tiles(rel_bias)
    ngrp = seq // CMP_STRIDE
    wgu_all = w_gate_up.astype(BF16)
    wd_all = w_down.astype(BF16)
    x2d = x.reshape(b * seq, D_MODEL)
    for l in range(depth):
        zero_row = jnp.zeros((5, LANES), F32)
        head_gains = jnp.concatenate([jnp.tile(nsa_q_gain[l] * (HEAD_DIM ** -0.5 * LOG2E), 2)[None],
                                      jnp.tile(nsa_k_gain[l, 1], 2)[None],
                                      jnp.tile(nsa_k_gain[l, 2], 2)[None], zero_row], axis=0)
        proj, cmp_grp = _inproj(x2d, attn_norm[l][None], *_w_in_layout(w_in[l]), cos_tab, sin_tab, head_gains, seq)
        proj3 = proj.reshape(b, seq, PROJ_W)
        ret_o = _retention(proj3, ret_tabs)
        cmp_in = cmp_grp.reshape(b, ngrp, CMP_STRIDE * 2 * LANES)
        wtop, wbot, pe2, w2bd = _compress_weights(cmp_pe[l], cmp_w1[l], cmp_w2[l])
        kgain = jnp.broadcast_to(jnp.tile(nsa_k_gain[l, 0], 2)[None], (8, LANES))
        kcmp, vcmp = _compress(cmp_in, wtop, wbot, pe2, w2bd, kgain)
        nsa_o = _nsa(proj3, kcmp, vcmp, onehot, overlap, band, d0, d1, tc, expand)
        gm_o = _gmlp(proj3, gm_ws[l], jnp.repeat(gm_b[l].T, GM_DIM, axis=1))
        x2d = _outproj_ffn(x2d, ret_o.reshape(b * seq, RET_W), nsa_o.reshape(b * seq, NSA_W),
                           gm_o.reshape(b * seq, GM_W), _w_out_layout(w_out[l]), ffn_norm[l][None],
                           wgu_all, wd_all, l)
    return x2d.reshape(b, seq, D_MODEL)
```

```python
import functools
import math

import numpy as np
import jax
import jax.numpy as jnp
from jax import lax
from jax.experimental import pallas as pl
from jax.experimental.pallas import tpu as pltpu

F32 = jnp.float32
BF16 = jnp.bfloat16

D_MODEL = 1024
HEAD_DIM = 64
N_RET = 6
N_NSA = 6
N_KV = 2
N_GM = 4
GM_DIM = 64
RET_W = N_RET * HEAD_DIM
NSA_W = N_NSA * HEAD_DIM
GM_W = N_GM * GM_DIM
RET_CHUNK = 128
CMP_LEN = 32
CMP_STRIDE = 16
SEL_BLOCK = 64
N_SEL = 16
WINDOW = 512
GM_CHUNK = 128
N_BUCKETS = 32
MAX_DISTANCE = 128
ROPE_THETA = 10000.0
D_FF = 2816
EPS = 1e-6
BIG = 1e9
NEG = -1e30
LOG2E = math.log2(math.e)

LANES = 128
PROJ_W = 3328
COL_RQ, COL_RK, COL_RV, COL_RG = 0, 3, 6, 9
COL_NQ = 12
COL_CMPK, COL_CMPV, COL_SELK, COL_SELV, COL_WINK, COL_WINV = 15, 16, 17, 18, 19, 20
COL_GATE = 21
COL_GMU, COL_GMV = 22, 24
QB = 256
TK = 256
CMP_LEAD = 16
NQ_SUB = 2
NSA_PHASES = 6
FAR_TILES = 8
INPROJ_SPLIT = 1
RET_STEP = 1024
GM_STEP = 1024
VMEM_LIMIT = 56 * 1024 * 1024


def _bucket_thresholds():
    n = np.arange(0, 4 * MAX_DISTANCE)
    max_exact = N_BUCKETS // 2
    nf = np.maximum(n, 1).astype(np.float64)
    large = max_exact + np.floor(np.log(nf / max_exact) / math.log(MAX_DISTANCE / max_exact)
                                 * (N_BUCKETS - max_exact)).astype(np.int64)
    bucket = np.where(n < max_exact, n, np.minimum(large, N_BUCKETS - 1))
    return [int(np.min(n[bucket >= b])) for b in range(N_BUCKETS)]


BUCKET_THR = _bucket_thresholds()


def _cparams(sem):
    return pltpu.CompilerParams(dimension_semantics=sem, vmem_limit_bytes=VMEM_LIMIT)


def _head_ones():
    r = lax.broadcasted_iota(jnp.int32, (LANES, LANES), 0) // HEAD_DIM
    c = lax.broadcasted_iota(jnp.int32, (LANES, LANES), 1) // HEAD_DIM
    return jnp.where(r == c, 1.0, 0.0).astype(BF16)


def _head_sum(x, ones_bd):
    return jnp.dot(x.astype(BF16), ones_bd, preferred_element_type=F32)


def _head_rmsnorm(x, ones_bd):
    return x * lax.rsqrt(_head_sum(x * x, ones_bd) * (1.0 / HEAD_DIM) + EPS)


def _dot_nt(a, b):
    return lax.dot_general(a, b, (((1,), (1,)), ((), ())), preferred_element_type=F32)


def _dot_tn(a, b):
    return lax.dot_general(a, b, (((0,), (0,)), ((), ())), preferred_element_type=F32)


def _stack_heads(q, low):
    zero = jnp.zeros_like(q)
    return jnp.concatenate([jnp.where(low, q, zero), jnp.where(low, zero, q)], axis=0)


def _inproj_kernel(x_ref, g_ref, wa_ref, wb_ref, cos_ref, sin_ref, hg_ref, o_ref, grp_ref, cmp_scr, *, tm):
    sub_rows = tm // INPROJ_SPLIT
    lane = lax.broadcasted_iota(jnp.int32, (sub_rows, LANES), 1)
    first_half = (lane % HEAD_DIM) < (HEAD_DIM // 2)
    ones_bd = _head_ones()
    norm_gain = {COL_NQ: 0, COL_NQ + 1: 0, COL_NQ + 2: 0, COL_SELK: 1, COL_WINK: 2}
    deferred = []
    for r0 in range(0, tm, sub_rows):
        rows = slice(r0, r0 + sub_rows)
        x = x_ref[rows, :]
        ms = jnp.mean(x * x, axis=-1, keepdims=True)
        h = (x * lax.rsqrt(ms + EPS) * g_ref[...]).astype(BF16)
        cos = cos_ref[rows, :]
        sin = sin_ref[rows, :]
        for s in range(0, PROJ_W // LANES, 2):
            if s < COL_GMU:
                acc = _dot_nt(h, wa_ref[s * LANES:(s + 2) * LANES, :])
            else:
                acc = _dot_nt(h, wb_ref[(s - COL_GMU) * LANES:(s - COL_GMU + 2) * LANES, :])
            for sub in range(2):
                j = s + sub
                a = acc[:, sub * LANES:(sub + 1) * LANES]
                if j < COL_RV:
                    swapped = jnp.where(first_half, pltpu.roll(a, LANES - HEAD_DIM // 2, axis=1),
                                        pltpu.roll(a, HEAD_DIM // 2, axis=1))
                    a = a * cos + swapped * sin
                    if j >= COL_RK:
                        a = a * (HEAD_DIM ** -0.5)
                elif j in norm_gain:
                    deferred.append((rows, j, a))
                    continue
                elif j in (COL_CMPK, COL_CMPV):
                    cmp_scr[j - COL_CMPK, rows, :] = a
                o_ref[rows, j * LANES:(j + 1) * LANES] = a.astype(BF16)
    for rows, j, a in deferred:
        a = _head_rmsnorm(a, ones_bd) * hg_ref[norm_gain[j]:norm_gain[j] + 1, :]
        o_ref[rows, j * LANES:(j + 1) * LANES] = a.astype(BF16)
    for l in range(CMP_STRIDE):
        for kv in range(2):
            grp_ref[:, (2 * l + kv) * LANES:(2 * l + kv + 1) * LANES] = (
                cmp_scr[kv, pl.ds(l, tm // CMP_STRIDE, stride=CMP_STRIDE), :].astype(BF16))


def _inproj(x2d, gain, w_main, w_gm, cos_tab, sin_tab, head_gains, seq):
    m = x2d.shape[0]
    tm = 1024
    nt = seq // tm
    const = lambda arr: pl.BlockSpec(arr.shape, lambda i: (0, 0), pipeline_mode=pl.Buffered(1))
    return pl.pallas_call(
        functools.partial(_inproj_kernel, tm=tm),
        grid=(m // tm,),
        in_specs=[
            pl.BlockSpec((tm, D_MODEL), lambda i: (i, 0)),
            pl.BlockSpec((1, D_MODEL), lambda i: (0, 0)),
            const(w_main), const(w_gm),
            pl.BlockSpec((tm, LANES), lambda i: (i % nt, 0)),
            pl.BlockSpec((tm, LANES), lambda i: (i % nt, 0)),
            pl.BlockSpec((8, LANES), lambda i: (0, 0)),
        ],
        out_specs=[pl.BlockSpec((tm, PROJ_W), lambda i: (i, 0)),
                   pl.BlockSpec((tm // CMP_STRIDE, CMP_STRIDE * 2 * LANES), lambda i: (i, 0))],
        out_shape=[jax.ShapeDtypeStruct((m, PROJ_W), BF16),
                   jax.ShapeDtypeStruct((m // CMP_STRIDE, CMP_STRIDE * 2 * LANES), BF16)],
        scratch_shapes=[pltpu.VMEM((2, tm, LANES), F32)],
        compiler_params=_cparams(("arbitrary",)),
        name="inproj",
    )(x2d, gain, w_main, w_gm, cos_tab, sin_tab, head_gains)


def _ret_kernel(q_ref, k_ref, v_ref, g_ref, dec_ref, xi_ref, zeta_ref, cd_ref, o_ref, r_ref):
    @pl.when(pl.program_id(1) == 0)
    def _():
        r_ref[...] = jnp.zeros_like(r_ref)

    lane = lax.broadcasted_iota(jnp.int32, (RET_CHUNK, LANES), 1)
    low = lane < HEAD_DIM
    ones_bd = _head_ones()
    cells = [(c, ch) for c in range(RET_W // LANES) for ch in range(RET_STEP // RET_CHUNK)]
    sl = lambda c: slice(c * LANES, (c + 1) * LANES)
    rows = lambda ch: slice(ch * RET_CHUNK, (ch + 1) * RET_CHUNK)
    qs = {(c, ch): _stack_heads(q_ref[rows(ch), sl(c)], low) for c, ch in cells}
    scores = {(c, ch): _dot_nt(qs[c, ch], k_ref[rows(ch), sl(c)]) for c, ch in cells}
    incr = {(c, ch): _dot_tn((k_ref[rows(ch), sl(c)].astype(F32) * zeta_ref[c]).astype(BF16), v_ref[rows(ch), sl(c)])
            for c, ch in cells}
    inner = {(c, ch): jnp.dot((scores[c, ch] * dec_ref[c]).astype(BF16), v_ref[rows(ch), sl(c)],
                              preferred_element_type=F32) for c, ch in cells}
    outs = {}
    for c in range(RET_W // LANES):
        state = r_ref[c]
        for ch in range(RET_STEP // RET_CHUNK):
            cross = jnp.dot(qs[c, ch], state.astype(BF16), preferred_element_type=F32) * xi_ref[c]
            tot = inner[c, ch] + cross
            outs[c, ch] = jnp.where(low, tot[:RET_CHUNK], tot[RET_CHUNK:])
            state = cd_ref[c] * state + incr[c, ch]
        r_ref[c] = state
    for c, ch in cells:
        g = g_ref[rows(ch), sl(c)].astype(F32)
        y = _head_rmsnorm(outs[c, ch], ones_bd) * (g * jax.nn.sigmoid(g))
        o_ref[rows(ch), sl(c)] = y.astype(BF16)


def _retention(proj3, tabs):
    b, seq, _ = proj3.shape
    dec, xi, zeta, cd = tabs
    ncol = RET_W // LANES
    qkvg = [pl.BlockSpec((None, RET_STEP, RET_W), functools.partial(lambda bi, ti, col: (bi, ti, col), col=col))
            for col in range(4)]
    const3 = lambda shape: pl.BlockSpec(shape, lambda bi, ti: (0, 0, 0))
    return pl.pallas_call(
        _ret_kernel,
        grid=(b, seq // RET_STEP),
        in_specs=qkvg + [const3((ncol, 2 * RET_CHUNK, LANES)), const3((ncol, 2 * RET_CHUNK, LANES)),
                         const3((ncol, RET_CHUNK, LANES)), const3((ncol, LANES, LANES))],
        out_specs=pl.BlockSpec((None, RET_STEP, RET_W), lambda bi, ti: (bi, ti, 0)),
        out_shape=jax.ShapeDtypeStruct((b, seq, RET_W), BF16),
        scratch_shapes=[pltpu.VMEM((ncol, LANES, LANES), F32)],
        compiler_params=_cparams(("arbitrary", "arbitrary")),
        name="retention",
    )(proj3, proj3, proj3, proj3, dec, xi, zeta, cd)


def _compress_kernel(a_ref, wtop_ref, wbot_ref, pe_ref, w2_ref, kg_ref, kc_ref, vc_ref, *, ngrp):
    half = CMP_LEN // 2
    ones_bd = _head_ones()
    for kv in range(2):
        top = jnp.zeros((ngrp, LANES), F32)
        bot = jnp.zeros((ngrp, LANES), F32)
        for l in range(half):
            x = a_ref[:, l * 2 * LANES + kv * LANES:l * 2 * LANES + (kv + 1) * LANES].astype(F32)
            top = top + jnp.dot((x + pe_ref[kv, l]).astype(BF16), wtop_ref[kv, l], preferred_element_type=F32)
            bot = bot + jnp.dot((x + pe_ref[kv, half + l]).astype(BF16), wbot_ref[kv, l],
                                preferred_element_type=F32)
        hid = jax.nn.gelu(top + pltpu.roll(bot, ngrp - 1, axis=0), approximate=True)
        out = jnp.dot(hid.astype(BF16), w2_ref[kv], preferred_element_type=F32)
        if kv == 0:
            kc_ref[...] = (_head_rmsnorm(out, ones_bd) * kg_ref[0:1, :]).astype(BF16)
        else:
            vc_ref[...] = out.astype(BF16)


def _compress(a, wtop, wbot, pe2, w2bd, kgain):
    b, ngrp, width = a.shape
    full = lambda arr: pl.BlockSpec(arr.shape, lambda bi: (0,) * arr.ndim)
    out_spec = pl.BlockSpec((None, ngrp, LANES), lambda bi: (bi, 0, 0))
    return pl.pallas_call(
        functools.partial(_compress_kernel, ngrp=ngrp),
        grid=(b,),
        in_specs=[pl.BlockSpec((None, ngrp, width), lambda bi: (bi, 0, 0)),
                  full(wtop), full(wbot), full(pe2), full(w2bd), full(kgain)],
        out_specs=[out_spec, out_spec],
        out_shape=[jax.ShapeDtypeStruct((b, ngrp, LANES), BF16)] * 2,
        compiler_params=_cparams(("arbitrary",)),
        name="nsa_compress",
    )(a, wtop, wbot, pe2, w2bd, kgain)


def _bias_kernel(rb_ref, d0_ref, d1_ref, tc_ref):
    h = pl.program_id(0)
    far = rb_ref[N_BUCKETS - 1, h]

    def rel(n):
        val = jnp.full(n.shape, (rb_ref[0, h] - far) * LOG2E, F32)
        for bkt in range(1, N_BUCKETS):
            val = jnp.where(n >= BUCKET_THR[bkt], (rb_ref[bkt, h] - far) * LOG2E, val)
        return val

    ql = lax.broadcasted_iota(jnp.int32, (QB, TK), 0)
    kl = lax.broadcasted_iota(jnp.int32, (QB, TK), 1)
    d = ql - kl
    d0_ref[...] = jnp.where(d >= 0, rel(d), NEG)
    d1_ref[...] = rel(d + TK)
    ql2 = lax.broadcasted_iota(jnp.int32, (QB, LANES), 0)
    m = lax.broadcasted_iota(jnp.int32, (QB, LANES), 1)
    dc = ql2 - CMP_STRIDE * (m - CMP_LEAD) - (CMP_LEN - 1)
    tc_ref[...] = jnp.where(dc >= 0, rel(dc), NEG).astype(BF16)


def _bias_tiles(rel_bias):
    return pl.pallas_call(
        _bias_kernel,
        grid=(N_NSA,),
        in_specs=[pl.BlockSpec(memory_space=pltpu.SMEM)],
        out_specs=[pl.BlockSpec((None, QB, TK), lambda h: (h, 0, 0)),
                   pl.BlockSpec((None, QB, TK), lambda h: (h, 0, 0)),
                   pl.BlockSpec((None, QB, LANES), lambda h: (h, 0, 0))],
        out_shape=[jax.ShapeDtypeStruct((N_NSA, QB, TK), F32),
                   jax.ShapeDtypeStruct((N_NSA, QB, TK), F32),
                   jax.ShapeDtypeStruct((N_NSA, QB, LANES), BF16)],
        compiler_params=_cparams(("arbitrary",)),
        name="t5_bias_tiles",
    )(rel_bias)


def _nsa_kernel(*refs, ncmp):
    programs = [_nsa_tile(sub, *refs, ncmp=ncmp) for sub in range(NQ_SUB)]
    for _ in range(NSA_PHASES):
        for prog in programs:
            next(prog)


def _nsa_tile(sub, q_ref, gt_ref, sk_ref, sv_ref, wk_ref, wv_ref, kc_ref, vc_ref, oh_ref, ov_ref, band_ref,
              d0_ref, d1_ref, tc_ref, e_ref, o_ref, acc_ref, m_ref, *, ncmp):
    i = NQ_SUB * pl.program_id(1) + sub
    t0 = i * QB
    ncol = NSA_W // LANES
    rows = slice(sub * QB, (sub + 1) * QB)
    slot = sub * ncol
    sq = 2 * QB
    low = lax.broadcasted_iota(jnp.int32, (QB, LANES), 1) < HEAD_DIM
    qs = [_stack_heads(q_ref[rows, c * LANES:(c + 1) * LANES], low) for c in range(ncol)]

    def stacked(ref, c):
        return jnp.concatenate([ref[c], ref[c + ncol]], axis=0)

    ones_col = jnp.ones((TK, LANES), BF16)

    n_tiles = NQ_SUB * pl.num_programs(1)
    band = band_ref[pl.ds(pl.multiple_of((QB // CMP_STRIDE) * (n_tiles - i), CMP_STRIDE), ncmp), :]
    kc_aug = jnp.concatenate([kc_ref[...], band], axis=1)
    vc_ov = jnp.concatenate([vc_ref[...], ov_ref[...]], axis=1)
    imp = jnp.zeros((sq, LANES), F32)
    o_cmp = []
    s_cmp = [_dot_nt(jnp.concatenate([qs[c], stacked(tc_ref, c)], axis=1), kc_aug) for c in range(ncol)]

    start_d = pl.multiple_of(t0, TK)
    start_p = pl.multiple_of(jnp.maximum(t0 - TK, 0), TK)
    start_e = pl.multiple_of(jnp.maximum(t0 - 2 * TK, 0), TK)
    s_win = [[_dot_nt(qs[c], wk_ref[pl.ds(st, TK), :]) for st in (start_d, start_p, start_e)] for c in range(ncol)]

    for c in range(ncol):
        s = s_cmp[c]
        mx = jnp.maximum(jnp.max(s, axis=-1, keepdims=True), -1e20)
        e = jnp.exp2(s - mx)
        den = jnp.sum(e, axis=-1, keepdims=True)
        inv = jnp.where(den > 0.0, 1.0 / den, 0.0)
        both = jnp.dot(e.astype(BF16), vc_ov, preferred_element_type=F32) * inv
        imp = imp + both[:, LANES:]
        o_cmp.append(jnp.where(low, both[:QB, :LANES], both[QB:, :LANES]))
    yield

    imp_t = imp.T
    blk = lax.broadcasted_iota(jnp.int32, (LANES, sq), 0)
    cur = (t0 + lax.broadcasted_iota(jnp.int32, (LANES, sq), 1) % QB) // SEL_BLOCK
    forced = (blk == 0) | (blk == cur) | (blk == cur - 1)
    imp_t = jnp.where(forced, -jnp.inf, jnp.where(blk > cur, -BIG, imp_t))
    blk_f = blk.astype(F32)

    val = imp_t
    for _ in range(N_SEL - 3):
        top = jnp.max(val, axis=0, keepdims=True)
        idx = jnp.min(jnp.where(val == top, blk_f, 1e6), axis=0, keepdims=True)
        val = jnp.where(blk_f == idx, -jnp.inf, val)
    msel = jnp.where(val == -jnp.inf, 0.0, NEG).T.astype(BF16)
    yield

    off_prev = jnp.where(i >= 1, 0.0, -NEG)
    off_edge = jnp.where(i >= 2, 0.0, -NEG)
    ql_t = lax.broadcasted_iota(jnp.int32, (sq, TK), 0) % QB
    kl_t = lax.broadcasted_iota(jnp.int32, (sq, TK), 1)
    edge = jnp.where(kl_t > ql_t, 0.0, NEG)
    wv_all = jnp.concatenate([
        jnp.concatenate([wv_ref[pl.ds(st, TK), :], ones_col], axis=1) for st in (start_d, start_p, start_e)], axis=0)
    o_win = []
    for c in range(ncol):
        s0 = s_win[c][0] + stacked(d0_ref, c)
        s1 = s_win[c][1] + stacked(d1_ref, c)
        s2 = s_win[c][2] + edge
        mx = jnp.maximum(jnp.maximum(jnp.max(s0, axis=-1, keepdims=True),
                                     jnp.max(s1, axis=-1, keepdims=True) - off_prev),
                         jnp.max(s2, axis=-1, keepdims=True) - off_edge)
        p = jnp.exp2(jnp.concatenate([s0 - mx, s1 - (mx + off_prev), s2 - (mx + off_edge)], axis=1)).astype(BF16)
        ow = jnp.dot(p, wv_all, preferred_element_type=F32)
        o_win.append(jnp.where(low, ow[:QB, :LANES] / ow[:QB, LANES:], ow[QB:, :LANES] / ow[QB:, LANES:]))
    yield

    qaug = [jnp.concatenate([qs[c], msel], axis=1) for c in range(ncol)]

    def sel_scores(c, start, width):
        kaug = jnp.concatenate([sk_ref[pl.ds(start, width), :], oh_ref[pl.ds(start, width), :]], axis=1)
        return _dot_nt(qaug[c], kaug)

    def sel_vaug(start, width):
        return jnp.concatenate([sv_ref[pl.ds(start, width), :], jnp.ones((width, LANES), BF16)], axis=1)

    def lanes(x, width):
        return jnp.concatenate([x] * (width // LANES), axis=1)

    s_near = [(sel_scores(c, start_d, TK), sel_scores(c, start_p, TK)) for c in range(ncol)]
    for c in range(ncol):
        s0 = s_near[c][0] + stacked(d0_ref, c)
        s1 = s_near[c][1] + stacked(d1_ref, c)
        mx = jnp.maximum(jnp.max(s0, axis=-1, keepdims=True), jnp.max(s1, axis=-1, keepdims=True) - off_prev)
        p0 = jnp.exp2(s0 - mx).astype(BF16)
        p1 = jnp.exp2(s1 - (mx + off_prev)).astype(BF16)
        acc_ref[slot + c] = (jnp.dot(p0, sel_vaug(start_d, TK), preferred_element_type=F32)
                             + jnp.dot(p1, sel_vaug(start_p, TK), preferred_element_type=F32))
        m_ref[slot + c] = jnp.broadcast_to(mx, (sq, LANES))
    yield

    def far_tile(kt0, tiles):
        start = pl.multiple_of(kt0 * TK, tiles * TK)
        width = tiles * TK
        vaug = sel_vaug(start, width)
        scores = [sel_scores(c, start, width) for c in range(ncol)]
        for c in range(ncol):
            s = scores[c]
            m_old = m_ref[slot + c]
            m_new = jnp.maximum(m_old, jnp.max(s, axis=-1, keepdims=True))
            alpha = jnp.exp2(m_old - m_new)
            p = jnp.exp2(s - lanes(m_new, width)).astype(BF16)
            acc_ref[slot + c] = (lanes(alpha, 2 * LANES) * acc_ref[slot + c]
                                 + jnp.dot(p, vaug, preferred_element_type=F32))
            m_ref[slot + c] = m_new

    n_far = jnp.maximum(i - 1, 0)

    def far_loop(tiles):
        def body(kt, carry):
            far_tile(kt * tiles, tiles)
            return carry
        return body

    tiles, done = FAR_TILES, 0
    while tiles >= 1:
        count = n_far // tiles
        lax.fori_loop(done, count, far_loop(tiles), 0)
        tiles, done = tiles // 2, count * 2
    yield

    gate = jax.nn.sigmoid(gt_ref[rows, :].astype(F32)).astype(BF16)
    gexp = jnp.dot(gate, e_ref[...], preferred_element_type=F32)
    for c in range(ncol):
        acc = acc_ref[slot + c]
        o_sel = jnp.where(low, acc[:QB, :LANES] / acc[:QB, LANES:], acc[QB:, :LANES] / acc[QB:, LANES:])
        gc = lambda br: gexp[:, (br * ncol + c) * LANES:(br * ncol + c + 1) * LANES]
        o_ref[rows, c * LANES:(c + 1) * LANES] = (gc(0) * o_cmp[c] + gc(1) * o_sel + gc(2) * o_win[c]).astype(BF16)
    yield


def _nsa(proj3, kcmp, vcmp, onehot, overlap, band, d0, d1, tc, expand):
    b, seq, _ = proj3.shape
    ncmp = kcmp.shape[1]
    step = NQ_SUB * QB
    col = lambda width, cidx: pl.BlockSpec((None, step, width), lambda bi, qi: (bi, qi, cidx))
    res = lambda cidx: pl.BlockSpec((None, seq, LANES), lambda bi, qi: (bi, 0, cidx))
    full = lambda arr: pl.BlockSpec(arr.shape, lambda bi, qi: (0,) * arr.ndim)
    cmp_spec = pl.BlockSpec((None, ncmp, LANES), lambda bi, qi: (bi, 0, 0))
    return pl.pallas_call(
        functools.partial(_nsa_kernel, ncmp=ncmp),
        grid=(b, seq // step),
        in_specs=[col(NSA_W, COL_NQ * LANES // NSA_W), col(LANES, COL_GATE),
                  res(COL_SELK), res(COL_SELV), res(COL_WINK), res(COL_WINV),
                  cmp_spec, cmp_spec, full(onehot), full(overlap), full(band), full(d0), full(d1), full(tc),
                  full(expand)],
        out_specs=pl.BlockSpec((None, step, NSA_W), lambda bi, qi: (bi, qi, 0)),
        out_shape=jax.ShapeDtypeStruct((b, seq, NSA_W), BF16),
        scratch_shapes=[pltpu.VMEM((NQ_SUB * NSA_W // LANES, 2 * QB, 2 * LANES), F32),
                        pltpu.VMEM((NQ_SUB * NSA_W // LANES, 2 * QB, LANES), F32)],
        compiler_params=_cparams(("arbitrary", "arbitrary")),
        name="nsa_attention",
    )(proj3, proj3, proj3, proj3, proj3, proj3, kcmp, vcmp, onehot, overlap, band, d0, d1, tc, expand)


def _gmlp_kernel(u_ref, v_ref, ws_ref, b_ref, o_ref):
    lane = lax.broadcasted_iota(jnp.int32, (GM_CHUNK, LANES), 1)
    low = lane < GM_DIM
    row = lax.broadcasted_iota(jnp.int32, (GM_CHUNK, GM_CHUNK), 0)
    colm = lax.broadcasted_iota(jnp.int32, (GM_CHUNK, GM_CHUNK), 1)
    tril = colm <= row
    ones_bd = _head_ones()
    cells = [(cc, ch) for cc in range(GM_W // LANES) for ch in range(GM_STEP // GM_CHUNK)]
    sl = lambda cc: slice(cc * LANES, (cc + 1) * LANES)
    rows = lambda ch: slice(ch * GM_CHUNK, (ch + 1) * GM_CHUNK)
    w = [jnp.concatenate([jnp.where(tril, ws_ref[2 * cc], 0.0), jnp.where(tril, ws_ref[2 * cc + 1], 0.0)],
                         axis=0).astype(BF16) for cc in range(GM_W // LANES)]
    v = {(cc, ch): jax.nn.gelu(v_ref[rows(ch), sl(cc)].astype(F32), approximate=True) for cc, ch in cells}
    sums = {cell: _head_sum(v[cell] * v[cell], ones_bd) for cell in cells}
    vn = {cell: (v[cell] * lax.rsqrt(sums[cell] * (1.0 / GM_DIM) + EPS)).astype(BF16) for cell in cells}
    mixed = {(cc, ch): jnp.dot(w[cc], vn[cc, ch], preferred_element_type=F32) for cc, ch in cells}
    for cc, ch in cells:
        r = mixed[cc, ch]
        sv = jnp.where(low, r[:GM_CHUNK], r[GM_CHUNK:]) + b_ref[:, sl(cc)]
        u = jax.nn.gelu(u_ref[rows(ch), sl(cc)].astype(F32), approximate=True)
        o_ref[rows(ch), sl(cc)] = (u * sv).astype(BF16)


def _gmlp(proj3, ws, bexp):
    b, seq, _ = proj3.shape
    return pl.pallas_call(
        _gmlp_kernel,
        grid=(b, seq // GM_STEP),
        in_specs=[pl.BlockSpec((None, GM_STEP, GM_W), lambda bi, ti: (bi, ti, COL_GMU * LANES // GM_W)),
                  pl.BlockSpec((None, GM_STEP, GM_W), lambda bi, ti: (bi, ti, COL_GMV * LANES // GM_W)),
                  pl.BlockSpec((N_GM, GM_CHUNK, GM_CHUNK), lambda bi, ti: (0, 0, 0)),
                  pl.BlockSpec((GM_CHUNK, GM_W), lambda bi, ti: (0, 0))],
        out_specs=pl.BlockSpec((None, GM_STEP, GM_W), lambda bi, ti: (bi, ti, 0)),
        out_shape=jax.ShapeDtypeStruct((b, seq, GM_W), BF16),
        compiler_params=_cparams(("arbitrary", "arbitrary")),
        name="gmlp",
    )(proj3, proj3, ws, bexp)


FFN_CHUNK = 256
FFN_SPLIT = 1


def _ffn_kernel(x_ref, r_ref, n_ref, m_ref, wo_ref, g_ref, wgu_ref, wd_ref, o_ref, *, tm):
    sub_rows = tm // FFN_SPLIT
    for r0 in range(0, tm, sub_rows):
        rows = slice(r0, r0 + sub_rows)
        mixed = jnp.concatenate([r_ref[rows, :], n_ref[rows, :], m_ref[rows, :]], axis=1)
        x = x_ref[rows, :] + jnp.dot(mixed, wo_ref[...], preferred_element_type=F32)
        ms = jnp.mean(x * x, axis=-1, keepdims=True)
        h = (x * lax.rsqrt(ms + EPS) * g_ref[...]).astype(BF16)
        acc = x
        for f in range(0, D_FF, FFN_CHUNK):
            gate = jnp.dot(h, wgu_ref[:, f:f + FFN_CHUNK], preferred_element_type=F32)
            up = jnp.dot(h, wgu_ref[:, D_FF + f:D_FF + f + FFN_CHUNK], preferred_element_type=F32)
            act = (gate * jax.nn.sigmoid(gate) * up).astype(BF16)
            acc = acc + jnp.dot(act, wd_ref[f:f + FFN_CHUNK, :], preferred_element_type=F32)
        o_ref[rows, :] = acc


def _outproj_ffn(x2d, ret_o, nsa_o, gm_o, w_out, gain, wgu, wd, layer):
    m = x2d.shape[0]
    tm = 1024
    row = lambda width: pl.BlockSpec((tm, width), lambda i: (i, 0))
    const = lambda arr: pl.BlockSpec(arr.shape, lambda i: (0, 0), pipeline_mode=pl.Buffered(1))
    of_layer = lambda arr: pl.BlockSpec((None,) + arr.shape[1:], lambda i: (layer, 0, 0),
                                        pipeline_mode=pl.Buffered(1))
    return pl.pallas_call(
        functools.partial(_ffn_kernel, tm=tm),
        grid=(m // tm,),
        in_specs=[row(D_MODEL), row(RET_W), row(NSA_W), row(GM_W), const(w_out),
                  pl.BlockSpec((1, D_MODEL), lambda i: (0, 0)),
                  of_layer(wgu), of_layer(wd)],
        out_specs=row(D_MODEL),
        out_shape=jax.ShapeDtypeStruct((m, D_MODEL), F32),
        compiler_params=_cparams(("arbitrary",)),
        name="outproj_ffn",
    )(x2d, ret_o, nsa_o, gm_o, w_out, gain, wgu, wd)


def _nq_head_order():
    return [c + (N_NSA // N_KV) * half for c in range(NSA_W // LANES) for half in range(2)]


def _w_in_layout(w):
    wt = w.T
    off_nq = 4 * RET_W
    off_kv = off_nq + NSA_W
    off_gate = off_kv + 3 * 2 * N_KV * HEAD_DIM
    off_gm = off_gate + 3 * N_NSA
    heads = [wt[off_nq + h * HEAD_DIM:off_nq + (h + 1) * HEAD_DIM] for h in _nq_head_order()]
    main = jnp.concatenate([wt[:off_nq]] + heads + [wt[off_kv:off_gate + LANES]], axis=0).astype(BF16)
    return main, wt[off_gm:].astype(BF16)


def _w_out_layout(w):
    heads = [w[RET_W + h * HEAD_DIM:RET_W + (h + 1) * HEAD_DIM] for h in _nq_head_order()]
    return jnp.concatenate([w[:RET_W]] + heads + [w[RET_W + NSA_W:]], axis=0).astype(BF16)


def _gate_expand():
    e = np.zeros((LANES, 3 * NSA_W), np.float32)
    ncol = NSA_W // LANES
    for br in range(3):
        for c in range(ncol):
            for half in range(2):
                head = c + (N_NSA // N_KV) * half
                dst = (br * ncol + c) * LANES + half * HEAD_DIM
                e[head * 3 + br, dst:dst + HEAD_DIM] = 1.0
    return jnp.asarray(e, BF16)


def _rope_tables(seq):
    half = HEAD_DIM // 2
    inv = ROPE_THETA ** (-np.arange(half, dtype=np.float64) / half)
    ang = np.arange(seq, dtype=np.float64)[:, None] * inv[None, :]
    cos = np.tile(np.cos(ang), (1, LANES // half))
    sin = np.tile(np.concatenate([-np.sin(ang), np.sin(ang)], axis=1), (1, LANES // HEAD_DIM))
    return jnp.asarray(cos, F32), jnp.asarray(sin, F32)


def _retention_tables():
    c = RET_CHUNK
    log_gamma = np.log(1.0 - 2.0 ** (-5.0 - np.arange(N_RET, dtype=np.float64)))
    idx = np.arange(c, dtype=np.float64)
    diff = idx[:, None] - idx[None, :]
    decay = np.where(diff >= 0, np.exp(np.maximum(diff, 0.0)[None] * log_gamma[:, None, None]), 0.0)
    zeta = np.exp((c - 1 - idx)[None, :] * log_gamma[:, None])
    xi = np.exp((idx + 1)[None, :] * log_gamma[:, None])
    chunk_decay = np.exp(c * log_gamma)
    ncol = RET_W // LANES
    dec = decay.reshape(ncol, 2 * c, c)
    xi_st = np.broadcast_to(xi.reshape(ncol, 2 * c, 1), (ncol, 2 * c, LANES))
    zeta_l = np.repeat(zeta.reshape(ncol, 2, c).transpose(0, 2, 1), HEAD_DIM, axis=2)
    cd = np.broadcast_to(np.repeat(chunk_decay.reshape(ncol, 2), HEAD_DIM, axis=1)[:, :, None],
                         (ncol, LANES, LANES))
    return tuple(jnp.asarray(t, F32) for t in (dec, xi_st, zeta_l, cd))


def _overlap_table(seq):
    n_cmp = (seq - CMP_LEN) // CMP_STRIDE + 1
    n_slc = seq // SEL_BLOCK
    cs = np.arange(n_cmp)[:, None] * CMP_STRIDE
    ss = np.arange(n_slc)[None, :] * SEL_BLOCK
    ov = np.clip(np.minimum(cs + CMP_LEN, ss + SEL_BLOCK) - np.maximum(cs, ss), 0, None) // CMP_STRIDE
    full = np.zeros((seq // CMP_STRIDE, LANES), np.float32)
    full[:n_cmp, :n_slc] = ov
    return jnp.asarray(full, BF16)


def _cmp_band(seq):
    nq = seq // QB
    ncmp = seq // CMP_STRIDE
    x = np.arange(2 * ncmp)[:, None]
    m = np.arange(LANES)[None, :]
    base = (QB // CMP_STRIDE) * nq - CMP_LEAD
    band = np.where(m < LANES - 1, x == m + base, x >= m + base)
    return jnp.asarray(band.astype(np.float32), BF16)


def _block_onehot(seq):
    oh = (np.arange(seq)[:, None] // SEL_BLOCK) == np.arange(LANES)[None, :]
    return jnp.asarray(oh.astype(np.float32), BF16)


def _compress_weights(cmp_pe, cmp_w1, cmp_w2):
    def block_diag2(w):
        zero = jnp.zeros_like(w)
        return jnp.concatenate([jnp.concatenate([w, zero], axis=-1), jnp.concatenate([zero, w], axis=-1)],
                               axis=-2).astype(BF16)

    w1bd = block_diag2(cmp_w1.reshape(2, CMP_LEN, HEAD_DIM, HEAD_DIM))
    w2bd = block_diag2(cmp_w2)
    pe2 = jnp.tile(cmp_pe, (1, 1, N_KV))[:, :, None, :]
    return w1bd[:, :CMP_LEN // 2], w1bd[:, CMP_LEN // 2:], pe2, w2bd


def kernel(x, attn_norm, w_in, w_out, nsa_q_gain, nsa_k_gain, cmp_pe, cmp_w1, cmp_w2, gm_ws, gm_b, ffn_norm,
           w_gate_up, w_down, rel_bias):
    b, seq, _ = x.shape
    assert seq % (2 * TK) == 0 and N_SEL <= seq // SEL_BLOCK <= LANES, "selection blocks must fit one lane row"
    depth = w_in.shape[0]
    cos_tab, sin_tab = _rope_tables(seq)
    ret_tabs = _retention_tables()
    overlap = _overlap_table(seq)
    onehot = _block_onehot(seq)
    band = _cmp_band(seq)
    expand = _gate_expand()
    d0, d1, tc = _bias_tiles(rel_bias)
    ngrp = seq // CMP_STRIDE
    wgu_all = w_gate_up.astype(BF16)
    wd_all = w_down.astype(BF16)
    x2d = x.reshape(b * seq, D_MODEL)
    for l in range(depth):
        zero_row = jnp.zeros((5, LANES), F32)
        head_gains = jnp.concatenate([jnp.tile(nsa_q_gain[l] * (HEAD_DIM ** -0.5 * LOG2E), 2)[None],
                                      jnp.tile(nsa_k_gain[l, 1], 2)[None],
                                      jnp.tile(nsa_k_gain[l, 2], 2)[None], zero_row], axis=0)
        proj, cmp_grp = _inproj(x2d, attn_norm[l][None], *_w_in_layout(w_in[l]), cos_tab, sin_tab, head_gains, seq)
        proj3 = proj.reshape(b, seq, PROJ_W)
        ret_o = _retention(proj3, ret_tabs)
        cmp_in = cmp_grp.reshape(b, ngrp, CMP_STRIDE * 2 * LANES)
        wtop, wbot, pe2, w2bd = _compress_weights(cmp_pe[l], cmp_w1[l], cmp_w2[l])
        kgain = jnp.broadcast_to(jnp.tile(nsa_k_gain[l, 0], 2)[None], (8, LANES))
        kcmp, vcmp = _compress(cmp_in, wtop, wbot, pe2, w2bd, kgain)
        nsa_o = _nsa(proj3, kcmp, vcmp, onehot, overlap, band, d0, d1, tc, expand)
        gm_o = _gmlp(proj3, gm_ws[l], jnp.repeat(gm_b[l].T, GM_DIM, axis=1))
        x2d = _outproj_ffn(x2d, ret_o.reshape(b * seq, RET_W), nsa_o.reshape(b * seq, NSA_W),
                           gm_o.reshape(b * seq, GM_W), _w_out_layout(w_out[l]), ffn_norm[l][None],
                           wgu_all, wd_all, l)
    return x2d.reshape(b, seq, D_MODEL)
```

```python
import functools
import math

import numpy as np
import jax
import jax.numpy as jnp
from jax import lax
from jax.experimental import pallas as pl
from jax.experimental.pallas import tpu as pltpu

F32 = jnp.float32
BF16 = jnp.bfloat16

D_MODEL = 1024
HEAD_DIM = 64
N_RET = 6
N_NSA = 6
N_KV = 2
N_GM = 4
GM_DIM = 64
RET_W = N_RET * HEAD_DIM
NSA_W = N_NSA * HEAD_DIM
GM_W = N_GM * GM_DIM
RET_CHUNK = 128
CMP_LEN = 32
CMP_STRIDE = 16
SEL_BLOCK = 64
N_SEL = 16
WINDOW = 512
GM_CHUNK = 128
N_BUCKETS = 32
MAX_DISTANCE = 128
ROPE_THETA = 10000.0
D_FF = 2816
EPS = 1e-6
BIG = 1e9
NEG = -1e30
LOG2E = math.log2(math.e)

LANES = 128
PROJ_W = 3328
COL_RQ, COL_RK, COL_RV, COL_RG = 0, 3, 6, 9
COL_NQ = 12
COL_CMPK, COL_CMPV, COL_SELK, COL_SELV, COL_WINK, COL_WINV = 15, 16, 17, 18, 19, 20
COL_GATE = 21
COL_GMU, COL_GMV = 22, 24
QB = 256
TK = 256
CMP_LEAD = 16
NQ_SUB = 2
NSA_PHASES = 6
FAR_TILES = 8
INPROJ_SPLIT = 1
RET_STEP = 2048
GM_STEP = 2048
VMEM_LIMIT = 56 * 1024 * 1024


def _bucket_thresholds():
    n = np.arange(0, 4 * MAX_DISTANCE)
    max_exact = N_BUCKETS // 2
    nf = np.maximum(n, 1).astype(np.float64)
    large = max_exact + np.floor(np.log(nf / max_exact) / math.log(MAX_DISTANCE / max_exact)
                                 * (N_BUCKETS - max_exact)).astype(np.int64)
    bucket = np.where(n < max_exact, n, np.minimum(large, N_BUCKETS - 1))
    return [int(np.min(n[bucket >= b])) for b in range(N_BUCKETS)]


BUCKET_THR = _bucket_thresholds()


def _cparams(sem):
    return pltpu.CompilerParams(dimension_semantics=sem, vmem_limit_bytes=VMEM_LIMIT)


def _head_ones():
    r = lax.broadcasted_iota(jnp.int32, (LANES, LANES), 0) // HEAD_DIM
    c = lax.broadcasted_iota(jnp.int32, (LANES, LANES), 1) // HEAD_DIM
    return jnp.where(r == c, 1.0, 0.0).astype(BF16)


def _head_sum(x, ones_bd):
    return jnp.dot(x.astype(BF16), ones_bd, preferred_element_type=F32)


def _head_rmsnorm(x, ones_bd):
    return x * lax.rsqrt(_head_sum(x * x, ones_bd) * (1.0 / HEAD_DIM) + EPS)


def _dot_nt(a, b):
    return lax.dot_general(a, b, (((1,), (1,)), ((), ())), preferred_element_type=F32)


def _dot_tn(a, b):
    return lax.dot_general(a, b, (((0,), (0,)), ((), ())), preferred_element_type=F32)


def _stack_heads(q, low):
    zero = jnp.zeros_like(q)
    return jnp.concatenate([jnp.where(low, q, zero), jnp.where(low, zero, q)], axis=0)


def _inproj_kernel(x_ref, g_ref, wa_ref, wb_ref, cos_ref, sin_ref, hg_ref, o_ref, grp_ref, cmp_scr, *, tm):
    sub_rows = tm // INPROJ_SPLIT
    lane = lax.broadcasted_iota(jnp.int32, (sub_rows, LANES), 1)
    first_half = (lane % HEAD_DIM) < (HEAD_DIM // 2)
    ones_bd = _head_ones()
    norm_gain = {COL_NQ: 0, COL_NQ + 1: 0, COL_NQ + 2: 0, COL_SELK: 1, COL_WINK: 2}
    deferred = []
    for r0 in range(0, tm, sub_rows):
        rows = slice(r0, r0 + sub_rows)
        x = x_ref[rows, :]
        ms = jnp.mean(x * x, axis=-1, keepdims=True)
        h = (x * lax.rsqrt(ms + EPS) * g_ref[...]).astype(BF16)
        cos = cos_ref[rows, :]
        sin = sin_ref[rows, :]
        for s in range(0, PROJ_W // LANES, 2):
            if s < COL_GMU:
                acc = _dot_nt(h, wa_ref[s * LANES:(s + 2) * LANES, :])
            else:
                acc = _dot_nt(h, wb_ref[(s - COL_GMU) * LANES:(s - COL_GMU + 2) * LANES, :])
            for sub in range(2):
                j = s + sub
                a = acc[:, sub * LANES:(sub + 1) * LANES]
                if j < COL_RV:
                    swapped = jnp.where(first_half, pltpu.roll(a, LANES - HEAD_DIM // 2, axis=1),
                                        pltpu.roll(a, HEAD_DIM // 2, axis=1))
                    a = a * cos + swapped * sin
                    if j >= COL_RK:
                        a = a * (HEAD_DIM ** -0.5)
                elif j in norm_gain:
                    deferred.append((rows, j, a))
                    continue
                elif j in (COL_CMPK, COL_CMPV):
                    cmp_scr[j - COL_CMPK, rows, :] = a
                o_ref[rows, j * LANES:(j + 1) * LANES] = a.astype(BF16)
    for rows, j, a in deferred:
        a = _head_rmsnorm(a, ones_bd) * hg_ref[norm_gain[j]:norm_gain[j] + 1, :]
        o_ref[rows, j * LANES:(j + 1) * LANES] = a.astype(BF16)
    for l in range(CMP_STRIDE):
        for kv in range(2):
            grp_ref[:, (2 * l + kv) * LANES:(2 * l + kv + 1) * LANES] = (
                cmp_scr[kv, pl.ds(l, tm // CMP_STRIDE, stride=CMP_STRIDE), :].astype(BF16))


def _inproj(x2d, gain, w_main, w_gm, cos_tab, sin_tab, head_gains, seq):
    m = x2d.shape[0]
    tm = 1024
    nt = seq // tm
    const = lambda arr: pl.BlockSpec(arr.shape, lambda i: (0, 0), pipeline_mode=pl.Buffered(1))
    return pl.pallas_call(
        functools.partial(_inproj_kernel, tm=tm),
        grid=(m // tm,),
        in_specs=[
            pl.BlockSpec((tm, D_MODEL), lambda i: (i, 0)),
            pl.BlockSpec((1, D_MODEL), lambda i: (0, 0)),
            const(w_main), const(w_gm),
            pl.BlockSpec((tm, LANES), lambda i: (i % nt, 0)),
            pl.BlockSpec((tm, LANES), lambda i: (i % nt, 0)),
            pl.BlockSpec((8, LANES), lambda i: (0, 0)),
        ],
        out_specs=[pl.BlockSpec((tm, PROJ_W), lambda i: (i, 0)),
                   pl.BlockSpec((tm // CMP_STRIDE, CMP_STRIDE * 2 * LANES), lambda i: (i, 0))],
        out_shape=[jax.ShapeDtypeStruct((m, PROJ_W), BF16),
                   jax.ShapeDtypeStruct((m // CMP_STRIDE, CMP_STRIDE * 2 * LANES), BF16)],
        scratch_shapes=[pltpu.VMEM((2, tm, LANES), F32)],
        compiler_params=_cparams(("arbitrary",)),
        name="inproj",
    )(x2d, gain, w_main, w_gm, cos_tab, sin_tab, head_gains)


def _ret_kernel(q_ref, k_ref, v_ref, g_ref, dec_ref, xi_ref, zeta_ref, cd_ref, o_ref, r_ref):
    @pl.when(pl.program_id(1) == 0)
    def _():
        r_ref[...] = jnp.zeros_like(r_ref)

    lane = lax.broadcasted_iota(jnp.int32, (RET_CHUNK, LANES), 1)
    low = lane < HEAD_DIM
    ones_bd = _head_ones()
    cells = [(c, ch) for c in range(RET_W // LANES) for ch in range(RET_STEP // RET_CHUNK)]
    sl = lambda c: slice(c * LANES, (c + 1) * LANES)
    rows = lambda ch: slice(ch * RET_CHUNK, (ch + 1) * RET_CHUNK)
    qs = {(c, ch): _stack_heads(q_ref[rows(ch), sl(c)], low) for c, ch in cells}
    scores = {(c, ch): _dot_nt(qs[c, ch], k_ref[rows(ch), sl(c)]) for c, ch in cells}
    incr = {(c, ch): _dot_tn((k_ref[rows(ch), sl(c)].astype(F32) * zeta_ref[c]).astype(BF16), v_ref[rows(ch), sl(c)])
            for c, ch in cells}
    inner = {(c, ch): jnp.dot((scores[c, ch] * dec_ref[c]).astype(BF16), v_ref[rows(ch), sl(c)],
                              preferred_element_type=F32) for c, ch in cells}
    outs = {}
    for c in range(RET_W // LANES):
        state = r_ref[c]
        for ch in range(RET_STEP // RET_CHUNK):
            cross = jnp.dot(qs[c, ch], state.astype(BF16), preferred_element_type=F32) * xi_ref[c]
            tot = inner[c, ch] + cross
            outs[c, ch] = jnp.where(low, tot[:RET_CHUNK], tot[RET_CHUNK:])
            state = cd_ref[c] * state + incr[c, ch]
        r_ref[c] = state
    for c, ch in cells:
        g = g_ref[rows(ch), sl(c)].astype(F32)
        y = _head_rmsnorm(outs[c, ch], ones_bd) * (g * jax.nn.sigmoid(g))
        o_ref[rows(ch), sl(c)] = y.astype(BF16)


def _retention(proj3, tabs):
    b, seq, _ = proj3.shape
    dec, xi, zeta, cd = tabs
    ncol = RET_W // LANES
    qkvg = [pl.BlockSpec((None, RET_STEP, RET_W), functools.partial(lambda bi, ti, col: (bi, ti, col), col=col))
            for col in range(4)]
    const3 = lambda shape: pl.BlockSpec(shape, lambda bi, ti: (0, 0, 0))
    return pl.pallas_call(
        _ret_kernel,
        grid=(b, seq // RET_STEP),
        in_specs=qkvg + [const3((ncol, 2 * RET_CHUNK, LANES)), const3((ncol, 2 * RET_CHUNK, LANES)),
                         const3((ncol, RET_CHUNK, LANES)), const3((ncol, LANES, LANES))],
        out_specs=pl.BlockSpec((None, RET_STEP, RET_W), lambda bi, ti: (bi, ti, 0)),
        out_shape=jax.ShapeDtypeStruct((b, seq, RET_W), BF16),
        scratch_shapes=[pltpu.VMEM((ncol, LANES, LANES), F32)],
        compiler_params=_cparams(("arbitrary", "arbitrary")),
        name="retention",
    )(proj3, proj3, proj3, proj3, dec, xi, zeta, cd)


def _compress_kernel(a_ref, wtop_ref, wbot_ref, pe_ref, w2_ref, kg_ref, kc_ref, vc_ref, *, ngrp):
    half = CMP_LEN // 2
    ones_bd = _head_ones()
    for kv in range(2):
        top = jnp.zeros((ngrp, LANES), F32)
        bot = jnp.zeros((ngrp, LANES), F32)
        for l in range(half):
            x = a_ref[:, l * 2 * LANES + kv * LANES:l * 2 * LANES + (kv + 1) * LANES].astype(F32)
            top = top + jnp.dot((x + pe_ref[kv, l]).astype(BF16), wtop_ref[kv, l], preferred_element_type=F32)
            bot = bot + jnp.dot((x + pe_ref[kv, half + l]).astype(BF16), wbot_ref[kv, l],
                                preferred_element_type=F32)
        hid = jax.nn.gelu(top + pltpu.roll(bot, ngrp - 1, axis=0), approximate=True)
        out = jnp.dot(hid.astype(BF16), w2_ref[kv], preferred_element_type=F32)
        if kv == 0:
            kc_ref[...] = (_head_rmsnorm(out, ones_bd) * kg_ref[0:1, :]).astype(BF16)
        else:
            vc_ref[...] = out.astype(BF16)


def _compress(a, wtop, wbot, pe2, w2bd, kgain):
    b, ngrp, width = a.shape
    full = lambda arr: pl.BlockSpec(arr.shape, lambda bi: (0,) * arr.ndim)
    out_spec = pl.BlockSpec((None, ngrp, LANES), lambda bi: (bi, 0, 0))
    return pl.pallas_call(
        functools.partial(_compress_kernel, ngrp=ngrp),
        grid=(b,),
        in_specs=[pl.BlockSpec((None, ngrp, width), lambda bi: (bi, 0, 0)),
                  full(wtop), full(wbot), full(pe2), full(w2bd), full(kgain)],
        out_specs=[out_spec, out_spec],
        out_shape=[jax.ShapeDtypeStruct((b, ngrp, LANES), BF16)] * 2,
        compiler_params=_cparams(("arbitrary",)),
        name="nsa_compress",
    )(a, wtop, wbot, pe2, w2bd, kgain)


def _bias_kernel(rb_ref, d0_ref, d1_ref, tc_ref):
    h = pl.program_id(0)
    far = rb_ref[N_BUCKETS - 1, h]

    def rel(n):
        val = jnp.full(n.shape, (rb_ref[0, h] - far) * LOG2E, F32)
        for bkt in range(1, N_BUCKETS):
            val = jnp.where(n >= BUCKET_THR[bkt], (rb_ref[bkt, h] - far) * LOG2E, val)
        return val

    ql = lax.broadcasted_iota(jnp.int32, (QB, TK), 0)
    kl = lax.broadcasted_iota(jnp.int32, (QB, TK), 1)
    d = ql - kl
    d0_ref[...] = jnp.where(d >= 0, rel(d), NEG)
    d1_ref[...] = rel(d + TK)
    ql2 = lax.broadcasted_iota(jnp.int32, (QB, LANES), 0)
    m = lax.broadcasted_iota(jnp.int32, (QB, LANES), 1)
    dc = ql2 - CMP_STRIDE * (m - CMP_LEAD) - (CMP_LEN - 1)
    tc_ref[...] = jnp.where(dc >= 0, rel(dc), NEG).astype(BF16)


def _bias_tiles(rel_bias):
    return pl.pallas_call(
        _bias_kernel,
        grid=(N_NSA,),
        in_specs=[pl.BlockSpec(memory_space=pltpu.SMEM)],
        out_specs=[pl.BlockSpec((None, QB, TK), lambda h: (h, 0, 0)),
                   pl.BlockSpec((None, QB, TK), lambda h: (h, 0, 0)),
                   pl.BlockSpec((None, QB, LANES), lambda h: (h, 0, 0))],
        out_shape=[jax.ShapeDtypeStruct((N_NSA, QB, TK), F32),
                   jax.ShapeDtypeStruct((N_NSA, QB, TK), F32),
                   jax.ShapeDtypeStruct((N_NSA, QB, LANES), BF16)],
        compiler_params=_cparams(("arbitrary",)),
        name="t5_bias_tiles",
    )(rel_bias)


def _nsa_kernel(*refs, ncmp):
    programs = [_nsa_tile(sub, *refs, ncmp=ncmp) for sub in range(NQ_SUB)]
    for _ in range(NSA_PHASES):
        for prog in programs:
            next(prog)


def _nsa_tile(sub, q_ref, gt_ref, sk_ref, sv_ref, wk_ref, wv_ref, kc_ref, vc_ref, oh_ref, ov_ref, band_ref,
              d0_ref, d1_ref, tc_ref, e_ref, o_ref, acc_ref, m_ref, *, ncmp):
    i = NQ_SUB * pl.program_id(1) + sub
    t0 = i * QB
    ncol = NSA_W // LANES
    rows = slice(sub * QB, (sub + 1) * QB)
    slot = sub * ncol
    sq = 2 * QB
    low = lax.broadcasted_iota(jnp.int32, (QB, LANES), 1) < HEAD_DIM
    qs = [_stack_heads(q_ref[rows, c * LANES:(c + 1) * LANES], low) for c in range(ncol)]

    def stacked(ref, c):
        return jnp.concatenate([ref[c], ref[c + ncol]], axis=0)

    ones_col = jnp.ones((TK, LANES), BF16)

    n_tiles = NQ_SUB * pl.num_programs(1)
    band = band_ref[pl.ds(pl.multiple_of((QB // CMP_STRIDE) * (n_tiles - i), CMP_STRIDE), ncmp), :]
    kc_aug = jnp.concatenate([kc_ref[...], band], axis=1)
    vc_ov = jnp.concatenate([vc_ref[...], ov_ref[...]], axis=1)
    imp = jnp.zeros((sq, LANES), F32)
    o_cmp = []
    s_cmp = [_dot_nt(jnp.concatenate([qs[c], stacked(tc_ref, c)], axis=1), kc_aug) for c in range(ncol)]

    start_d = pl.multiple_of(t0, TK)
    start_p = pl.multiple_of(jnp.maximum(t0 - TK, 0), TK)
    start_e = pl.multiple_of(jnp.maximum(t0 - 2 * TK, 0), TK)
    s_win = [[_dot_nt(qs[c], wk_ref[pl.ds(st, TK), :]) for st in (start_d, start_p, start_e)] for c in range(ncol)]

    for c in range(ncol):
        s = s_cmp[c]
        mx = jnp.maximum(jnp.max(s, axis=-1, keepdims=True), -1e20)
        e = jnp.exp2(s - mx)
        den = jnp.sum(e, axis=-1, keepdims=True)
        inv = jnp.where(den > 0.0, 1.0 / den, 0.0)
        both = jnp.dot(e.astype(BF16), vc_ov, preferred_element_type=F32) * inv
        imp = imp + both[:, LANES:]
        o_cmp.append(jnp.where(low, both[:QB, :LANES], both[QB:, :LANES]))
    yield

    imp_t = imp.T
    blk = lax.broadcasted_iota(jnp.int32, (LANES, sq), 0)
    cur = (t0 + lax.broadcasted_iota(jnp.int32, (LANES, sq), 1) % QB) // SEL_BLOCK
    forced = (blk == 0) | (blk == cur) | (blk == cur - 1)
    imp_t = jnp.where(forced, -jnp.inf, jnp.where(blk > cur, -BIG, imp_t))
    blk_f = blk.astype(F32)

    val = imp_t
    for _ in range(N_SEL - 3):
        top = jnp.max(val, axis=0, keepdims=True)
        idx = jnp.min(jnp.where(val == top, blk_f, 1e6), axis=0, keepdims=True)
        val = jnp.where(blk_f == idx, -jnp.inf, val)
    msel = jnp.where(val == -jnp.inf, 0.0, NEG).T.astype(BF16)
    yield

    off_prev = jnp.where(i >= 1, 0.0, -NEG)
    off_edge = jnp.where(i >= 2, 0.0, -NEG)
    ql_t = lax.broadcasted_iota(jnp.int32, (sq, TK), 0) % QB
    kl_t = lax.broadcasted_iota(jnp.int32, (sq, TK), 1)
    edge = jnp.where(kl_t > ql_t, 0.0, NEG)
    wv_all = jnp.concatenate([
        jnp.concatenate([wv_ref[pl.ds(st, TK), :], ones_col], axis=1) for st in (start_d, start_p, start_e)], axis=0)
    o_win = []
    for c in range(ncol):
        s0 = s_win[c][0] + stacked(d0_ref, c)
        s1 = s_win[c][1] + stacked(d1_ref, c)
        s2 = s_win[c][2] + edge
        mx = jnp.maximum(jnp.maximum(jnp.max(s0, axis=-1, keepdims=True),
                                     jnp.max(s1, axis=-1, keepdims=True) - off_prev),
                         jnp.max(s2, axis=-1, keepdims=True) - off_edge)
        p = jnp.exp2(jnp.concatenate([s0 - mx, s1 - (mx + off_prev), s2 - (mx + off_edge)], axis=1)).astype(BF16)
        ow = jnp.dot(p, wv_all, preferred_element_type=F32)
        o_win.append(jnp.where(low, ow[:QB, :LANES] / ow[:QB, LANES:], ow[QB:, :LANES] / ow[QB:, LANES:]))
    yield

    qaug = [jnp.concatenate([qs[c], msel], axis=1) for c in range(ncol)]

    def sel_scores(c, start, width):
        kaug = jnp.concatenate([sk_ref[pl.ds(start, width), :], oh_ref[pl.ds(start, width), :]], axis=1)
        return _dot_nt(qaug[c], kaug)

    def sel_vaug(start, width):
        return jnp.concatenate([sv_ref[pl.ds(start, width), :], jnp.ones((width, LANES), BF16)], axis=1)

    def lanes(x, width):
        return jnp.concatenate([x] * (width // LANES), axis=1)

    s_near = [(sel_scores(c, start_d, TK), sel_scores(c, start_p, TK)) for c in range(ncol)]
    for c in range(ncol):
        s0 = s_near[c][0] + stacked(d0_ref, c)
        s1 = s_near[c][1] + stacked(d1_ref, c)
        mx = jnp.maximum(jnp.max(s0, axis=-1, keepdims=True), jnp.max(s1, axis=-1, keepdims=True) - off_prev)
        p0 = jnp.exp2(s0 - mx).astype(BF16)
        p1 = jnp.exp2(s1 - (mx + off_prev)).astype(BF16)
        acc_ref[slot + c] = (jnp.dot(p0, sel_vaug(start_d, TK), preferred_element_type=F32)
                             + jnp.dot(p1, sel_vaug(start_p, TK), preferred_element_type=F32))
        m_ref[slot + c] = jnp.broadcast_to(mx, (sq, LANES))
    yield

    def far_tile(kt0, tiles):
        start = pl.multiple_of(kt0 * TK, tiles * TK)
        width = tiles * TK
        vaug = sel_vaug(start, width)
        scores = [sel_scores(c, start, width) for c in range(ncol)]
        for c in range(ncol):
            s = scores[c]
            m_old = m_ref[slot + c]
            m_new = jnp.maximum(m_old, jnp.max(s, axis=-1, keepdims=True))
            alpha = jnp.exp2(m_old - m_new)
            p = jnp.exp2(s - lanes(m_new, width)).astype(BF16)
            acc_ref[slot + c] = (lanes(alpha, 2 * LANES) * acc_ref[slot + c]
                                 + jnp.dot(p, vaug, preferred_element_type=F32))
            m_ref[slot + c] = m_new

    n_far = jnp.maximum(i - 1, 0)

    def far_loop(tiles):
        def body(kt, carry):
            far_tile(kt * tiles, tiles)
            return carry
        return body

    tiles, done = FAR_TILES, 0
    while tiles >= 1:
        count = n_far // tiles
        lax.fori_loop(done, count, far_loop(tiles), 0)
        tiles, done = tiles // 2, count * 2
    yield

    gate = jax.nn.sigmoid(gt_ref[rows, :].astype(F32)).astype(BF16)
    gexp = jnp.dot(gate, e_ref[...], preferred_element_type=F32)
    for c in range(ncol):
        acc = acc_ref[slot + c]
        o_sel = jnp.where(low, acc[:QB, :LANES] / acc[:QB, LANES:], acc[QB:, :LANES] / acc[QB:, LANES:])
        gc = lambda br: gexp[:, (br * ncol + c) * LANES:(br * ncol + c + 1) * LANES]
        o_ref[rows, c * LANES:(c + 1) * LANES] = (gc(0) * o_cmp[c] + gc(1) * o_sel + gc(2) * o_win[c]).astype(BF16)
    yield


def _nsa(proj3, kcmp, vcmp, onehot, overlap, band, d0, d1, tc, expand):
    b, seq, _ = proj3.shape
    ncmp = kcmp.shape[1]
    step = NQ_SUB * QB
    col = lambda width, cidx: pl.BlockSpec((None, step, width), lambda bi, qi: (bi, qi, cidx))
    res = lambda cidx: pl.BlockSpec((None, seq, LANES), lambda bi, qi: (bi, 0, cidx))
    full = lambda arr: pl.BlockSpec(arr.shape, lambda bi, qi: (0,) * arr.ndim)
    cmp_spec = pl.BlockSpec((None, ncmp, LANES), lambda bi, qi: (bi, 0, 0))
    return pl.pallas_call(
        functools.partial(_nsa_kernel, ncmp=ncmp),
        grid=(b, seq // step),
        in_specs=[col(NSA_W, COL_NQ * LANES // NSA_W), col(LANES, COL_GATE),
                  res(COL_SELK), res(COL_SELV), res(COL_WINK), res(COL_WINV),
                  cmp_spec, cmp_spec, full(onehot), full(overlap), full(band), full(d0), full(d1), full(tc),
                  full(expand)],
        out_specs=pl.BlockSpec((None, step, NSA_W), lambda bi, qi: (bi, qi, 0)),
        out_shape=jax.ShapeDtypeStruct((b, seq, NSA_W), BF16),
        scratch_shapes=[pltpu.VMEM((NQ_SUB * NSA_W // LANES, 2 * QB, 2 * LANES), F32),
                        pltpu.VMEM((NQ_SUB * NSA_W // LANES, 2 * QB, LANES), F32)],
        compiler_params=_cparams(("arbitrary", "arbitrary")),
        name="nsa_attention",
    )(proj3, proj3, proj3, proj3, proj3, proj3, kcmp, vcmp, onehot, overlap, band, d0, d1, tc, expand)


def _gmlp_kernel(u_ref, v_ref, ws_ref, b_ref, o_ref):
    lane = lax.broadcasted_iota(jnp.int32, (GM_CHUNK, LANES), 1)
    low = lane < GM_DIM
    row = lax.broadcasted_iota(jnp.int32, (GM_CHUNK, GM_CHUNK), 0)
    colm = lax.broadcasted_iota(jnp.int32, (GM_CHUNK, GM_CHUNK), 1)
    tril = colm <= row
    ones_bd = _head_ones()
    cells = [(cc, ch) for cc in range(GM_W // LANES) for ch in range(GM_STEP // GM_CHUNK)]
    sl = lambda cc: slice(cc * LANES, (cc + 1) * LANES)
    rows = lambda ch: slice(ch * GM_CHUNK, (ch + 1) * GM_CHUNK)
    w = [jnp.concatenate([jnp.where(tril, ws_ref[2 * cc], 0.0), jnp.where(tril, ws_ref[2 * cc + 1], 0.0)],
                         axis=0).astype(BF16) for cc in range(GM_W // LANES)]
    v = {(cc, ch): jax.nn.gelu(v_ref[rows(ch), sl(cc)].astype(F32), approximate=True) for cc, ch in cells}
    sums = {cell: _head_sum(v[cell] * v[cell], ones_bd) for cell in cells}
    vn = {cell: (v[cell] * lax.rsqrt(sums[cell] * (1.0 / GM_DIM) + EPS)).astype(BF16) for cell in cells}
    mixed = {(cc, ch): jnp.dot(w[cc], vn[cc, ch], preferred_element_type=F32) for cc, ch in cells}
    for cc, ch in cells:
        r = mixed[cc, ch]
        sv = jnp.where(low, r[:GM_CHUNK], r[GM_CHUNK:]) + b_ref[:, sl(cc)]
        u = jax.nn.gelu(u_ref[rows(ch), sl(cc)].astype(F32), approximate=True)
        o_ref[rows(ch), sl(cc)] = (u * sv).astype(BF16)


def _gmlp(proj3, ws, bexp):
    b, seq, _ = proj3.shape
    return pl.pallas_call(
        _gmlp_kernel,
        grid=(b, seq // GM_STEP),
        in_specs=[pl.BlockSpec((None, GM_STEP, GM_W), lambda bi, ti: (bi, ti, COL_GMU * LANES // GM_W)),
                  pl.BlockSpec((None, GM_STEP, GM_W), lambda bi, ti: (bi, ti, COL_GMV * LANES // GM_W)),
                  pl.BlockSpec((N_GM, GM_CHUNK, GM_CHUNK), lambda bi, ti: (0, 0, 0)),
                  pl.BlockSpec((GM_CHUNK, GM_W), lambda bi, ti: (0, 0))],
        out_specs=pl.BlockSpec((None, GM_STEP, GM_W), lambda bi, ti: (bi, ti, 0)),
        out_shape=jax.ShapeDtypeStruct((b, seq, GM_W), BF16),
        compiler_params=_cparams(("arbitrary", "arbitrary")),
        name="gmlp",
    )(proj3, proj3, ws, bexp)


FFN_CHUNK = 256
FFN_SPLIT = 1


def _ffn_kernel(x_ref, r_ref, n_ref, m_ref, wo_ref, g_ref, wgu_ref, wd_ref, o_ref, *, tm):
    sub_rows = tm // FFN_SPLIT
    for r0 in range(0, tm, sub_rows):
        rows = slice(r0, r0 + sub_rows)
        mixed = jnp.concatenate([r_ref[rows, :], n_ref[rows, :], m_ref[rows, :]], axis=1)
        x = x_ref[rows, :] + jnp.dot(mixed, wo_ref[...], preferred_element_type=F32)
        ms = jnp.mean(x * x, axis=-1, keepdims=True)
        h = (x * lax.rsqrt(ms + EPS) * g_ref[...]).astype(BF16)
        acc = x
        for f in range(0, D_FF, FFN_CHUNK):
            gate = jnp.dot(h, wgu_ref[:, f:f + FFN_CHUNK], preferred_element_type=F32)
            up = jnp.dot(h, wgu_ref[:, D_FF + f:D_FF + f + FFN_CHUNK], preferred_element_type=F32)
            act = (gate * jax.nn.sigmoid(gate) * up).astype(BF16)
            acc = acc + jnp.dot(act, wd_ref[f:f + FFN_CHUNK, :], preferred_element_type=F32)
        o_ref[rows, :] = acc


def _outproj_ffn(x2d, ret_o, nsa_o, gm_o, w_out, gain, wgu, wd, layer):
    m = x2d.shape[0]
    tm = 1024
    row = lambda width: pl.BlockSpec((tm, width), lambda i: (i, 0))
    const = lambda arr: pl.BlockSpec(arr.shape, lambda i: (0, 0), pipeline_mode=pl.Buffered(1))
    of_layer = lambda arr: pl.BlockSpec((None,) + arr.shape[1:], lambda i: (layer, 0, 0),
                                        pipeline_mode=pl.Buffered(1))
    return pl.pallas_call(
        functools.partial(_ffn_kernel, tm=tm),
        grid=(m // tm,),
        in_specs=[row(D_MODEL), row(RET_W), row(NSA_W), row(GM_W), const(w_out),
                  pl.BlockSpec((1, D_MODEL), lambda i: (0, 0)),
                  of_layer(wgu), of_layer(wd)],
        out_specs=row(D_MODEL),
        out_shape=jax.ShapeDtypeStruct((m, D_MODEL), F32),
        compiler_params=_cparams(("arbitrary",)),
        name="outproj_ffn",
    )(x2d, ret_o, nsa_o, gm_o, w_out, gain, wgu, wd)


def _nq_head_order():
    return [c + (N_NSA // N_KV) * half for c in range(NSA_W // LANES) for half in range(2)]


def _w_in_layout(w):
    wt = w.T
    off_nq = 4 * RET_W
    off_kv = off_nq + NSA_W
    off_gate = off_kv + 3 * 2 * N_KV * HEAD_DIM
    off_gm = off_gate + 3 * N_NSA
    heads = [wt[off_nq + h * HEAD_DIM:off_nq + (h + 1) * HEAD_DIM] for h in _nq_head_order()]
    main = jnp.concatenate([wt[:off_nq]] + heads + [wt[off_kv:off_gate + LANES]], axis=0).astype(BF16)
    return main, wt[off_gm:].astype(BF16)


def _w_out_layout(w):
    heads = [w[RET_W + h * HEAD_DIM:RET_W + (h + 1) * HEAD_DIM] for h in _nq_head_order()]
    return jnp.concatenate([w[:RET_W]] + heads + [w[RET_W + NSA_W:]], axis=0).astype(BF16)


def _gate_expand():
    e = np.zeros((LANES, 3 * NSA_W), np.float32)
    ncol = NSA_W // LANES
    for br in range(3):
        for c in range(ncol):
            for half in range(2):
                head = c + (N_NSA // N_KV) * half
                dst = (br * ncol + c) * LANES + half * HEAD_DIM
                e[head * 3 + br, dst:dst + HEAD_DIM] = 1.0
    return jnp.asarray(e, BF16)


def _rope_tables(seq):
    half = HEAD_DIM // 2
    inv = ROPE_THETA ** (-np.arange(half, dtype=np.float64) / half)
    ang = np.arange(seq, dtype=np.float64)[:, None] * inv[None, :]
    cos = np.tile(np.cos(ang), (1, LANES // half))
    sin = np.tile(np.concatenate([-np.sin(ang), np.sin(ang)], axis=1), (1, LANES // HEAD_DIM))
    return jnp.asarray(cos, F32), jnp.asarray(sin, F32)


def _retention_tables():
    c = RET_CHUNK
    log_gamma = np.log(1.0 - 2.0 ** (-5.0 - np.arange(N_RET, dtype=np.float64)))
    idx = np.arange(c, dtype=np.float64)
    diff = idx[:, None] - idx[None, :]
    decay = np.where(diff >= 0, np.exp(np.maximum(diff, 0.0)[None] * log_gamma[:, None, None]), 0.0)
    zeta = np.exp((c - 1 - idx)[None, :] * log_gamma[:, None])
    xi = np.exp((idx + 1)[None, :] * log_gamma[:, None])
    chunk_decay = np.exp(c * log_gamma)
    ncol = RET_W // LANES
    dec = decay.reshape(ncol, 2 * c, c)
    xi_st = np.broadcast_to(xi.reshape(ncol, 2 * c, 1), (ncol, 2 * c, LANES))
    zeta_l = np.repeat(zeta.reshape(ncol, 2, c).transpose(0, 2, 1), HEAD_DIM, axis=2)
    cd = np.broadcast_to(np.repeat(chunk_decay.reshape(ncol, 2), HEAD_DIM, axis=1)[:, :, None],
                         (ncol, LANES, LANES))
    return tuple(jnp.asarray(t, F32) for t in (dec, xi_st, zeta_l, cd))


def _overlap_table(seq):
    n_cmp = (seq - CMP_LEN) // CMP_STRIDE + 1
    n_slc = seq // SEL_BLOCK
    cs = np.arange(n_cmp)[:, None] * CMP_STRIDE
    ss = np.arange(n_slc)[None, :] * SEL_BLOCK
    ov = np.clip(np.minimum(cs + CMP_LEN, ss + SEL_BLOCK) - np.maximum(cs, ss), 0, None) // CMP_STRIDE
    full = np.zeros((seq // CMP_STRIDE, LANES), np.float32)
    full[:n_cmp, :n_slc] = ov
    return jnp.asarray(full, BF16)


def _cmp_band(seq):
    nq = seq // QB
    ncmp = seq // CMP_STRIDE
    x = np.arange(2 * ncmp)[:, None]
    m = np.arange(LANES)[None, :]
    base = (QB // CMP_STRIDE) * nq - CMP_LEAD
    band = np.where(m < LANES - 1, x == m + base, x >= m + base)
    return jnp.asarray(band.astype(np.float32), BF16)


def _block_onehot(seq):
    oh = (np.arange(seq)[:, None] // SEL_BLOCK) == np.arange(LANES)[None, :]
    return jnp.asarray(oh.astype(np.float32), BF16)


def _compress_weights(cmp_pe, cmp_w1, cmp_w2):
    def block_diag2(w):
        zero = jnp.zeros_like(w)
        return jnp.concatenate([jnp.concatenate([w, zero], axis=-1), jnp.concatenate([zero, w], axis=-1)],
                               axis=-2).astype(BF16)

    w1bd = block_diag2(cmp_w1.reshape(2, CMP_LEN, HEAD_DIM, HEAD_DIM))
    w2bd = block_diag2(cmp_w2)
    pe2 = jnp.tile(cmp_pe, (1, 1, N_KV))[:, :, None, :]
    return w1bd[:, :CMP_LEN // 2], w1bd[:, CMP_LEN // 2:], pe2, w2bd


def kernel(x, attn_norm, w_in, w_out, nsa_q_gain, nsa_k_gain, cmp_pe, cmp_w1, cmp_w2, gm_ws, gm_b, ffn_norm,
           w_gate_up, w_down, rel_bias):
    b, seq, _ = x.shape
    assert seq % (2 * TK) == 0 and N_SEL <= seq // SEL_BLOCK <= LANES, "selection blocks must fit one lane row"
    depth = w_in.shape[0]
    cos_tab, sin_tab = _rope_tables(seq)
    ret_tabs = _retention_tables()
    overlap = _overlap_table(seq)
    onehot = _block_onehot(seq)
    band = _cmp_band(seq)
    expand = _gate_expand()
    d0, d1, tc = _bias_tiles(rel_bias)
    ngrp = seq // CMP_STRIDE
    wgu_all = w_gate_up.astype(BF16)
    wd_all = w_down.astype(BF16)
    x2d = x.reshape(b * seq, D_MODEL)
    for l in range(depth):
        zero_row = jnp.zeros((5, LANES), F32)
        head_gains = jnp.concatenate([jnp.tile(nsa_q_gain[l] * (HEAD_DIM ** -0.5 * LOG2E), 2)[None],
                                      jnp.tile(nsa_k_gain[l, 1], 2)[None],
                                      jnp.tile(nsa_k_gain[l, 2], 2)[None], zero_row], axis=0)
        proj, cmp_grp = _inproj(x2d, attn_norm[l][None], *_w_in_layout(w_in[l]), cos_tab, sin_tab, head_gains, seq)
        proj3 = proj.reshape(b, seq, PROJ_W)
        ret_o = _retention(proj3, ret_tabs)
        cmp_in = cmp_grp.reshape(b, ngrp, CMP_STRIDE * 2 * LANES)
        wtop, wbot, pe2, w2bd = _compress_weights(cmp_pe[l], cmp_w1[l], cmp_w2[l])
        kgain = jnp.broadcast_to(jnp.tile(nsa_k_gain[l, 0], 2)[None], (8, LANES))
        kcmp, vcmp = _compress(cmp_in, wtop, wbot, pe2, w2bd, kgain)
        nsa_o = _nsa(proj3, kcmp, vcmp, onehot, overlap, band, d0, d1, tc, expand)
        gm_o = _gmlp(proj3, gm_ws[l], jnp.repeat(gm_b[l].T, GM_DIM, axis=1))
        x2d = _outproj_ffn(x2d, ret_o.reshape(b * seq, RET_W), nsa_o.reshape(b * seq, NSA_W),
                           gm_o.reshape(b * seq, GM_W), _w_out_layout(w_out[l]), ffn_norm[l][None],
                           wgu_all, wd_all, l)
    return x2d.reshape(b, seq, D_MODEL)
```

```python
import functools
import math

import numpy as np
import jax
import jax.numpy as jnp
from jax import lax
from jax.experimental import pallas as pl
from jax.experimental.pallas import tpu as pltpu

F32 = jnp.float32
BF16 = jnp.bfloat16

D_MODEL = 1024
HEAD_DIM = 64
N_RET = 6
N_NSA = 6
N_KV = 2
N_GM = 4
GM_DIM = 64
RET_W = N_RET * HEAD_DIM
NSA_W = N_NSA * HEAD_DIM
GM_W = N_GM * GM_DIM
RET_CHUNK = 128
CMP_LEN = 32
CMP_STRIDE = 16
SEL_BLOCK = 64
N_SEL = 16
WINDOW = 512
GM_CHUNK = 128
N_BUCKETS = 32
MAX_DISTANCE = 128
ROPE_THETA = 10000.0
D_FF = 2816
EPS = 1e-6
BIG = 1e9
NEG = -1e30
LOG2E = math.log2(math.e)

LANES = 128
PROJ_W = 3328
COL_RQ, COL_RK, COL_RV, COL_RG = 0, 3, 6, 9
COL_NQ = 12
COL_CMPK, COL_CMPV, COL_SELK, COL_SELV, COL_WINK, COL_WINV = 15, 16, 17, 18, 19, 20
COL_GATE = 21
COL_GMU, COL_GMV = 22, 24
QB = 256
TK = 256
CMP_LEAD = 16
NQ_SUB = 2
NSA_PHASES = 6
FAR_TILES = 8
INPROJ_SPLIT = 1
RET_STEP = 1024
GM_STEP = 1024
VMEM_LIMIT = 56 * 1024 * 1024


def _bucket_thresholds():
    n = np.arange(0, 4 * MAX_DISTANCE)
    max_exact = N_BUCKETS // 2
    nf = np.maximum(n, 1).astype(np.float64)
    large = max_exact + np.floor(np.log(nf / max_exact) / math.log(MAX_DISTANCE / max_exact)
                                 * (N_BUCKETS - max_exact)).astype(np.int64)
    bucket = np.where(n < max_exact, n, np.minimum(large, N_BUCKETS - 1))
    return [int(np.min(n[bucket >= b])) for b in range(N_BUCKETS)]


BUCKET_THR = _bucket_thresholds()


def _cparams(sem):
    return pltpu.CompilerParams(dimension_semantics=sem, vmem_limit_bytes=VMEM_LIMIT)


def _head_ones():
    r = lax.broadcasted_iota(jnp.int32, (LANES, LANES), 0) // HEAD_DIM
    c = lax.broadcasted_iota(jnp.int32, (LANES, LANES), 1) // HEAD_DIM
    return jnp.where(r == c, 1.0, 0.0).astype(BF16)


def _head_sum(x, ones_bd):
    return jnp.dot(x.astype(BF16), ones_bd, preferred_element_type=F32)


def _head_rmsnorm(x, ones_bd):
    return x * lax.rsqrt(_head_sum(x * x, ones_bd) * (1.0 / HEAD_DIM) + EPS)


def _dot_nt(a, b):
    return lax.dot_general(a, b, (((1,), (1,)), ((), ())), preferred_element_type=F32)


def _dot_tn(a, b):
    return lax.dot_general(a, b, (((0,), (0,)), ((), ())), preferred_element_type=F32)


def _stack_heads(q, low):
    zero = jnp.zeros_like(q)
    return jnp.concatenate([jnp.where(low, q, zero), jnp.where(low, zero, q)], axis=0)


def _inproj_kernel(x_ref, g_ref, wa_ref, wb_ref, cos_ref, sin_ref, hg_ref, o_ref, grp_ref, cmp_scr, *, tm):
    sub_rows = tm // INPROJ_SPLIT
    lane = lax.broadcasted_iota(jnp.int32, (sub_rows, LANES), 1)
    first_half = (lane % HEAD_DIM) < (HEAD_DIM // 2)
    ones_bd = _head_ones()
    norm_gain = {COL_NQ: 0, COL_NQ + 1: 0, COL_NQ + 2: 0, COL_SELK: 1, COL_WINK: 2}
    deferred = []
    for r0 in range(0, tm, sub_rows):
        rows = slice(r0, r0 + sub_rows)
        x = x_ref[rows, :]
        ms = jnp.mean(x * x, axis=-1, keepdims=True)
        h = (x * lax.rsqrt(ms + EPS) * g_ref[...]).astype(BF16)
        cos = cos_ref[rows, :]
        sin = sin_ref[rows, :]
        for s in range(0, PROJ_W // LANES, 2):
            if s < COL_GMU:
                acc = _dot_nt(h, wa_ref[s * LANES:(s + 2) * LANES, :])
            else:
                acc = _dot_nt(h, wb_ref[(s - COL_GMU) * LANES:(s - COL_GMU + 2) * LANES, :])
            for sub in range(2):
                j = s + sub
                a = acc[:, sub * LANES:(sub + 1) * LANES]
                if j < COL_RV:
                    swapped = jnp.where(first_half, pltpu.roll(a, LANES - HEAD_DIM // 2, axis=1),
                                        pltpu.roll(a, HEAD_DIM // 2, axis=1))
                    a = a * cos + swapped * sin
                    if j >= COL_RK:
                        a = a * (HEAD_DIM ** -0.5)
                elif j in norm_gain:
                    deferred.append((rows, j, a))
                    continue
                elif j in (COL_CMPK, COL_CMPV):
                    cmp_scr[j - COL_CMPK, rows, :] = a
                o_ref[rows, j * LANES:(j + 1) * LANES] = a.astype(BF16)
    for rows, j, a in deferred:
        a = _head_rmsnorm(a, ones_bd) * hg_ref[norm_gain[j]:norm_gain[j] + 1, :]
        o_ref[rows, j * LANES:(j + 1) * LANES] = a.astype(BF16)
    for l in range(CMP_STRIDE):
        for kv in range(2):
            grp_ref[:, (2 * l + kv) * LANES:(2 * l + kv + 1) * LANES] = (
                cmp_scr[kv, pl.ds(l, tm // CMP_STRIDE, stride=CMP_STRIDE), :].astype(BF16))


def _inproj(x2d, gain, w_main, w_gm, cos_tab, sin_tab, head_gains, seq):
    m = x2d.shape[0]
    tm = 1024
    nt = seq // tm
    const = lambda arr: pl.BlockSpec(arr.shape, lambda i: (0, 0), pipeline_mode=pl.Buffered(1))
    return pl.pallas_call(
        functools.partial(_inproj_kernel, tm=tm),
        grid=(m // tm,),
        in_specs=[
            pl.BlockSpec((tm, D_MODEL), lambda i: (i, 0)),
            pl.BlockSpec((1, D_MODEL), lambda i: (0, 0)),
            const(w_main), const(w_gm),
            pl.BlockSpec((tm, LANES), lambda i: (i % nt, 0)),
            pl.BlockSpec((tm, LANES), lambda i: (i % nt, 0)),
            pl.BlockSpec((8, LANES), lambda i: (0, 0)),
        ],
        out_specs=[pl.BlockSpec((tm, PROJ_W), lambda i: (i, 0)),
                   pl.BlockSpec((tm // CMP_STRIDE, CMP_STRIDE * 2 * LANES), lambda i: (i, 0))],
        out_shape=[jax.ShapeDtypeStruct((m, PROJ_W), BF16),
                   jax.ShapeDtypeStruct((m // CMP_STRIDE, CMP_STRIDE * 2 * LANES), BF16)],
        scratch_shapes=[pltpu.VMEM((2, tm, LANES), F32)],
        compiler_params=_cparams(("arbitrary",)),
        name="inproj",
    )(x2d, gain, w_main, w_gm, cos_tab, sin_tab, head_gains)


def _ret_kernel(q_ref, k_ref, v_ref, g_ref, dec_ref, xi_ref, zeta_ref, cd_ref, o_ref, r_ref):
    @pl.when(pl.program_id(1) == 0)
    def _():
        r_ref[...] = jnp.zeros_like(r_ref)

    lane = lax.broadcasted_iota(jnp.int32, (RET_CHUNK, LANES), 1)
    low = lane < HEAD_DIM
    ones_bd = _head_ones()
    cells = [(c, ch) for c in range(RET_W // LANES) for ch in range(RET_STEP // RET_CHUNK)]
    sl = lambda c: slice(c * LANES, (c + 1) * LANES)
    rows = lambda ch: slice(ch * RET_CHUNK, (ch + 1) * RET_CHUNK)
    qs = {(c, ch): _stack_heads(q_ref[rows(ch), sl(c)], low) for c, ch in cells}
    scores = {(c, ch): _dot_nt(qs[c, ch], k_ref[rows(ch), sl(c)]) for c, ch in cells}
    incr = {(c, ch): _dot_tn((k_ref[rows(ch), sl(c)].astype(F32) * zeta_ref[c]).astype(BF16), v_ref[rows(ch), sl(c)])
            for c, ch in cells}
    inner = {(c, ch): jnp.dot((scores[c, ch] * dec_ref[c]).astype(BF16), v_ref[rows(ch), sl(c)],
                              preferred_element_type=F32) for c, ch in cells}
    outs = {}
    for c in range(RET_W // LANES):
        state = r_ref[c]
        for ch in range(RET_STEP // RET_CHUNK):
            cross = jnp.dot(qs[c, ch], state.astype(BF16), preferred_element_type=F32) * xi_ref[c]
            tot = inner[c, ch] + cross
            outs[c, ch] = jnp.where(low, tot[:RET_CHUNK], tot[RET_CHUNK:])
            state = cd_ref[c] * state + incr[c, ch]
        r_ref[c] = state
    for c, ch in cells:
        g = g_ref[rows(ch), sl(c)].astype(F32)
        y = _head_rmsnorm(outs[c, ch], ones_bd) * (g * jax.nn.sigmoid(g))
        o_ref[rows(ch), sl(c)] = y.astype(BF16)


def _retention(proj3, tabs):
    b, seq, _ = proj3.shape
    dec, xi, zeta, cd = tabs
    ncol = RET_W // LANES
    qkvg = [pl.BlockSpec((None, RET_STEP, RET_W), functools.partial(lambda bi, ti, col: (bi, ti, col), col=col))
            for col in range(4)]
    const3 = lambda shape: pl.BlockSpec(shape, lambda bi, ti: (0, 0, 0))
    return pl.pallas_call(
        _ret_kernel,
        grid=(b, seq // RET_STEP),
        in_specs=qkvg + [const3((ncol, 2 * RET_CHUNK, LANES)), const3((ncol, 2 * RET_CHUNK, LANES)),
                         const3((ncol, RET_CHUNK, LANES)), const3((ncol, LANES, LANES))],
        out_specs=pl.BlockSpec((None, RET_STEP, RET_W), lambda bi, ti: (bi, ti, 0)),
        out_shape=jax.ShapeDtypeStruct((b, seq, RET_W), BF16),
        scratch_shapes=[pltpu.VMEM((ncol, LANES, LANES), F32)],
        compiler_params=_cparams(("arbitrary", "arbitrary")),
        name="retention",
    )(proj3, proj3, proj3, proj3, dec, xi, zeta, cd)


def _compress_kernel(a_ref, wtop_ref, wbot_ref, pe_ref, w2_ref, kg_ref, kc_ref, vc_ref, *, ngrp):
    half = CMP_LEN // 2
    ones_bd = _head_ones()
    for kv in range(2):
        top = jnp.zeros((ngrp, LANES), F32)
        bot = jnp.zeros((ngrp, LANES), F32)
        for l in range(half):
            x = a_ref[:, l * 2 * LANES + kv * LANES:l * 2 * LANES + (kv + 1) * LANES].astype(F32)
            top = top + jnp.dot((x + pe_ref[kv, l]).astype(BF16), wtop_ref[kv, l], preferred_element_type=F32)
            bot = bot + jnp.dot((x + pe_ref[kv, half + l]).astype(BF16), wbot_ref[kv, l],
                                preferred_element_type=F32)
        hid = jax.nn.gelu(top + pltpu.roll(bot, ngrp - 1, axis=0), approximate=True)
        out = jnp.dot(hid.astype(BF16), w2_ref[kv], preferred_element_type=F32)
        if kv == 0:
            kc_ref[...] = (_head_rmsnorm(out, ones_bd) * kg_ref[0:1, :]).astype(BF16)
        else:
            vc_ref[...] = out.astype(BF16)


def _compress(a, wtop, wbot, pe2, w2bd, kgain):
    b, ngrp, width = a.shape
    full = lambda arr: pl.BlockSpec(arr.shape, lambda bi: (0,) * arr.ndim)
    out_spec = pl.BlockSpec((None, ngrp, LANES), lambda bi: (bi, 0, 0))
    return pl.pallas_call(
        functools.partial(_compress_kernel, ngrp=ngrp),
        grid=(b,),
        in_specs=[pl.BlockSpec((None, ngrp, width), lambda bi: (bi, 0, 0)),
                  full(wtop), full(wbot), full(pe2), full(w2bd), full(kgain)],
        out_specs=[out_spec, out_spec],
        out_shape=[jax.ShapeDtypeStruct((b, ngrp, LANES), BF16)] * 2,
        compiler_params=_cparams(("arbitrary",)),
        name="nsa_compress",
    )(a, wtop, wbot, pe2, w2bd, kgain)


def _bias_kernel(rb_ref, d0_ref, d1_ref, tc_ref):
    h = pl.program_id(0)
    far = rb_ref[N_BUCKETS - 1, h]

    def rel(n):
        val = jnp.full(n.shape, (rb_ref[0, h] - far) * LOG2E, F32)
        for bkt in range(1, N_BUCKETS):
            val = jnp.where(n >= BUCKET_THR[bkt], (rb_ref[bkt, h] - far) * LOG2E, val)
        return val

    ql = lax.broadcasted_iota(jnp.int32, (QB, TK), 0)
    kl = lax.broadcasted_iota(jnp.int32, (QB, TK), 1)
    d = ql - kl
    d0_ref[...] = jnp.where(d >= 0, rel(d), NEG)
    d1_ref[...] = rel(d + TK)
    ql2 = lax.broadcasted_iota(jnp.int32, (QB, LANES), 0)
    m = lax.broadcasted_iota(jnp.int32, (QB, LANES), 1)
    dc = ql2 - CMP_STRIDE * (m - CMP_LEAD) - (CMP_LEN - 1)
    tc_ref[...] = jnp.where(dc >= 0, rel(dc), NEG).astype(BF16)


def _bias_tiles(rel_bias):
    return pl.pallas_call(
        _bias_kernel,
        grid=(N_NSA,),
        in_specs=[pl.BlockSpec(memory_space=pltpu.SMEM)],
        out_specs=[pl.BlockSpec((None, QB, TK), lambda h: (h, 0, 0)),
                   pl.BlockSpec((None, QB, TK), lambda h: (h, 0, 0)),
                   pl.BlockSpec((None, QB, LANES), lambda h: (h, 0, 0))],
        out_shape=[jax.ShapeDtypeStruct((N_NSA, QB, TK), F32),
                   jax.ShapeDtypeStruct((N_NSA, QB, TK), F32),
                   jax.ShapeDtypeStruct((N_NSA, QB, LANES), BF16)],
        compiler_params=_cparams(("arbitrary",)),
        name="t5_bias_tiles",
    )(rel_bias)


def _nsa_kernel(*refs, ncmp):
    programs = [_nsa_tile(sub, *refs, ncmp=ncmp) for sub in range(NQ_SUB)]
    for _ in range(NSA_PHASES):
        for prog in programs:
            next(prog)


def _nsa_tile(sub, q_ref, gt_ref, sk_ref, sv_ref, wk_ref, wv_ref, kc_ref, vc_ref, oh_ref, ov_ref, band_ref,
              d0_ref, d1_ref, tc_ref, e_ref, o_ref, acc_ref, m_ref, *, ncmp):
    i = NQ_SUB * pl.program_id(1) + sub
    t0 = i * QB
    ncol = NSA_W // LANES
    rows = slice(sub * QB, (sub + 1) * QB)
    slot = sub * ncol
    sq = 2 * QB
    low = lax.broadcasted_iota(jnp.int32, (QB, LANES), 1) < HEAD_DIM
    qs = [_stack_heads(q_ref[rows, c * LANES:(c + 1) * LANES], low) for c in range(ncol)]

    def stacked(ref, c):
        return jnp.concatenate([ref[c], ref[c + ncol]], axis=0)

    ones_col = jnp.ones((TK, LANES), BF16)

    n_tiles = NQ_SUB * pl.num_programs(1)
    band = band_ref[pl.ds(pl.multiple_of((QB // CMP_STRIDE) * (n_tiles - i), CMP_STRIDE), ncmp), :]
    kc_aug = jnp.concatenate([kc_ref[...], band], axis=1)
    vc_ov = jnp.concatenate([vc_ref[...], ov_ref[...]], axis=1)
    imp = jnp.zeros((sq, LANES), F32)
    o_cmp = []
    s_cmp = [_dot_nt(jnp.concatenate([qs[c], stacked(tc_ref, c)], axis=1), kc_aug) for c in range(ncol)]

    start_d = pl.multiple_of(t0, TK)
    start_p = pl.multiple_of(jnp.maximum(t0 - TK, 0), TK)
    start_e = pl.multiple_of(jnp.maximum(t0 - 2 * TK, 0), TK)
    s_win = [[_dot_nt(qs[c], wk_ref[pl.ds(st, TK), :]) for st in (start_d, start_p, start_e)] for c in range(ncol)]

    for c in range(ncol):
        s = s_cmp[c]
        mx = jnp.maximum(jnp.max(s, axis=-1, keepdims=True), -1e20)
        e = jnp.exp2(s - mx)
        den = jnp.sum(e, axis=-1, keepdims=True)
        inv = jnp.where(den > 0.0, 1.0 / den, 0.0)
        both = jnp.dot(e.astype(BF16), vc_ov, preferred_element_type=F32) * inv
        imp = imp + both[:, LANES:]
        o_cmp.append(jnp.where(low, both[:QB, :LANES], both[QB:, :LANES]))
    yield

    imp_t = imp.T
    blk = lax.broadcasted_iota(jnp.int32, (LANES, sq), 0)
    cur = (t0 + lax.broadcasted_iota(jnp.int32, (LANES, sq), 1) % QB) // SEL_BLOCK
    forced = (blk == 0) | (blk == cur) | (blk == cur - 1)
    imp_t = jnp.where(forced, -jnp.inf, jnp.where(blk > cur, -BIG, imp_t))
    blk_f = blk.astype(F32)

    val = imp_t
    for _ in range(N_SEL - 3):
        top = jnp.max(val, axis=0, keepdims=True)
        idx = jnp.min(jnp.where(val == top, blk_f, 1e6), axis=0, keepdims=True)
        val = jnp.where(blk_f == idx, -jnp.inf, val)
    msel = jnp.where(val == -jnp.inf, 0.0, NEG).T.astype(BF16)
    yield

    off_prev = jnp.where(i >= 1, 0.0, -NEG)
    off_edge = jnp.where(i >= 2, 0.0, -NEG)
    ql_t = lax.broadcasted_iota(jnp.int32, (sq, TK), 0) % QB
    kl_t = lax.broadcasted_iota(jnp.int32, (sq, TK), 1)
    edge = jnp.where(kl_t > ql_t, 0.0, NEG)
    wv_all = jnp.concatenate([
        jnp.concatenate([wv_ref[pl.ds(st, TK), :], ones_col], axis=1) for st in (start_d, start_p, start_e)], axis=0)
    o_win = []
    for c in range(ncol):
        s0 = s_win[c][0] + stacked(d0_ref, c)
        s1 = s_win[c][1] + stacked(d1_ref, c)
        s2 = s_win[c][2] + edge
        mx = jnp.maximum(jnp.maximum(jnp.max(s0, axis=-1, keepdims=True),
                                     jnp.max(s1, axis=-1, keepdims=True) - off_prev),
                         jnp.max(s2, axis=-1, keepdims=True) - off_edge)
        p = jnp.exp2(jnp.concatenate([s0 - mx, s1 - (mx + off_prev), s2 - (mx + off_edge)], axis=1)).astype(BF16)
        ow = jnp.dot(p, wv_all, preferred_element_type=F32)
        o_win.append(jnp.where(low, ow[:QB, :LANES] / ow[:QB, LANES:], ow[QB:, :LANES] / ow[QB:, LANES:]))
    yield

    qaug = [jnp.concatenate([qs[c], msel], axis=1) for c in range(ncol)]

    def sel_scores(c, start, width):
        kaug = jnp.concatenate([sk_ref[pl.ds(start, width), :], oh_ref[pl.ds(start, width), :]], axis=1)
        return _dot_nt(qaug[c], kaug)

    def sel_vaug(start, width):
        return jnp.concatenate([sv_ref[pl.ds(start, width), :], jnp.ones((width, LANES), BF16)], axis=1)

    def lanes(x, width):
        return jnp.concatenate([x] * (width // LANES), axis=1)

    s_near = [(sel_scores(c, start_d, TK), sel_scores(c, start_p, TK)) for c in range(ncol)]
    for c in range(ncol):
        s0 = s_near[c][0] + stacked(d0_ref, c)
        s1 = s_near[c][1] + stacked(d1_ref, c)
        mx = jnp.maximum(jnp.max(s0, axis=-1, keepdims=True), jnp.max(s1, axis=-1, keepdims=True) - off_prev)
        p0 = jnp.exp2(s0 - mx).astype(BF16)
        p1 = jnp.exp2(s1 - (mx + off_prev)).astype(BF16)
        acc_ref[slot + c] = (jnp.dot(p0, sel_vaug(start_d, TK), preferred_element_type=F32)
                             + jnp.dot(p1, sel_vaug(start_p, TK), preferred_element_type=F32))
        m_ref[slot + c] = jnp.broadcast_to(mx, (sq, LANES))
    yield

    def far_tile(kt0, tiles):
        start = pl.multiple_of(kt0 * TK, tiles * TK)
        width = tiles * TK
        vaug = sel_vaug(start, width)
        kaug = jnp.concatenate([sk_ref[pl.ds(start, width), :], oh_ref[pl.ds(start, width), :]], axis=1)
        cells = [(c, hd) for c in range(ncol) for hd in range(2)]
        scores = {(c, hd): _dot_nt(qaug[c][hd * QB:(hd + 1) * QB], kaug) for c, hd in cells}
        for c, hd in cells:
            part = slice(hd * QB, (hd + 1) * QB)
            s = scores[c, hd]
            m_old = m_ref[slot + c, part, :]
            m_new = jnp.maximum(m_old, jnp.max(s, axis=-1, keepdims=True))
            alpha = jnp.exp2(m_old - m_new)
            p = jnp.exp2(s - lanes(m_new, width)).astype(BF16)
            acc_ref[slot + c, part, :] = (lanes(alpha, 2 * LANES) * acc_ref[slot + c, part, :]
                                          + jnp.dot(p, vaug, preferred_element_type=F32))
            m_ref[slot + c, part, :] = m_new

    n_far = jnp.maximum(i - 1, 0)

    def far_loop(tiles):
        def body(kt, carry):
            far_tile(kt * tiles, tiles)
            return carry
        return body

    tiles, done = FAR_TILES, 0
    while tiles >= 1:
        count = n_far // tiles
        lax.fori_loop(done, count, far_loop(tiles), 0)
        tiles, done = tiles // 2, count * 2
    yield

    gate = jax.nn.sigmoid(gt_ref[rows, :].astype(F32)).astype(BF16)
    gexp = jnp.dot(gate, e_ref[...], preferred_element_type=F32)
    for c in range(ncol):
        acc = acc_ref[slot + c]
        o_sel = jnp.where(low, acc[:QB, :LANES] / acc[:QB, LANES:], acc[QB:, :LANES] / acc[QB:, LANES:])
        gc = lambda br: gexp[:, (br * ncol + c) * LANES:(br * ncol + c + 1) * LANES]
        o_ref[rows, c * LANES:(c + 1) * LANES] = (gc(0) * o_cmp[c] + gc(1) * o_sel + gc(2) * o_win[c]).astype(BF16)
    yield


def _nsa(proj3, kcmp, vcmp, onehot, overlap, band, d0, d1, tc, expand):
    b, seq, _ = proj3.shape
    ncmp = kcmp.shape[1]
    step = NQ_SUB * QB
    col = lambda width, cidx: pl.BlockSpec((None, step, width), lambda bi, qi: (bi, qi, cidx))
    res = lambda cidx: pl.BlockSpec((None, seq, LANES), lambda bi, qi: (bi, 0, cidx))
    full = lambda arr: pl.BlockSpec(arr.shape, lambda bi, qi: (0,) * arr.ndim)
    cmp_spec = pl.BlockSpec((None, ncmp, LANES), lambda bi, qi: (bi, 0, 0))
    return pl.pallas_call(
        functools.partial(_nsa_kernel, ncmp=ncmp),
        grid=(b, seq // step),
        in_specs=[col(NSA_W, COL_NQ * LANES // NSA_W), col(LANES, COL_GATE),
                  res(COL_SELK), res(COL_SELV), res(COL_WINK), res(COL_WINV),
                  cmp_spec, cmp_spec, full(onehot), full(overlap), full(band), full(d0), full(d1), full(tc),
                  full(expand)],
        out_specs=pl.BlockSpec((None, step, NSA_W), lambda bi, qi: (bi, qi, 0)),
        out_shape=jax.ShapeDtypeStruct((b, seq, NSA_W), BF16),
        scratch_shapes=[pltpu.VMEM((NQ_SUB * NSA_W // LANES, 2 * QB, 2 * LANES), F32),
                        pltpu.VMEM((NQ_SUB * NSA_W // LANES, 2 * QB, LANES), F32)],
        compiler_params=_cparams(("arbitrary", "arbitrary")),
        name="nsa_attention",
    )(proj3, proj3, proj3, proj3, proj3, proj3, kcmp, vcmp, onehot, overlap, band, d0, d1, tc, expand)


def _gmlp_kernel(u_ref, v_ref, ws_ref, b_ref, o_ref):
    lane = lax.broadcasted_iota(jnp.int32, (GM_CHUNK, LANES), 1)
    low = lane < GM_DIM
    row = lax.broadcasted_iota(jnp.int32, (GM_CHUNK, GM_CHUNK), 0)
    colm = lax.broadcasted_iota(jnp.int32, (GM_CHUNK, GM_CHUNK), 1)
    tril = colm <= row
    ones_bd = _head_ones()
    cells = [(cc, ch) for cc in range(GM_W // LANES) for ch in range(GM_STEP // GM_CHUNK)]
    sl = lambda cc: slice(cc * LANES, (cc + 1) * LANES)
    rows = lambda ch: slice(ch * GM_CHUNK, (ch + 1) * GM_CHUNK)
    w = [jnp.concatenate([jnp.where(tril, ws_ref[2 * cc], 0.0), jnp.where(tril, ws_ref[2 * cc + 1], 0.0)],
                         axis=0).astype(BF16) for cc in range(GM_W // LANES)]
    v = {(cc, ch): jax.nn.gelu(v_ref[rows(ch), sl(cc)].astype(F32), approximate=True) for cc, ch in cells}
    sums = {cell: _head_sum(v[cell] * v[cell], ones_bd) for cell in cells}
    vn = {cell: (v[cell] * lax.rsqrt(sums[cell] * (1.0 / GM_DIM) + EPS)).astype(BF16) for cell in cells}
    mixed = {(cc, ch): jnp.dot(w[cc], vn[cc, ch], preferred_element_type=F32) for cc, ch in cells}
    for cc, ch in cells:
        r = mixed[cc, ch]
        sv = jnp.where(low, r[:GM_CHUNK], r[GM_CHUNK:]) + b_ref[:, sl(cc)]
        u = jax.nn.gelu(u_ref[rows(ch), sl(cc)].astype(F32), approximate=True)
        o_ref[rows(ch), sl(cc)] = (u * sv).astype(BF16)


def _gmlp(proj3, ws, bexp):
    b, seq, _ = proj3.shape
    return pl.pallas_call(
        _gmlp_kernel,
        grid=(b, seq // GM_STEP),
        in_specs=[pl.BlockSpec((None, GM_STEP, GM_W), lambda bi, ti: (bi, ti, COL_GMU * LANES // GM_W)),
                  pl.BlockSpec((None, GM_STEP, GM_W), lambda bi, ti: (bi, ti, COL_GMV * LANES // GM_W)),
                  pl.BlockSpec((N_GM, GM_CHUNK, GM_CHUNK), lambda bi, ti: (0, 0, 0)),
                  pl.BlockSpec((GM_CHUNK, GM_W), lambda bi, ti: (0, 0))],
        out_specs=pl.BlockSpec((None, GM_STEP, GM_W), lambda bi, ti: (bi, ti, 0)),
        out_shape=jax.ShapeDtypeStruct((b, seq, GM_W), BF16),
        compiler_params=_cparams(("arbitrary", "arbitrary")),
        name="gmlp",
    )(proj3, proj3, ws, bexp)


FFN_CHUNK = 256
FFN_SPLIT = 1


def _ffn_kernel(x_ref, r_ref, n_ref, m_ref, wo_ref, g_ref, wgu_ref, wd_ref, o_ref, *, tm):
    sub_rows = tm // FFN_SPLIT
    for r0 in range(0, tm, sub_rows):
        rows = slice(r0, r0 + sub_rows)
        mixed = jnp.concatenate([r_ref[rows, :], n_ref[rows, :], m_ref[rows, :]], axis=1)
        x = x_ref[rows, :] + jnp.dot(mixed, wo_ref[...], preferred_element_type=F32)
        ms = jnp.mean(x * x, axis=-1, keepdims=True)
        h = (x * lax.rsqrt(ms + EPS) * g_ref[...]).astype(BF16)
        acc = x
        for f in range(0, D_FF, FFN_CHUNK):
            gate = jnp.dot(h, wgu_ref[:, f:f + FFN_CHUNK], preferred_element_type=F32)
            up = jnp.dot(h, wgu_ref[:, D_FF + f:D_FF + f + FFN_CHUNK], preferred_element_type=F32)
            act = (gate * jax.nn.sigmoid(gate) * up).astype(BF16)
            acc = acc + jnp.dot(act, wd_ref[f:f + FFN_CHUNK, :], preferred_element_type=F32)
        o_ref[rows, :] = acc


def _outproj_ffn(x2d, ret_o, nsa_o, gm_o, w_out, gain, wgu, wd, layer):
    m = x2d.shape[0]
    tm = 1024
    row = lambda width: pl.BlockSpec((tm, width), lambda i: (i, 0))
    const = lambda arr: pl.BlockSpec(arr.shape, lambda i: (0, 0), pipeline_mode=pl.Buffered(1))
    of_layer = lambda arr: pl.BlockSpec((None,) + arr.shape[1:], lambda i: (layer, 0, 0),
                                        pipeline_mode=pl.Buffered(1))
    return pl.pallas_call(
        functools.partial(_ffn_kernel, tm=tm),
        grid=(m // tm,),
        in_specs=[row(D_MODEL), row(RET_W), row(NSA_W), row(GM_W), const(w_out),
                  pl.BlockSpec((1, D_MODEL), lambda i: (0, 0)),
                  of_layer(wgu), of_layer(wd)],
        out_specs=row(D_MODEL),
        out_shape=jax.ShapeDtypeStruct((m, D_MODEL), F32),
        compiler_params=_cparams(("arbitrary",)),
        name="outproj_ffn",
    )(x2d, ret_o, nsa_o, gm_o, w_out, gain, wgu, wd)


def _nq_head_order():
    return [c + (N_NSA // N_KV) * half for c in range(NSA_W // LANES) for half in range(2)]


def _w_in_layout(w):
    wt = w.T
    off_nq = 4 * RET_W
    off_kv = off_nq + NSA_W
    off_gate = off_kv + 3 * 2 * N_KV * HEAD_DIM
    off_gm = off_gate + 3 * N_NSA
    heads = [wt[off_nq + h * HEAD_DIM:off_nq + (h + 1) * HEAD_DIM] for h in _nq_head_order()]
    main = jnp.concatenate([wt[:off_nq]] + heads + [wt[off_kv:off_gate + LANES]], axis=0).astype(BF16)
    return main, wt[off_gm:].astype(BF16)


def _w_out_layout(w):
    heads = [w[RET_W + h * HEAD_DIM:RET_W + (h + 1) * HEAD_DIM] for h in _nq_head_order()]
    return jnp.concatenate([w[:RET_W]] + heads + [w[RET_W + NSA_W:]], axis=0).astype(BF16)


def _gate_expand():
    e = np.zeros((LANES, 3 * NSA_W), np.float32)
    ncol = NSA_W // LANES
    for br in range(3):
        for c in range(ncol):
            for half in range(2):
                head = c + (N_NSA // N_KV) * half
                dst = (br * ncol + c) * LANES + half * HEAD_DIM
                e[head * 3 + br, dst:dst + HEAD_DIM] = 1.0
    return jnp.asarray(e, BF16)


def _rope_tables(seq):
    half = HEAD_DIM // 2
    inv = ROPE_THETA ** (-np.arange(half, dtype=np.float64) / half)
    ang = np.arange(seq, dtype=np.float64)[:, None] * inv[None, :]
    cos = np.tile(np.cos(ang), (1, LANES // half))
    sin = np.tile(np.concatenate([-np.sin(ang), np.sin(ang)], axis=1), (1, LANES // HEAD_DIM))
    return jnp.asarray(cos, F32), jnp.asarray(sin, F32)


def _retention_tables():
    c = RET_CHUNK
    log_gamma = np.log(1.0 - 2.0 ** (-5.0 - np.arange(N_RET, dtype=np.float64)))
    idx = np.arange(c, dtype=np.float64)
    diff = idx[:, None] - idx[None, :]
    decay = np.where(diff >= 0, np.exp(np.maximum(diff, 0.0)[None] * log_gamma[:, None, None]), 0.0)
    zeta = np.exp((c - 1 - idx)[None, :] * log_gamma[:, None])
    xi = np.exp((idx + 1)[None, :] * log_gamma[:, None])
    chunk_decay = np.exp(c * log_gamma)
    ncol = RET_W // LANES
    dec = decay.reshape(ncol, 2 * c, c)
    xi_st = np.broadcast_to(xi.reshape(ncol, 2 * c, 1), (ncol, 2 * c, LANES))
    zeta_l = np.repeat(zeta.reshape(ncol, 2, c).transpose(0, 2, 1), HEAD_DIM, axis=2)
    cd = np.broadcast_to(np.repeat(chunk_decay.reshape(ncol, 2), HEAD_DIM, axis=1)[:, :, None],
                         (ncol, LANES, LANES))
    return tuple(jnp.asarray(t, F32) for t in (dec, xi_st, zeta_l, cd))


def _overlap_table(seq):
    n_cmp = (seq - CMP_LEN) // CMP_STRIDE + 1
    n_slc = seq // SEL_BLOCK
    cs = np.arange(n_cmp)[:, None] * CMP_STRIDE
    ss = np.arange(n_slc)[None, :] * SEL_BLOCK
    ov = np.clip(np.minimum(cs + CMP_LEN, ss + SEL_BLOCK) - np.maximum(cs, ss), 0, None) // CMP_STRIDE
    full = np.zeros((seq // CMP_STRIDE, LANES), np.float32)
    full[:n_cmp, :n_slc] = ov
    return jnp.asarray(full, BF16)


def _cmp_band(seq):
    nq = seq // QB
    ncmp = seq // CMP_STRIDE
    x = np.arange(2 * ncmp)[:, None]
    m = np.arange(LANES)[None, :]
    base = (QB // CMP_STRIDE) * nq - CMP_LEAD
    band = np.where(m < LANES - 1, x == m + base, x >= m + base)
    return jnp.asarray(band.astype(np.float32), BF16)


def _block_onehot(seq):
    oh = (np.arange(seq)[:, None] // SEL_BLOCK) == np.arange(LANES)[None, :]
    return jnp.asarray(oh.astype(np.float32), BF16)


def _compress_weights(cmp_pe, cmp_w1, cmp_w2):
    def block_diag2(w):
        zero = jnp.zeros_like(w)
        return jnp.concatenate([jnp.concatenate([w, zero], axis=-1), jnp.concatenate([zero, w], axis=-1)],
                               axis=-2).astype(BF16)

    w1bd = block_diag2(cmp_w1.reshape(2, CMP_LEN, HEAD_DIM, HEAD_DIM))
    w2bd = block_diag2(cmp_w2)
    pe2 = jnp.tile(cmp_pe, (1, 1, N_KV))[:, :, None, :]
    return w1bd[:, :CMP_LEN // 2], w1bd[:, CMP_LEN // 2:], pe2, w2bd


def kernel(x, attn_norm, w_in, w_out, nsa_q_gain, nsa_k_gain, cmp_pe, cmp_w1, cmp_w2, gm_ws, gm_b, ffn_norm,
           w_gate_up, w_down, rel_bias):
    b, seq, _ = x.shape
    assert seq % (2 * TK) == 0 and N_SEL <= seq // SEL_BLOCK <= LANES, "selection blocks must fit one lane row"
    depth = w_in.shape[0]
    cos_tab, sin_tab = _rope_tables(seq)
    ret_tabs = _retention_tables()
    overlap = _overlap_table(seq)
    onehot = _block_onehot(seq)
    band = _cmp_band(seq)
    expand = _gate_expand()
    d0, d1, tc = _bias_tiles(rel_bias)
    ngrp = seq // CMP_STRIDE
    wgu_all = w_gate_up.astype(BF16)
    wd_all = w_down.astype(BF16)
    x2d = x.reshape(b * seq, D_MODEL)
    for l in range(depth):
        zero_row = jnp.zeros((5, LANES), F32)
        head_gains = jnp.concatenate([jnp.tile(nsa_q_gain[l] * (HEAD_DIM ** -0.5 * LOG2E), 2)[None],
                                      jnp.tile(nsa_k_gain[l, 1], 2)[None],
                                      jnp.tile(nsa_k_gain[l, 2], 2)[None], zero_row], axis=0)
        proj, cmp_grp = _inproj(x2d, attn_norm[l][None], *_w_in_layout(w_in[l]), cos_tab, sin_tab, head_gains, seq)
        proj3 = proj.reshape(b, seq, PROJ_W)
        ret_o = _retention(proj3, ret_tabs)
        cmp_in = cmp_grp.reshape(b, ngrp, CMP_STRIDE * 2 * LANES)
        wtop, wbot, pe2, w2bd = _compress_weights(cmp_pe[l], cmp_w1[l], cmp_w2[l])
        kgain = jnp.broadcast_to(jnp.tile(nsa_k_gain[l, 0], 2)[None], (8, LANES))
        kcmp, vcmp = _compress(cmp_in, wtop, wbot, pe2, w2bd, kgain)
        nsa_o = _nsa(proj3, kcmp, vcmp, onehot, overlap, band, d0, d1, tc, expand)
        gm_o = _gmlp(proj3, gm_ws[l], jnp.repeat(gm_b[l].T, GM_DIM, axis=1))
        x2d = _outproj_ffn(x2d, ret_o.reshape(b * seq, RET_W), nsa_o.reshape(b * seq, NSA_W),
                           gm_o.reshape(b * seq, GM_W), _w_out_layout(w_out[l]), ffn_norm[l][None],
                           wgu_all, wd_all, l)
    return x2d.reshape(b, seq, D_MODEL)
```
